```python
import jax
import jax.numpy as jnp
from jax import lax
import numpy as np

D_MODEL = 2048
BATCH = 2
SEQ = 16384
DEPTH = 2

GRID_W = 64
CTX_LEN = 256

MLA_HEADS = 8
Q_LORA = 512
KV_LORA = 256
QK_NOPE = 128
QK_ROPE = 64
V_HEAD = 128
ROPE_BASE = 10000.0
Q_BLOCK = 128
MLA_SCALE = (QK_NOPE + QK_ROPE) ** -0.5

ML_HEADS = 8
ML_DQK = 128
ML_DV = 128
ML_CHUNK = 128
ML_WIDTH = ML_HEADS * ML_DV

RG_WIDTH = 1024
RG_BLOCKS = 8
RG_BW = RG_WIDTH // RG_BLOCKS
RG_CONV = 4
RG_C = 8.0

N_BRANCH = 3
BRANCH_W = 1024
D_FF = 4 * D_MODEL
N_MOD = 6
EPS = 1e-6
F32 = jnp.float32

IN_LAYOUT = (
    ('mla_q', Q_LORA),
    ('mla_kv', KV_LORA),
    ('mla_kr', QK_ROPE),
    ('ml_q', ML_HEADS * ML_DQK),
    ('ml_k', ML_HEADS * ML_DQK),
    ('ml_v', ML_WIDTH),
    ('ml_o', ML_WIDTH),
    ('ml_g', 4 * ML_HEADS),
    ('rg_x', RG_WIDTH),
    ('rg_gate', RG_WIDTH),
    ('merge', N_BRANCH * D_MODEL),
)
D_IN = sum(w for _, w in IN_LAYOUT)

kernel_name = 'hybrid_mla_mlstm_rglru_prefix_dit_block'


def rms_norm(x, g):
    xf = x.astype(F32)
    y = xf * lax.rsqrt(jnp.mean(xf * xf, axis=-1, keepdims=True) + EPS)
    return (y * g.astype(F32)).astype(x.dtype)


def split_cols(z):
    out = {}
    off = 0
    for name, w in IN_LAYOUT:
        out[name] = z[..., off:off + w]
        off += w
    return out


def heads_first(a):
    return a.transpose(0, 2, 1, 3)


def rope2d(x, ang_r, ang_c):
    half = QK_ROPE // 2
    n = half // 2

    def rot(xp, ang):
        x1, x2 = xp[..., :n], xp[..., n:]
        cs = jnp.cos(ang).astype(xp.dtype)
        sn = jnp.sin(ang).astype(xp.dtype)
        return jnp.concatenate([x1 * cs - x2 * sn, x1 * sn + x2 * cs], axis=-1)

    return jnp.concatenate([rot(x[..., :half], ang_r), rot(x[..., half:], ang_c)], axis=-1)


def mla_q(z, p):
    B, T, _ = z['mla_q'].shape
    q = (rms_norm(z['mla_q'], p['q_norm_g']) @ p['w_uq']).reshape(B, T, MLA_HEADS, QK_NOPE + QK_ROPE)
    return q[..., :QK_NOPE], q[..., QK_NOPE:]


def mla_kv(z, p):
    B, T, _ = z['mla_kv'].shape
    kv = (rms_norm(z['mla_kv'], p['kv_norm_g']) @ p['w_ukv']).reshape(B, T, MLA_HEADS, QK_NOPE + V_HEAD)
    return kv[..., :QK_NOPE], z['mla_kr'], kv[..., QK_NOPE:]


def mla_attend(qn, qr, kn, kr, v):
    s = jnp.einsum('bhqd,bhkd->bhqk', qn, kn) + jnp.einsum('bhqr,bkr->bhqk', qr, kr)
    pr = jax.nn.softmax(s.astype(F32) * MLA_SCALE, axis=-1)
    return jnp.einsum('bhqk,bhkd->bhqd', pr.astype(v.dtype), v)


def mla_mixer(zx, zc, p, ang_r, ang_c, ctx_out):
    B, S, _ = zx['mla_q'].shape
    Lc = zc['mla_q'].shape[1]
    qn_x, qr_x = mla_q(zx, p)
    qr_x = rope2d(qr_x, ang_r[:, None, :], ang_c[:, None, :])
    kn_x, kr_x, v_x = mla_kv(zx, p)
    kr_x = rope2d(kr_x, ang_r, ang_c)
    kn_c, kr_c, v_c = mla_kv(zc, p)
    kn_c, v_c = heads_first(kn_c), heads_first(v_c)
    kn_all = jnp.concatenate([kn_c, heads_first(kn_x)], axis=2)
    kr_all = jnp.concatenate([kr_c, kr_x], axis=1)
    v_all = jnp.concatenate([v_c, heads_first(v_x)], axis=2)
    nb = S // Q_BLOCK

    def blocks(a):
        return heads_first(a).reshape(B, MLA_HEADS, nb, Q_BLOCK, a.shape[-1]).transpose(2, 0, 1, 3, 4)

    o = lax.map(lambda qs: mla_attend(qs[0], qs[1], kn_all, kr_all, v_all), (blocks(qn_x), blocks(qr_x)))
    y_x = o.transpose(1, 0, 3, 2, 4).reshape(B, S, MLA_HEADS * V_HEAD)
    y_c = None
    if ctx_out:
        qn_c, qr_c = mla_q(zc, p)
        oc = mla_attend(heads_first(qn_c), heads_first(qr_c), kn_c, kr_c, v_c)
        y_c = heads_first(oc).reshape(B, Lc, MLA_HEADS * V_HEAD)
    return y_x, y_c


def mlstm_chunked(q, k, v, ig, lf, state, want_out):
    B, H, T, dk = q.shape
    dv = v.shape[-1]
    L = ML_CHUNK
    nc = T // L
    qc = q.reshape(B, H, nc, L, dk)
    kc = k.reshape(B, H, nc, L, dk)
    vc = v.reshape(B, H, nc, L, dv)
    igc = ig.reshape(B, H, nc, L)
    bcum = jnp.cumsum(lf.reshape(B, H, nc, L), axis=-1)
    g = bcum[..., -1]
    w_end = g[..., None] - bcum + igc
    a = jnp.max(w_end, axis=-1)
    e = jnp.exp(w_end - a[..., None])
    dC = jnp.einsum('bhcl,bhclk,bhclv->bhckv', e, kc, vc)
    dn = jnp.einsum('bhcl,bhclk->bhck', e, kc)

    def step(carry, xs):
        C, n, m = carry
        g_c, a_c, dC_c, dn_c = xs
        m_new = jnp.maximum(g_c + m, a_c)
        dec = jnp.exp(g_c + m - m_new)
        inp = jnp.exp(a_c - m_new)
        C_new = dec[..., None, None] * C + inp[..., None, None] * dC_c
        n_new = dec[..., None] * n + inp[..., None] * dn_c
        return (C_new, n_new, m_new), (C, n, m)

    xs = (jnp.moveaxis(g, 2, 0), jnp.moveaxis(a, 2, 0), jnp.moveaxis(dC, 2, 0), jnp.moveaxis(dn, 2, 0))
    final, (Cs, ns, ms) = lax.scan(step, state, xs)
    if not want_out:
        return None, final
    Cs = jnp.moveaxis(Cs, 0, 2)
    ns = jnp.moveaxis(ns, 0, 2)
    ms = jnp.moveaxis(ms, 0, 2)
    inter = bcum + ms[..., None]
    dmat = bcum[..., :, None] - bcum[..., None, :] + igc[..., None, :]
    tril = jnp.tril(jnp.ones((L, L), dtype=bool))
    dmat = jnp.where(tril, dmat, -jnp.inf)
    m_t = jnp.maximum(inter, jnp.max(dmat, axis=-1))
    s = jnp.einsum('bhctk,bhcsk->bhcts', qc, kc) * jnp.exp(dmat - m_t[..., None])
    di = jnp.exp(inter - m_t)
    num = di[..., None] * jnp.einsum('bhctk,bhckv->bhctv', qc, Cs) + jnp.einsum('bhcts,bhcsv->bhctv', s, vc)
    den = di * jnp.einsum('bhctk,bhck->bhct', qc, ns) + jnp.sum(s, axis=-1)
    h = num / jnp.maximum(jnp.abs(den), jnp.exp(-m_t))[..., None]
    return h.reshape(B, H, T, dv), final


def mlstm_inputs(z, p, d):
    B, T, _ = z['ml_q'].shape

    def hd(a):
        return a.astype(F32).reshape(B, T, ML_HEADS, -1).transpose(0, 2, 1, 3)

    q = hd(z['ml_q'])
    k = hd(z['ml_k']) * (ML_DQK ** -0.5)
    v = hd(z['ml_v'])
    gts = z['ml_g'].astype(F32).reshape(B, T, 4, ML_HEADS) + p['mlstm_gate_b'].astype(F32)
    ig = gts[:, :, 2 * d].transpose(0, 2, 1)
    lf = jax.nn.log_sigmoid(gts[:, :, 2 * d + 1]).transpose(0, 2, 1)
    if d == 1:
        q, k, v, ig, lf = [jnp.flip(t, axis=2) for t in (q, k, v, ig, lf)]
    return q, k, v, ig, lf


def mlstm_mixer(zx, zc, p, ctx_out):
    B = zx['ml_q'].shape[0]
    hx_dirs = []
    hc_dirs = []
    for d in range(2):
        state0 = (jnp.zeros((B, ML_HEADS, ML_DQK, ML_DV), F32),
                  jnp.zeros((B, ML_HEADS, ML_DQK), F32),
                  jnp.zeros((B, ML_HEADS), F32))
        qc, kc, vc, igc, lfc = mlstm_inputs(zc, p, d)
        hc, st = mlstm_chunked(qc, kc, vc, igc, lfc, state0, ctx_out)
        qx, kx, vx, igx, lfx = mlstm_inputs(zx, p, d)
        hx, _ = mlstm_chunked(qx, kx, vx, igx, lfx, st, True)
        if d == 1:
            hx = jnp.flip(hx, axis=2)
            if ctx_out:
                hc = jnp.flip(hc, axis=2)
        hx_dirs.append(hx)
        hc_dirs.append(hc)

    def finish(h, z):
        Bh, H, T, dv = h.shape
        hn = h * lax.rsqrt(jnp.mean(h * h, axis=-1, keepdims=True) + EPS)
        hn = hn.transpose(0, 2, 1, 3).reshape(Bh, T, H * dv) * p['mlstm_norm_g'].astype(F32)
        return (hn * jax.nn.sigmoid(z['ml_o'].astype(F32))).astype(z['ml_o'].dtype)

    y_x = finish(hx_dirs[0] + hx_dirs[1], zx)
    y_c = finish(hc_dirs[0] + hc_dirs[1], zc) if ctx_out else None
    return y_x, y_c


def conv_centred(x, w, b):
    T = x.shape[1]
    lp = RG_CONV // 2
    xp = jnp.pad(x, ((0, 0), (lp, RG_CONV - 1 - lp), (0, 0)))
    y = xp[:, 0:T] * w[0] + b
    for j in range(1, RG_CONV):
        y = y + xp[:, j:j + T] * w[j]
    return y


def block_diag(x, w):
    B, T, _ = x.shape
    return jnp.einsum('btgi,gij->btgj', x.reshape(B, T, RG_BLOCKS, RG_BW), w).reshape(B, T, RG_WIDTH)


def lru_scan(a, b, h0):
    def comb(l, r):
        return l[0] * r[0], r[0] * l[1] + r[1]

    A, Hs = lax.associative_scan(comb, (a, b), axis=1)
    return Hs + A * h0[:, None, :]


def rglru_dir(xc, p, d, h0):
    xf = xc.astype(F32)
    r = jax.nn.sigmoid(block_diag(xf, p['rg_wa'][d].astype(F32)) + p['rg_ba'][d].astype(F32))
    i = jax.nn.sigmoid(block_diag(xf, p['rg_wx'][d].astype(F32)) + p['rg_bx'][d].astype(F32))
    log_a = -RG_C * r * jax.nn.softplus(-p['rg_lam'][d].astype(F32))
    a = jnp.exp(log_a)
    b = jnp.sqrt(-jnp.expm1(2.0 * log_a)) * (i * xf)
    if d == 1:
        a, b = jnp.flip(a, axis=1), jnp.flip(b, axis=1)
    h = lru_scan(a, b, h0)
    last = h[:, -1]
    if d == 1:
        h = jnp.flip(h, axis=1)
    return h, last


def rglru_mixer(zx, zc, p, ctx_out):
    xx = conv_centred(zx['rg_x'], p['rg_conv_w'], p['rg_conv_b'])
    xcc = conv_centred(zc['rg_x'], p['rg_conv_w'], p['rg_conv_b'])
    B = xx.shape[0]
    hx_dirs = []
    hc_dirs = []
    for d in range(2):
        hc, last = rglru_dir(xcc, p, d, jnp.zeros((B, RG_WIDTH), F32))
        hx, _ = rglru_dir(xx, p, d, last)
        hx_dirs.append(hx)
        hc_dirs.append(hc)
    y_x = ((hx_dirs[0] + hx_dirs[1]) * jax.nn.gelu(zx['rg_gate'].astype(F32))).astype(zx['rg_gate'].dtype)
    y_c = None
    if ctx_out:
        y_c = ((hc_dirs[0] + hc_dirs[1]) * jax.nn.gelu(zc['rg_gate'].astype(F32))).astype(zc['rg_gate'].dtype)
    return y_x, y_c


def modulate(h, g, shift, scale):
    return rms_norm(h, g) * (1.0 + scale) + shift


def merge_branches(ys, z, p):
    B, T, _ = z['merge'].shape
    g = jax.nn.sigmoid(z['merge'].astype(F32)).reshape(B, T, N_BRANCH, D_MODEL).astype(z['merge'].dtype)
    wb = p['w_branch']
    m = g[:, :, 0] * (ys[0] @ wb[0]) + g[:, :, 1] * (ys[1] @ wb[1]) + g[:, :, 2] * (ys[2] @ wb[2])
    return m @ p['w_out']


def sq_relu_mlp(h, p):
    return jnp.square(jax.nn.relu(h @ p['w_mlp1'])) @ p['w_mlp2']


def layer(x, cx, p, c, c_ctx, ang_r, ang_c, ctx_out):
    mx = jnp.split((jax.nn.silu(c) @ p['w_mod'] + p['b_mod'])[:, None, :], N_MOD, axis=-1)
    mc = jnp.split((jax.nn.silu(c_ctx) @ p['w_mod'] + p['b_mod'])[None, None, :], N_MOD, axis=-1)
    zx = split_cols(modulate(x, p['norm1_g'], mx[0], mx[1]) @ p['w_in'])
    zc = split_cols(modulate(cx, p['norm1_g'], mc[0], mc[1]) @ p['w_in'])
    a_x, a_c = mla_mixer(zx, zc, p, ang_r, ang_c, ctx_out)
    b_x, b_c = mlstm_mixer(zx, zc, p, ctx_out)
    r_x, r_c = rglru_mixer(zx, zc, p, ctx_out)
    x = x + mx[2] * merge_branches([a_x, b_x, r_x], zx, p)
    x = x + mx[5] * sq_relu_mlp(modulate(x, p['norm2_g'], mx[3], mx[4]), p)
    if not ctx_out:
        return x, None
    cx = cx + mc[2] * merge_branches([a_c, b_c, r_c], zc, p)
    cx = cx + mc[5] * sq_relu_mlp(modulate(cx, p['norm2_g'], mc[3], mc[4]), p)
    return x, cx


def setup_inputs(seed: int = 0) -> dict:
    key = jax.random.key(seed)
    ks = jax.random.split(key, 32)

    def nrm(k, shape, scale):
        return jax.random.normal(k, shape, F32) * scale

    fbase = jnp.linspace(3.0, 6.0, ML_HEADS)
    zero = jnp.zeros((ML_HEADS,), F32)
    gate_base = jnp.stack([zero, fbase, zero, fbase])
    u = jax.random.uniform(ks[21], (DEPTH, 2, RG_WIDTH), F32, minval=0.9, maxval=0.999)
    s = u ** (1.0 / RG_C)
    return {
        'x': nrm(ks[0], (BATCH, SEQ, D_MODEL), 1.0),
        'c': nrm(ks[1], (BATCH, D_MODEL), 1.0),
        'ctx': nrm(ks[2], (BATCH, CTX_LEN, D_MODEL), 1.0),
        'c_ctx': nrm(ks[3], (D_MODEL,), 1.0),
        'norm1_g': 1.0 + nrm(ks[4], (DEPTH, D_MODEL), 0.02),
        'norm2_g': 1.0 + nrm(ks[5], (DEPTH, D_MODEL), 0.02),
        'w_mod': nrm(ks[6], (DEPTH, D_MODEL, N_MOD * D_MODEL), 0.5 * D_MODEL ** -0.5),
        'b_mod': nrm(ks[7], (DEPTH, N_MOD * D_MODEL), 0.02),
        'w_in': nrm(ks[8], (DEPTH, D_MODEL, D_IN), D_MODEL ** -0.5),
        'q_norm_g': 1.0 + nrm(ks[9], (DEPTH, Q_LORA), 0.02),
        'w_uq': nrm(ks[10], (DEPTH, Q_LORA, MLA_HEADS * (QK_NOPE + QK_ROPE)), Q_LORA ** -0.5),
        'kv_norm_g': 1.0 + nrm(ks[11], (DEPTH, KV_LORA), 0.02),
        'w_ukv': nrm(ks[12], (DEPTH, KV_LORA, MLA_HEADS * (QK_NOPE + V_HEAD)), KV_LORA ** -0.5),
        'mlstm_gate_b': gate_base[None] + nrm(ks[13], (DEPTH, 4, ML_HEADS), 0.1),
        'mlstm_norm_g': 1.0 + nrm(ks[14], (DEPTH, ML_WIDTH), 0.02),
        'rg_conv_w': nrm(ks[15], (DEPTH, RG_CONV, RG_WIDTH), RG_CONV ** -0.5),
        'rg_conv_b': nrm(ks[16], (DEPTH, RG_WIDTH), 0.02),
        'rg_wa': nrm(ks[17], (DEPTH, 2, RG_BLOCKS, RG_BW, RG_BW), RG_BW ** -0.5),
        'rg_ba': nrm(ks[18], (DEPTH, 2, RG_WIDTH), 0.02),
        'rg_wx': nrm(ks[19], (DEPTH, 2, RG_BLOCKS, RG_BW, RG_BW), RG_BW ** -0.5),
        'rg_bx': nrm(ks[20], (DEPTH, 2, RG_WIDTH), 0.02),
        'rg_lam': jnp.log(s) - jnp.log1p(-s),
        'w_branch': nrm(ks[22], (DEPTH, N_BRANCH, BRANCH_W, D_MODEL), BRANCH_W ** -0.5),
        'w_out': nrm(ks[23], (DEPTH, D_MODEL, D_MODEL), D_MODEL ** -0.5),
        'w_mlp1': nrm(ks[24], (DEPTH, D_MODEL, D_FF), D_MODEL ** -0.5),
        'w_mlp2': nrm(ks[25], (DEPTH, D_FF, D_MODEL), D_FF ** -0.5),
        'final_g': 1.0 + nrm(ks[26], (D_MODEL,), 0.02),
    }


def reference(x, c, ctx, c_ctx, norm1_g, norm2_g, w_mod, b_mod, w_in, q_norm_g, w_uq, kv_norm_g, w_ukv,
              mlstm_gate_b, mlstm_norm_g, rg_conv_w, rg_conv_b, rg_wa, rg_ba, rg_wx, rg_bx, rg_lam,
              w_branch, w_out, w_mlp1, w_mlp2, final_g):
    S = x.shape[1]
    rows = S // GRID_W
    row = jnp.broadcast_to(jnp.arange(rows, dtype=F32)[:, None], (rows, GRID_W)).reshape(-1)
    col = jnp.broadcast_to(jnp.arange(GRID_W, dtype=F32)[None, :], (rows, GRID_W)).reshape(-1)
    n_freq = QK_ROPE // 4
    inv = ROPE_BASE ** (-jnp.arange(n_freq, dtype=F32) / n_freq)
    ang_r = row[:, None] * inv[None, :]
    ang_c = col[:, None] * inv[None, :]
    cx = ctx
    for l in range(DEPTH):
        p = {
            'norm1_g': norm1_g[l], 'norm2_g': norm2_g[l], 'w_mod': w_mod[l], 'b_mod': b_mod[l],
            'w_in': w_in[l], 'q_norm_g': q_norm_g[l], 'w_uq': w_uq[l], 'kv_norm_g': kv_norm_g[l],
            'w_ukv': w_ukv[l], 'mlstm_gate_b': mlstm_gate_b[l], 'mlstm_norm_g': mlstm_norm_g[l],
            'rg_conv_w': rg_conv_w[l], 'rg_conv_b': rg_conv_b[l], 'rg_wa': rg_wa[l], 'rg_ba': rg_ba[l],
            'rg_wx': rg_wx[l], 'rg_bx': rg_bx[l], 'rg_lam': rg_lam[l], 'w_branch': w_branch[l],
            'w_out': w_out[l], 'w_mlp1': w_mlp1[l], 'w_mlp2': w_mlp2[l],
        }
        x, cx = layer(x, cx, p, c, c_ctx, ang_r, ang_c, l < DEPTH - 1)
    return rms_norm(x, final_g)
```

```python
import functools
import math

import jax
import jax.numpy as jnp
from jax import lax
from jax.experimental import pallas as pl
from jax.experimental.pallas import tpu as pltpu

F32 = jnp.float32
BF16 = jnp.bfloat16

HEADS = 8
Q_LORA = 512
KV_LORA = 256
QK_NOPE = 128
QK_ROPE = 64
V_HEAD = 128
QK_DIM = QK_NOPE + QK_ROPE
ROPE_BASE = 10000.0
GRID_W = 64
ML_DQK = 128
ML_CHUNK = 128
WIDTH = 1024
RG_BLOCKS = 8
RG_BW = WIDTH // RG_BLOCKS
RG_CONV = 4
RG_C = 8.0
N_MOD = 6
EPS = 1e-6

LANE = 128
SUBLANE = 8
VT_ROWS = V_HEAD + 16

Z_Q = 0
Z_KV = Z_Q + Q_LORA
Z_KR = Z_KV + KV_LORA
Z_KRP = Z_KR + QK_ROPE
Z_MLA_END = Z_KRP + QK_ROPE
Z_G = Z_MLA_END
Z_MQ = Z_G + LANE
Z_MK = Z_MQ + WIDTH
Z_MV = Z_MK + WIDTH
Z_MO = Z_MV + WIDTH
Z_RX = Z_MO + WIDTH
Z_RG = Z_RX + WIDTH
Z_MERGE = Z_RG + WIDTH

Q_SCALE = (QK_DIM ** -0.5) * math.log2(math.e)
VMEM_LIMIT = 48 * 1024 * 1024


def _cparams(sem, vmem=None):
    return pltpu.CompilerParams(dimension_semantics=sem, vmem_limit_bytes=vmem)


def _pick(n, cands):
    for c in cands:
        if n % c == 0:
            return c
    raise ValueError(f"no tile in {cands} divides {n}")


def _rms(x, g):
    return x * lax.rsqrt(jnp.mean(x * x, axis=-1, keepdims=True) + EPS) * g


def _softplus(x):
    return jnp.maximum(x, 0.0) + jnp.log1p(jnp.exp(-jnp.abs(x)))


def _log_sigmoid(x):
    return jnp.minimum(x, 0.0) - jnp.log1p(jnp.exp(-jnp.abs(x)))


def _gelu_tanh(x):
    return 0.5 * x * (1.0 + jnp.tanh(math.sqrt(2.0 / math.pi) * (x + 0.044715 * (x * x * x))))


def _mod_kernel(c_ref, w_ref, b_ref, o_ref):
    c = c_ref[...]
    sc = c * jax.nn.sigmoid(c)
    o_ref[...] = jnp.dot(sc, w_ref[...], preferred_element_type=F32,
                         precision=lax.Precision.HIGHEST) + b_ref[...]


def _modulation(cc, w_mod, b_mod):
    rows, d = cc.shape
    n = w_mod.shape[1]
    tn = _pick(n, (1024, 512, 256, 128))
    return pl.pallas_call(
        _mod_kernel,
        grid=(n // tn,),
        in_specs=[pl.BlockSpec((rows, d), lambda j: (0, 0)),
                  pl.BlockSpec((d, tn), lambda j: (0, j)),
                  pl.BlockSpec((1, tn), lambda j: (0, j))],
        out_specs=pl.BlockSpec((rows, tn), lambda j: (0, j)),
        out_shape=jax.ShapeDtypeStruct((rows, n), F32),
        compiler_params=_cparams(("parallel",), VMEM_LIMIT),
        name="adaln_mod",
    )(cc, w_mod, b_mod.reshape(1, n))


def _inproj_kernel(x_ref, mod_ref, g_ref, w_ref, z_ref, xn_ref, *, d):
    @pl.when(pl.program_id(1) == 0)
    def _():
        y = _rms(x_ref[...], g_ref[...])
        shift = mod_ref[0, :, 0:d]
        scale = mod_ref[0, :, d:2 * d]
        xn_ref[...] = (y * (1.0 + scale) + shift).astype(BF16)

    z_ref[...] = jnp.dot(xn_ref[...], w_ref[...], preferred_element_type=F32)


def _in_projection(xu, mods3, g, w_in_p, mod_row, tm):
    r, d = xu.shape
    n = w_in_p.shape[1]
    tn = _pick(n, (1024, 512, 256, 128))
    return pl.pallas_call(
        functools.partial(_inproj_kernel, d=d),
        grid=(r // tm, n // tn),
        in_specs=[pl.BlockSpec((tm, d), lambda i, j: (i, 0)),
                  pl.BlockSpec((1, 1, N_MOD * d), lambda i, j: (mod_row(i), 0, 0)),
                  pl.BlockSpec((1, d), lambda i, j: (0, 0)),
                  pl.BlockSpec((d, tn), lambda i, j: (0, j))],
        out_specs=pl.BlockSpec((tm, tn), lambda i, j: (i, j)),
        out_shape=jax.ShapeDtypeStruct((r, n), F32),
        scratch_shapes=[pltpu.VMEM((tm, d), BF16)],
        compiler_params=_cparams(("parallel", "arbitrary"), VMEM_LIMIT),
        name="in_proj",
    )(xu, mods3, g.reshape(1, d), w_in_p)


def _mla_prep_kernel(z_ref, cos_ref, sin_ref, cost_ref, sint_ref, gq_ref, gkv_ref,
                     wqt_ref, wqrt_ref, wk_ref, wvt_ref, qt_ref, k_ref, vt_ref):
    tm = z_ref.shape[0]
    qn = _rms(z_ref[:, Z_Q:Z_Q + Q_LORA], gq_ref[...]).astype(BF16)
    kvn = _rms(z_ref[:, Z_KV:Z_KV + KV_LORA], gkv_ref[...]).astype(BF16)
    kr = (z_ref[:, Z_KR:Z_KR + QK_ROPE] * cos_ref[...]
          + z_ref[:, Z_KRP:Z_KRP + QK_ROPE] * sin_ref[...]).astype(BF16)
    kn = jnp.dot(kvn, wk_ref[...], preferred_element_type=F32)
    nt = (((1,), (1,)), ((), ()))
    ones = jnp.ones((VT_ROWS - V_HEAD, tm), BF16)
    for h in range(HEADS):
        q_t = lax.dot_general(wqt_ref[h], qn, nt, preferred_element_type=F32)
        qr_t = lax.dot_general(wqrt_ref[h], qn, nt, preferred_element_type=F32)
        q_rot = q_t[QK_NOPE:QK_DIM] * cost_ref[...] + qr_t * sint_ref[...]
        qt_ref[0, h, 0:QK_NOPE, :] = (q_t[0:QK_NOPE] * Q_SCALE).astype(BF16)
        qt_ref[0, h, QK_NOPE:QK_DIM, :] = (q_rot * Q_SCALE).astype(BF16)
        k_ref[0, h, :, 0:QK_NOPE] = kn[:, h * QK_NOPE:(h + 1) * QK_NOPE].astype(BF16)
        k_ref[0, h, :, QK_NOPE:QK_DIM] = kr
        v_t = lax.dot_general(wvt_ref[h], kvn, nt, preferred_element_type=F32)
        vt_ref[0, h, 0:V_HEAD, :] = v_t.astype(BF16)
        vt_ref[0, h, V_HEAD:VT_ROWS, :] = ones


def _mla_prep(z, tabs, gq, gkv, wqt, wqrt, wk, wvt, b, s, lc, tm):
    r = z.shape[0]
    cos_u, sin_u, cost_u, sint_u = tabs
    nxt, nct = s // tm, lc // tm
    ltot = s + lc

    def bidx(i):
        j = i - b * nxt
        return jnp.where(i < b * nxt, i // nxt, j // nct)

    def sblk(i):
        j = i - b * nxt
        return jnp.where(i < b * nxt, i % nxt, nxt + j % nct)

    full = lambda shape: pl.BlockSpec(shape, lambda i: (0,) * len(shape))
    return pl.pallas_call(
        _mla_prep_kernel,
        grid=(r // tm,),
        in_specs=[pl.BlockSpec((tm, Z_MLA_END), lambda i: (i, 0)),
                  pl.BlockSpec((tm, QK_ROPE), lambda i: (i, 0)),
                  pl.BlockSpec((tm, QK_ROPE), lambda i: (i, 0)),
                  pl.BlockSpec((QK_ROPE, tm), lambda i: (0, i)),
                  pl.BlockSpec((QK_ROPE, tm), lambda i: (0, i)),
                  full((1, Q_LORA)), full((1, KV_LORA)),
                  full(wqt.shape), full(wqrt.shape), full(wk.shape), full(wvt.shape)],
        out_specs=[pl.BlockSpec((1, HEADS, QK_DIM, tm), lambda i: (bidx(i), 0, 0, sblk(i))),
                   pl.BlockSpec((1, HEADS, tm, QK_DIM), lambda i: (bidx(i), 0, sblk(i), 0)),
                   pl.BlockSpec((1, HEADS, VT_ROWS, tm), lambda i: (bidx(i), 0, 0, sblk(i)))],
        out_shape=[jax.ShapeDtypeStruct((b, HEADS, QK_DIM, ltot), BF16),
                   jax.ShapeDtypeStruct((b, HEADS, ltot, QK_DIM), BF16),
                   jax.ShapeDtypeStruct((b, HEADS, VT_ROWS, ltot), BF16)],
        compiler_params=_cparams(("parallel",), VMEM_LIMIT),
        name="mla_prep",
    )(z, cos_u, sin_u, cost_u, sint_u, gq.reshape(1, -1), gkv.reshape(1, -1), wqt, wqrt, wk, wvt)


def _attn_kernel(qt_ref, k_ref, vt_ref, o_ref, *, tk, nk):
    qt = qt_ref[0, 0]
    tq = qt.shape[1]

    def body(c, carry):
        m, acc = carry
        off = pl.multiple_of(c * tk, tk)
        s = jnp.dot(k_ref[0, 0, pl.ds(off, tk), :], qt, preferred_element_type=F32)
        m_new = jnp.maximum(m, jnp.max(s, axis=0, keepdims=True))
        p = jnp.exp2(s - m_new).astype(BF16)
        alpha = jnp.exp2(m - m_new)
        pv = jnp.dot(vt_ref[0, 0, :, pl.ds(off, tk)], p, preferred_element_type=F32)
        return m_new, alpha * acc + pv

    m0 = jnp.full((1, tq), -jnp.inf, F32)
    acc0 = jnp.zeros((VT_ROWS, tq), F32)
    _, acc = lax.fori_loop(0, nk, body, (m0, acc0))
    o = acc[0:V_HEAD] / acc[V_HEAD:V_HEAD + 1]
    o_ref[...] = o.T.astype(o_ref.dtype)


def _attn_kernel_aliased(qt_ref, k_ref, vt_ref, prev_ref, o_ref, *, tk, nk):
    del prev_ref
    _attn_kernel(qt_ref, k_ref, vt_ref, o_ref, tk=tk, nk=nk)


def _attn_call(qt, k, vt, *, b, nq, tq, q_blk0, kv_len, kv_blk0, out_rows, row_blk, prev):
    tk = _pick(kv_len, (1280, 1024, 512, 256, 128))
    nk = kv_len // tk
    in_specs = [pl.BlockSpec((1, 1, QK_DIM, tq), lambda bb, h, i: (bb, h, 0, q_blk0 + i)),
                pl.BlockSpec((1, 1, kv_len, QK_DIM), lambda bb, h, i: (bb, h, kv_blk0, 0)),
                pl.BlockSpec((1, 1, VT_ROWS, kv_len), lambda bb, h, i: (bb, h, 0, kv_blk0))]
    args = [qt, k, vt]
    kernel = functools.partial(_attn_kernel, tk=tk, nk=nk)
    aliases = {}
    if prev is not None:
        in_specs.append(pl.BlockSpec(memory_space=pl.ANY))
        args.append(prev)
        aliases = {3: 0}
        kernel = functools.partial(_attn_kernel_aliased, tk=tk, nk=nk)
    return pl.pallas_call(
        kernel,
        grid=(b, HEADS, nq),
        in_specs=in_specs,
        out_specs=pl.BlockSpec((tq, V_HEAD), lambda bb, h, i: (row_blk(bb, i), h)),
        out_shape=jax.ShapeDtypeStruct((out_rows, WIDTH), BF16),
        input_output_aliases=aliases,
        compiler_params=_cparams(("parallel", "parallel", "arbitrary"), VMEM_LIMIT),
        name="mla_attn",
    )(*args)


def _mlstm_gates_kernel(z_ref, b_ref, gc_ref, gr_ref):
    g = z_ref[...] + b_ref[...]
    n = g.shape[0]
    lane = lax.broadcasted_iota(jnp.int32, g.shape, 1)
    row = lax.broadcasted_iota(jnp.int32, g.shape, 0)
    lf = _log_sigmoid(g)
    pre = lf
    suf = lf
    k = 1
    while k < n:
        pre = pre + jnp.where(row >= k, pltpu.roll(pre, k, 0), 0.0)
        suf = suf + jnp.where(row < n - k, pltpu.roll(suf, n - k, 0), 0.0)
        k *= 2
    fwd_f = (lane >= HEADS) & (lane < 2 * HEADS)
    bwd_f = (lane >= 3 * HEADS) & (lane < 4 * HEADS)
    gc = jnp.where(fwd_f, pre, jnp.where(bwd_f, suf, g))
    gc_ref[...] = gc
    gr_ref[0] = gc.T


def _mlstm_gates(z, bias_pad):
    r = z.shape[0]
    nchunk = r // ML_CHUNK
    return pl.pallas_call(
        _mlstm_gates_kernel,
        grid=(nchunk,),
        in_specs=[pl.BlockSpec((ML_CHUNK, LANE), lambda i: (i, Z_G // LANE)),
                  pl.BlockSpec((1, LANE), lambda i: (0, 0))],
        out_specs=[pl.BlockSpec((ML_CHUNK, LANE), lambda i: (i, 0)),
                   pl.BlockSpec((1, LANE, ML_CHUNK), lambda i: (i, 0, 0))],
        out_shape=[jax.ShapeDtypeStruct((r, LANE), F32),
                   jax.ShapeDtypeStruct((nchunk, LANE, ML_CHUNK), F32)],
        compiler_params=_cparams(("parallel",)),
        name="mlstm_gates",
    )(z, bias_pad)


def _mlstm_kernel(qf_ref, kf_ref, vf_ref, gcf_ref, grf_ref,
                  qb_ref, kb_ref, vb_ref, gcb_ref, grb_ref,
                  hf_ref, hb_ref, c_ref, m_ref):
    L = ML_CHUNK

    @pl.when(pl.program_id(1) == 0)
    def _():
        c_ref[...] = jnp.zeros(c_ref.shape, F32)
        m_ref[...] = jnp.zeros(m_ref.shape, F32)

    row = lax.broadcasted_iota(jnp.int32, (L, L), 0)
    col = lax.broadcasted_iota(jnp.int32, (L, L), 1)
    lane = lax.broadcasted_iota(jnp.int32, (L, LANE), 1)
    ones_col = jnp.where(lane == 0, 1.0, 0.0).astype(BF16)
    nt = (((1,), (1,)), ((), ()))
    tn = (((0,), (0,)), ((), ()))
    dirs = ((qf_ref, kf_ref, vf_ref, gcf_ref, grf_ref, hf_ref, col <= row),
            (qb_ref, kb_ref, vb_ref, gcb_ref, grb_ref, hb_ref, col >= row))
    for d, (q_ref, k_ref, v_ref, gc_ref, gr_ref, o_ref, mask) in enumerate(dirs):
        gc = gc_ref[...]
        gr = gr_ref[0]
        for h in range(HEADS):
            li = 2 * HEADS * d + h
            lb = li + HEADS
            sl = slice(h * ML_DQK, (h + 1) * ML_DQK)
            bcol = gc[:, lb:lb + 1]
            igcol = gc[:, li:li + 1]
            rrow = gr[li:li + 1, :] - gr[lb:lb + 1, :]
            g_tot = bcol[L - 1:L] if d == 0 else bcol[0:1]
            wend = g_tot - bcol + igcol
            a = jnp.max(wend, axis=0, keepdims=True)
            e = jnp.exp(wend - a)
            m_old = m_ref[d * HEADS + h][0:1, 0:1]
            q = q_ref[:, sl].astype(BF16)
            k_s = k_ref[:, sl] * (ML_DQK ** -0.5)
            vext = jnp.concatenate([v_ref[:, sl].astype(BF16), ones_col], axis=1)
            d_c = lax.dot_general((k_s * e).astype(BF16), vext, tn, preferred_element_type=F32)
            c_old = c_ref[d * HEADS + h]
            inter = bcol + m_old
            dmat = jnp.where(mask, bcol + rrow, -jnp.inf)
            m_t = jnp.maximum(inter, jnp.max(dmat, axis=1, keepdims=True))
            qk = lax.dot_general(q, k_s.astype(BF16), nt, preferred_element_type=F32)
            s_mat = (qk * jnp.exp(dmat - m_t)).astype(BF16)
            di = jnp.exp(inter - m_t)
            res = (di * jnp.dot(q, c_old.astype(BF16), preferred_element_type=F32)
                   + jnp.dot(s_mat, vext, preferred_element_type=F32))
            den = res[:, ML_DQK:ML_DQK + 1]
            o_ref[:, sl] = res[:, 0:ML_DQK] / jnp.maximum(jnp.abs(den), jnp.exp(-m_t))
            m_new = jnp.maximum(g_tot + m_old, a)
            dec = jnp.exp(g_tot + m_old - m_new)
            inp = jnp.exp(a - m_new)
            c_ref[d * HEADS + h] = dec * c_old + inp * d_c
            m_ref[d * HEADS + h] = jnp.broadcast_to(m_new, (SUBLANE, LANE))


def _mlstm(z, gc, gr, b, s, lc):
    r = z.shape[0]
    L = ML_CHUNK
    nxc, ncc = s // L, lc // L
    x0 = lambda bb: bb * nxc
    c0 = lambda bb: b * nxc + bb * ncc

    def blk_f(bb, st):
        return jnp.where(st < ncc, c0(bb) + st, x0(bb) + st - ncc)

    def blk_b(bb, st):
        return jnp.where(st < ncc, c0(bb) + ncc - 1 - st, x0(bb) + nxc - 1 - (st - ncc))

    def zspec(blk, cb):
        return pl.BlockSpec((L, WIDTH), lambda bb, st: (blk(bb, st), cb))

    def dir_specs(blk):
        return [zspec(blk, Z_MQ // WIDTH), zspec(blk, Z_MK // WIDTH), zspec(blk, Z_MV // WIDTH),
                pl.BlockSpec((L, LANE), lambda bb, st: (blk(bb, st), 0)),
                pl.BlockSpec((1, LANE, L), lambda bb, st: (blk(bb, st), 0, 0))]

    return pl.pallas_call(
        _mlstm_kernel,
        grid=(b, nxc + ncc),
        in_specs=dir_specs(blk_f) + dir_specs(blk_b),
        out_specs=[pl.BlockSpec((L, WIDTH), lambda bb, st: (blk_f(bb, st), 0)),
                   pl.BlockSpec((L, WIDTH), lambda bb, st: (blk_b(bb, st), 0))],
        out_shape=[jax.ShapeDtypeStruct((r, WIDTH), F32), jax.ShapeDtypeStruct((r, WIDTH), F32)],
        scratch_shapes=[pltpu.VMEM((2 * HEADS, ML_DQK, 2 * LANE), F32),
                        pltpu.VMEM((2 * HEADS, SUBLANE, LANE), F32)],
        compiler_params=_cparams(("arbitrary", "arbitrary"), VMEM_LIMIT),
        name="mlstm",
    )(z, z, z, gc, gr, z, z, z, gc, gr)


def _rg_ab_kernel(cur_ref, prev_ref, next_ref, cw_ref, cb_ref, w_ref, ba_ref, bx_ref, lam_ref,
                  af_ref, bf_ref, ab_ref, bb_ref, xe_ref, *, tm, s_len, c_len, rows_x):
    row0 = pl.program_id(0) * tm
    in_x = row0 < rows_x
    seq = jnp.where(in_x, s_len, c_len)
    off = jnp.where(in_x, row0, row0 - rows_x)
    first = lax.rem(off, seq) == 0
    last = lax.rem(off + tm, seq) == 0
    xe_ref[0:SUBLANE, :] = jnp.where(first, 0.0, prev_ref[...])
    xe_ref[SUBLANE:SUBLANE + tm, :] = cur_ref[...]
    xe_ref[SUBLANE + tm:2 * SUBLANE + tm, :] = jnp.where(last, 0.0, next_ref[...])
    lp = RG_CONV // 2
    xc = cb_ref[...] + cw_ref[0:1, :] * xe_ref[pl.ds(SUBLANE - lp, tm), :]
    for j in range(1, RG_CONV):
        xc = xc + cw_ref[j:j + 1, :] * xe_ref[pl.ds(SUBLANE - lp + j, tm), :]
    outs = ((af_ref, bf_ref), (ab_ref, bb_ref))
    for g in range(RG_BLOCKS):
        sl = slice(g * RG_BW, (g + 1) * RG_BW)
        xg = xc[:, sl]
        o = jnp.dot(xg.astype(BF16), w_ref[g], preferred_element_type=F32)
        for d in range(2):
            r = jax.nn.sigmoid(o[:, (2 * d) * RG_BW:(2 * d + 1) * RG_BW] + ba_ref[d:d + 1, sl])
            i = jax.nn.sigmoid(o[:, (2 * d + 1) * RG_BW:(2 * d + 2) * RG_BW] + bx_ref[d:d + 1, sl])
            log_a = (-RG_C) * r * _softplus(-lam_ref[d:d + 1, sl])
            a = jnp.exp(log_a)
            one_m_a2 = -jnp.tanh(log_a) * (a * a + 1.0)
            outs[d][0][:, sl] = a
            outs[d][1][:, sl] = jnp.sqrt(one_m_a2) * (i * xg)


def _rg_coeffs(z, cw, cb, w_rg, ba, bx, lam, b, s, lc, tm):
    r = z.shape[0]
    per = tm // SUBLANE
    nblk8 = r // SUBLANE
    cbk = Z_RX // WIDTH
    full = lambda shape: pl.BlockSpec(shape, lambda i: (0,) * len(shape))
    kernel = functools.partial(_rg_ab_kernel, tm=tm, s_len=s, c_len=lc, rows_x=b * s)
    o_spec = pl.BlockSpec((tm, WIDTH), lambda i: (i, 0))
    o_shape = jax.ShapeDtypeStruct((r, WIDTH), F32)
    return pl.pallas_call(
        kernel,
        grid=(r // tm,),
        in_specs=[pl.BlockSpec((tm, WIDTH), lambda i: (i, cbk)),
                  pl.BlockSpec((SUBLANE, WIDTH), lambda i: (jnp.maximum(i * per - 1, 0), cbk)),
                  pl.BlockSpec((SUBLANE, WIDTH), lambda i: (jnp.minimum((i + 1) * per, nblk8 - 1), cbk)),
                  full((RG_CONV, WIDTH)), full((1, WIDTH)), full(w_rg.shape),
                  full((2, WIDTH)), full((2, WIDTH)), full((2, WIDTH))],
        out_specs=[o_spec] * 4,
        out_shape=[o_shape] * 4,
        scratch_shapes=[pltpu.VMEM((tm + 2 * SUBLANE, WIDTH), F32)],
        compiler_params=_cparams(("parallel",), VMEM_LIMIT),
        name="rglru_coeffs",
    )(z, z, z, cw, cb.reshape(1, WIDTH), w_rg, ba, bx, lam)


def _rg_scan_kernel(af_ref, bf_ref, ab_ref, bb_ref, hf_ref, hb_ref, sf_ref, sb_ref, *, tt):
    @pl.when(pl.program_id(1) == 0)
    def _():
        sf_ref[...] = jnp.zeros(sf_ref.shape, F32)
        sb_ref[...] = jnp.zeros(sb_ref.shape, F32)

    def body(t, carry):
        hf, hb = carry
        hf = af_ref[pl.ds(t, 1), :] * hf + bf_ref[pl.ds(t, 1), :]
        hf_ref[pl.ds(t, 1), :] = hf
        tb = tt - 1 - t
        hb = ab_ref[pl.ds(tb, 1), :] * hb + bb_ref[pl.ds(tb, 1), :]
        hb_ref[pl.ds(tb, 1), :] = hb
        return hf, hb

    hf, hb = lax.fori_loop(0, tt, body, (sf_ref[...], sb_ref[...]), unroll=8)
    sf_ref[...] = hf
    sb_ref[...] = hb


def _rg_scan(af, bf, ab, bb, b, s, lc, tt):
    r = af.shape[0]
    nxt, nct = s // tt, lc // tt
    x0 = lambda bi: bi * nxt
    c0 = lambda bi: b * nxt + bi * nct

    def blk_f(bi, st):
        return jnp.where(st < nct, c0(bi) + st, x0(bi) + st - nct)

    def blk_b(bi, st):
        return jnp.where(st < nct, c0(bi) + nct - 1 - st, x0(bi) + nxt - 1 - (st - nct))

    sf = pl.BlockSpec((tt, WIDTH), lambda bi, st: (blk_f(bi, st), 0))
    sb = pl.BlockSpec((tt, WIDTH), lambda bi, st: (blk_b(bi, st), 0))
    o_shape = jax.ShapeDtypeStruct((r, WIDTH), F32)
    return pl.pallas_call(
        functools.partial(_rg_scan_kernel, tt=tt),
        grid=(b, nxt + nct),
        in_specs=[sf, sf, sb, sb],
        out_specs=[sf, sb],
        out_shape=[o_shape, o_shape],
        scratch_shapes=[pltpu.VMEM((1, WIDTH), F32), pltpu.VMEM((1, WIDTH), F32)],
        compiler_params=_cparams(("arbitrary", "arbitrary")),
        name="rglru_scan",
    )(af, bf, ab, bb)


def _merge_kernel(ya_ref, mhf_ref, mhb_ref, zo_ref, ng_ref, rhf_ref, rhb_ref, zg_ref,
                  zm0_ref, zm1_ref, zm2_ref, w0_ref, w1_ref, w2_ref, m_ref, yb_ref, yr_ref):
    @pl.when(pl.program_id(1) == 0)
    def _():
        hsum = mhf_ref[...] + mhb_ref[...]
        for h in range(HEADS):
            sl = slice(h * V_HEAD, (h + 1) * V_HEAD)
            hn = _rms(hsum[:, sl], ng_ref[:, sl])
            yb_ref[:, sl] = (hn * jax.nn.sigmoid(zo_ref[:, sl])).astype(BF16)
        yr_ref[...] = ((rhf_ref[...] + rhb_ref[...]) * _gelu_tanh(zg_ref[...])).astype(BF16)

    m = (jax.nn.sigmoid(zm0_ref[...]) * jnp.dot(ya_ref[...], w0_ref[0], preferred_element_type=F32)
         + jax.nn.sigmoid(zm1_ref[...]) * jnp.dot(yb_ref[...], w1_ref[0], preferred_element_type=F32)
         + jax.nn.sigmoid(zm2_ref[...]) * jnp.dot(yr_ref[...], w2_ref[0], preferred_element_type=F32))
    m_ref[...] = m.astype(BF16)


def _merge(rows, d, ya, mhf, mhb, z, ng, rhf, rhb, wb, tm):
    tn = _pick(d, (1024, 512, 256, 128))
    wide = lambda: pl.BlockSpec((tm, WIDTH), lambda i, j: (i, 0))
    zblk = lambda cb: pl.BlockSpec((tm, WIDTH), lambda i, j: (i, cb))
    zm = lambda br: pl.BlockSpec((tm, tn), lambda i, j: (i, (Z_MERGE + br * d) // tn + j))
    wspec = lambda br: pl.BlockSpec((1, WIDTH, tn), lambda i, j: (br, 0, j))
    return pl.pallas_call(
        _merge_kernel,
        grid=(rows // tm, d // tn),
        in_specs=[wide(), wide(), wide(), zblk(Z_MO // WIDTH),
                  pl.BlockSpec((1, WIDTH), lambda i, j: (0, 0)),
                  wide(), wide(), zblk(Z_RG // WIDTH),
                  zm(0), zm(1), zm(2), wspec(0), wspec(1), wspec(2)],
        out_specs=pl.BlockSpec((tm, tn), lambda i, j: (i, j)),
        out_shape=jax.ShapeDtypeStruct((rows, d), BF16),
        scratch_shapes=[pltpu.VMEM((tm, WIDTH), BF16), pltpu.VMEM((tm, WIDTH), BF16)],
        compiler_params=_cparams(("parallel", "arbitrary"), VMEM_LIMIT),
        name="branch_merge",
    )(ya, mhf, mhb, z, ng.reshape(1, WIDTH), rhf, rhb, z, z, z, z, wb, wb, wb)


def _outproj_kernel(m_ref, w_ref, x_ref, gate_ref, o_ref):
    o_ref[...] = x_ref[...] + gate_ref[0] * jnp.dot(m_ref[...], w_ref[...], preferred_element_type=F32)


def _out_projection(rows, xu, m, w_out, mods3, mod_row, tm):
    d = xu.shape[1]
    tn = _pick(d, (1024, 512, 256, 128))
    return pl.pallas_call(
        _outproj_kernel,
        grid=(rows // tm, d // tn),
        in_specs=[pl.BlockSpec((tm, d), lambda i, j: (i, 0)),
                  pl.BlockSpec((d, tn), lambda i, j: (0, j)),
                  pl.BlockSpec((tm, tn), lambda i, j: (i, j)),
                  pl.BlockSpec((1, 1, tn), lambda i, j: (mod_row(i), 0, (2 * d) // tn + j))],
        out_specs=pl.BlockSpec((tm, tn), lambda i, j: (i, j)),
        out_shape=jax.ShapeDtypeStruct((rows, d), F32),
        compiler_params=_cparams(("parallel", "parallel"), VMEM_LIMIT),
        name="out_proj",
    )(m, w_out, xu, mods3)


def _mlp_kernel(x_ref, mod_ref, g_ref, w1_ref, w2_ref, fg_ref, o_ref, xn_ref, acc_ref, *, d, final):
    j = pl.program_id(1)

    @pl.when(j == 0)
    def _():
        y = _rms(x_ref[...], g_ref[...])
        shift = mod_ref[0, :, 3 * d:4 * d]
        scale = mod_ref[0, :, 4 * d:5 * d]
        xn_ref[...] = (y * (1.0 + scale) + shift).astype(BF16)
        acc_ref[...] = jnp.zeros(acc_ref.shape, F32)

    h = jnp.dot(xn_ref[...], w1_ref[...], preferred_element_type=F32)
    h = jnp.square(jnp.maximum(h, 0.0)).astype(BF16)
    acc_ref[...] += jnp.dot(h, w2_ref[...], preferred_element_type=F32)

    @pl.when(j == pl.num_programs(1) - 1)
    def _():
        out = x_ref[...] + mod_ref[0, :, 5 * d:6 * d] * acc_ref[...]
        if final:
            out = _rms(out, fg_ref[...])
        o_ref[...] = out


def _mlp(rows, x1, mods3, g, w1, w2, fg, mod_row, tm, final):
    d = x1.shape[1]
    dff = w1.shape[1]
    tf = _pick(dff, (512, 256, 128))
    return pl.pallas_call(
        functools.partial(_mlp_kernel, d=d, final=final),
        grid=(rows // tm, dff // tf),
        in_specs=[pl.BlockSpec((tm, d), lambda i, j: (i, 0)),
                  pl.BlockSpec((1, 1, N_MOD * d), lambda i, j: (mod_row(i), 0, 0)),
                  pl.BlockSpec((1, d), lambda i, j: (0, 0)),
                  pl.BlockSpec((d, tf), lambda i, j: (0, j)),
                  pl.BlockSpec((tf, d), lambda i, j: (j, 0)),
                  pl.BlockSpec((1, d), lambda i, j: (0, 0))],
        out_specs=pl.BlockSpec((tm, d), lambda i, j: (i, 0)),
        out_shape=jax.ShapeDtypeStruct((rows, d), F32),
        scratch_shapes=[pltpu.VMEM((tm, d), BF16), pltpu.VMEM((tm, d), F32)],
        compiler_params=_cparams(("parallel", "arbitrary"), VMEM_LIMIT),
        name="mlp",
    )(x1, mods3, g.reshape(1, d), w1, w2, fg.reshape(1, d))


def _rope_perm_cols(w):
    n = QK_ROPE // 4
    return jnp.concatenate([-w[..., n:2 * n], w[..., 0:n], -w[..., 3 * n:4 * n], w[..., 2 * n:3 * n]], axis=-1)


def _layer_weights(l, d, w_in, w_uq, w_ukv, mlstm_gate_b, rg_wa, rg_wx, w_branch, w_out, w_mlp1, w_mlp2):
    wi = w_in[l]
    o = 0
    parts = {}
    for name, wdt in (("mla_q", Q_LORA), ("mla_kv", KV_LORA), ("mla_kr", QK_ROPE), ("ml_q", WIDTH),
                      ("ml_k", WIDTH), ("ml_v", WIDTH), ("ml_o", WIDTH), ("ml_g", 4 * HEADS),
                      ("rg_x", WIDTH), ("rg_gate", WIDTH), ("merge", 3 * d)):
        parts[name] = wi[:, o:o + wdt]
        o += wdt
    g_pad = jnp.pad(parts["ml_g"], ((0, 0), (0, LANE - 4 * HEADS)))
    w_in_p = jnp.concatenate(
        [parts["mla_q"], parts["mla_kv"], parts["mla_kr"], _rope_perm_cols(parts["mla_kr"]), g_pad,
         parts["ml_q"], parts["ml_k"], parts["ml_v"], parts["ml_o"], parts["rg_x"], parts["rg_gate"],
         parts["merge"]], axis=1).astype(BF16)
    wq = w_uq[l].reshape(Q_LORA, HEADS, QK_DIM)
    wqt = jnp.transpose(wq, (1, 2, 0)).astype(BF16)
    wqrt = jnp.transpose(_rope_perm_cols(wq[:, :, QK_NOPE:]), (1, 2, 0)).astype(BF16)
    wkv = w_ukv[l].reshape(KV_LORA, HEADS, QK_NOPE + V_HEAD)
    wk = wkv[:, :, :QK_NOPE].reshape(KV_LORA, HEADS * QK_NOPE).astype(BF16)
    wvt = jnp.transpose(wkv[:, :, QK_NOPE:], (1, 2, 0)).astype(BF16)
    gate_b = jnp.pad(mlstm_gate_b[l].reshape(1, 4 * HEADS), ((0, 0), (0, LANE - 4 * HEADS)))
    w_rg = jnp.concatenate([rg_wa[l, 0], rg_wx[l, 0], rg_wa[l, 1], rg_wx[l, 1]], axis=-1).astype(BF16)
    return dict(w_in=w_in_p, wqt=wqt, wqrt=wqrt, wk=wk, wvt=wvt, gate_b=gate_b, w_rg=w_rg,
                wb=w_branch[l].astype(BF16), w_out=w_out[l].astype(BF16),
                w1=w_mlp1[l].astype(BF16), w2=w_mlp2[l].astype(BF16))


def _rope_tables(b, s, lc):
    t = jnp.arange(s, dtype=jnp.int32)
    row = (t // GRID_W).astype(F32)
    col = (t % GRID_W).astype(F32)
    n_freq = QK_ROPE // 4
    inv = ROPE_BASE ** (-jnp.arange(n_freq, dtype=F32) / n_freq)
    ang_r = row[:, None] * inv[None, :]
    ang_c = col[:, None] * inv[None, :]
    cos = jnp.concatenate([jnp.cos(ang_r)] * 2 + [jnp.cos(ang_c)] * 2, axis=1)
    sin = jnp.concatenate([jnp.sin(ang_r)] * 2 + [jnp.sin(ang_c)] * 2, axis=1)
    cos_u = jnp.concatenate([jnp.tile(cos, (b, 1)), jnp.ones((b * lc, QK_ROPE), F32)], axis=0)
    sin_u = jnp.concatenate([jnp.tile(sin, (b, 1)), jnp.zeros((b * lc, QK_ROPE), F32)], axis=0)
    return cos_u, sin_u, cos_u.T, sin_u.T


def kernel(x, c, ctx, c_ctx, norm1_g, norm2_g, w_mod, b_mod, w_in, q_norm_g, w_uq, kv_norm_g, w_ukv,
           mlstm_gate_b, mlstm_norm_g, rg_conv_w, rg_conv_b, rg_wa, rg_ba, rg_wx, rg_bx, rg_lam,
           w_branch, w_out, w_mlp1, w_mlp2, final_g):
    b, s, d = x.shape
    lc = ctx.shape[1]
    depth = w_in.shape[0]
    rows_x, rows_c = b * s, b * lc
    r = rows_x + rows_c

    tm = _pick(math.gcd(rows_x, rows_c), (512, 256, 128))
    tm_seq = _pick(math.gcd(s, lc), (256, 128))
    tq = _pick(s, (512, 256, 128))
    n_x_tiles = rows_x // tm
    tiles_per_batch = s // tm

    def mod_row(i):
        return jnp.where(i < n_x_tiles, 1 + i // tiles_per_batch, 0)

    xu = jnp.concatenate([x.reshape(rows_x, d), ctx.reshape(rows_c, d)], axis=0)
    cc = jnp.concatenate([c_ctx[None, :], c, jnp.zeros((SUBLANE - 1 - b, d), F32)], axis=0)
    tabs = _rope_tables(b, s, lc)

    for l in range(depth):
        last = l == depth - 1
        w = _layer_weights(l, d, w_in, w_uq, w_ukv, mlstm_gate_b, rg_wa, rg_wx, w_branch, w_out,
                           w_mlp1, w_mlp2)
        mods3 = _modulation(cc, w_mod[l], b_mod[l]).reshape(SUBLANE, 1, N_MOD * d)
        z = _in_projection(xu, mods3, norm1_g[l], w["w_in"], mod_row, tm)

        qt, kk, vt = _mla_prep(z, tabs, q_norm_g[l], kv_norm_g[l], w["wqt"], w["wqrt"], w["wk"],
                               w["wvt"], b, s, lc, tm_seq)
        ya = _attn_call(qt, kk, vt, b=b, nq=s // tq, tq=tq, q_blk0=0, kv_len=s + lc, kv_blk0=0,
                        out_rows=r, row_blk=lambda bb, i: bb * (s // tq) + i, prev=None)
        if not last:
            ya = _attn_call(qt, kk, vt, b=b, nq=1, tq=lc, q_blk0=s // lc, kv_len=lc, kv_blk0=s // lc,
                            out_rows=r, row_blk=lambda bb, i: rows_x // lc + bb, prev=ya)

        gc, gr = _mlstm_gates(z, w["gate_b"])
        mhf, mhb = _mlstm(z, gc, gr, b, s, lc)

        af, bf, ab, bb_ = _rg_coeffs(z, rg_conv_w[l], rg_conv_b[l], w["w_rg"], rg_ba[l], rg_bx[l],
                                     rg_lam[l], b, s, lc, tm_seq)
        rhf, rhb = _rg_scan(af, bf, ab, bb_, b, s, lc, tm_seq)

        rows = rows_x if last else r
        m = _merge(rows, d, ya, mhf, mhb, z, mlstm_norm_g[l], rhf, rhb, w["wb"], tm_seq)
        x1 = _out_projection(rows, xu, m, w["w_out"], mods3, mod_row, tm)
        xu = _mlp(rows, x1, mods3, norm2_g[l], w["w1"], w["w2"], final_g, mod_row, tm, last)

    return xu.reshape(b, s, d)
```

```python
import functools
import math

import jax
import jax.numpy as jnp
from jax import lax
from jax.experimental import pallas as pl
from jax.experimental.pallas import tpu as pltpu

F32 = jnp.float32
BF16 = jnp.bfloat16

HEADS = 8
Q_LORA = 512
KV_LORA = 256
QK_NOPE = 128
QK_ROPE = 64
V_HEAD = 128
QK_DIM = QK_NOPE + QK_ROPE
ROPE_BASE = 10000.0
GRID_W = 64
ML_DQK = 128
ML_CHUNK = 128
WIDTH = 1024
RG_BLOCKS = 8
RG_BW = WIDTH // RG_BLOCKS
RG_CONV = 4
RG_C = 8.0
N_MOD = 6
EPS = 1e-6

LANE = 128
SUBLANE = 8
VT_ROWS = V_HEAD + 16

Z_Q = 0
Z_KV = Z_Q + Q_LORA
Z_KR = Z_KV + KV_LORA
Z_KRP = Z_KR + QK_ROPE
Z_MLA_END = Z_KRP + QK_ROPE
Z_G = Z_MLA_END
Z_MQ = Z_G + LANE
Z_MK = Z_MQ + WIDTH
Z_MV = Z_MK + WIDTH
Z_MO = Z_MV + WIDTH
Z_RX = Z_MO + WIDTH
Z_RG = Z_RX + WIDTH
Z_MERGE = Z_RG + WIDTH

Q_SCALE = (QK_DIM ** -0.5) * math.log2(math.e)
VMEM_LIMIT = 56 * 1024 * 1024


def _cparams(sem, vmem=None):
    return pltpu.CompilerParams(dimension_semantics=sem, vmem_limit_bytes=vmem)


def _pick(n, cands):
    for c in cands:
        if n % c == 0:
            return c
    raise ValueError(f"no tile in {cands} divides {n}")


def _pcall(kernel, *, prev, in_specs, args, **kw):
    if prev is None:
        return pl.pallas_call(kernel, in_specs=in_specs, **kw)(*args)
    n_in = len(args)

    def aliased(*refs):
        kernel(*refs[:n_in], *refs[n_in + 1:])

    return pl.pallas_call(aliased, in_specs=in_specs + [pl.BlockSpec(memory_space=pl.ANY)],
                          input_output_aliases={n_in: 0}, **kw)(*args, prev)


def _rms(x, g):
    return x * lax.rsqrt(jnp.mean(x * x, axis=-1, keepdims=True) + EPS) * g


def _softplus(x):
    return jnp.maximum(x, 0.0) + jnp.log1p(jnp.exp(-jnp.abs(x)))


def _log_sigmoid(x):
    return jnp.minimum(x, 0.0) - jnp.log1p(jnp.exp(-jnp.abs(x)))


def _gelu_tanh(x):
    return 0.5 * x * (1.0 + jnp.tanh(math.sqrt(2.0 / math.pi) * (x + 0.044715 * (x * x * x))))


def _mod_kernel(c_ref, w_ref, b_ref, o_ref):
    c = c_ref[...]
    sc = c * jax.nn.sigmoid(c)
    o_ref[...] = jnp.dot(sc, w_ref[...], preferred_element_type=F32,
                         precision=lax.Precision.HIGHEST) + b_ref[...]


def _modulation(cc, w_mod, b_mod):
    rows, d = cc.shape
    n = w_mod.shape[1]
    tn = _pick(n, (1024, 512, 256, 128))
    return pl.pallas_call(
        _mod_kernel,
        grid=(n // tn,),
        in_specs=[pl.BlockSpec((rows, d), lambda j: (0, 0)),
                  pl.BlockSpec((d, tn), lambda j: (0, j)),
                  pl.BlockSpec((1, tn), lambda j: (0, j))],
        out_specs=pl.BlockSpec((rows, tn), lambda j: (0, j)),
        out_shape=jax.ShapeDtypeStruct((rows, n), F32),
        compiler_params=_cparams(("parallel",), VMEM_LIMIT),
        name="adaln_mod",
    )(cc, w_mod, b_mod.reshape(1, n))


def _inproj_kernel(x_ref, mod_ref, g_ref, w_ref, z_ref, xn_ref, *, d):
    @pl.when(pl.program_id(1) == 0)
    def _():
        y = _rms(x_ref[...], g_ref[...])
        shift = mod_ref[0, :, 0:d]
        scale = mod_ref[0, :, d:2 * d]
        xn_ref[...] = (y * (1.0 + scale) + shift).astype(BF16)

    z_ref[...] = jnp.dot(xn_ref[...], w_ref[...], preferred_element_type=F32)


def _in_projection(xu, mods3, g, w_in_p, mod_row, span, prev):
    tm, blk0, nblk = span
    r, d = xu.shape
    n = w_in_p.shape[1]
    tn = _pick(n, (1024, 512, 256, 128))
    return _pcall(
        functools.partial(_inproj_kernel, d=d),
        prev=prev,
        grid=(nblk, n // tn),
        in_specs=[pl.BlockSpec((tm, d), lambda i, j: (blk0 + i, 0)),
                  pl.BlockSpec((1, 1, N_MOD * d), lambda i, j: (mod_row((blk0 + i) * tm), 0, 0)),
                  pl.BlockSpec((1, d), lambda i, j: (0, 0)),
                  pl.BlockSpec((d, tn), lambda i, j: (0, j))],
        args=[xu, mods3, g.reshape(1, d), w_in_p],
        out_specs=pl.BlockSpec((tm, tn), lambda i, j: (blk0 + i, j)),
        out_shape=jax.ShapeDtypeStruct((r, n), F32),
        scratch_shapes=[pltpu.VMEM((tm, d), BF16)],
        compiler_params=_cparams(("parallel", "arbitrary"), VMEM_LIMIT),
        name="in_proj",
    )


def _mla_prep_kernel(z_ref, cos_ref, sin_ref, cost_ref, sint_ref, gq_ref, gkv_ref,
                     wqt_ref, wqrt_ref, wk_ref, wvt_ref, qt_ref, k_ref, vt_ref):
    tm = z_ref.shape[0]
    qn = _rms(z_ref[:, Z_Q:Z_Q + Q_LORA], gq_ref[...]).astype(BF16)
    kvn = _rms(z_ref[:, Z_KV:Z_KV + KV_LORA], gkv_ref[...]).astype(BF16)
    kr = (z_ref[:, Z_KR:Z_KR + QK_ROPE] * cos_ref[...]
          + z_ref[:, Z_KRP:Z_KRP + QK_ROPE] * sin_ref[...]).astype(BF16)
    kn = jnp.dot(kvn, wk_ref[...], preferred_element_type=F32)
    nt = (((1,), (1,)), ((), ()))
    ones = jnp.ones((VT_ROWS - V_HEAD, tm), BF16)
    for h in range(HEADS):
        q_t = lax.dot_general(wqt_ref[h], qn, nt, preferred_element_type=F32)
        qr_t = lax.dot_general(wqrt_ref[h], qn, nt, preferred_element_type=F32)
        q_rot = q_t[QK_NOPE:QK_DIM] * cost_ref[...] + qr_t * sint_ref[...]
        qt_ref[0, h, 0:QK_NOPE, :] = (q_t[0:QK_NOPE] * Q_SCALE).astype(BF16)
        qt_ref[0, h, QK_NOPE:QK_DIM, :] = (q_rot * Q_SCALE).astype(BF16)
        k_ref[0, h, :, 0:QK_NOPE] = kn[:, h * QK_NOPE:(h + 1) * QK_NOPE].astype(BF16)
        k_ref[0, h, :, QK_NOPE:QK_DIM] = kr
        v_t = lax.dot_general(wvt_ref[h], kvn, nt, preferred_element_type=F32)
        vt_ref[0, h, 0:V_HEAD, :] = v_t.astype(BF16)
        vt_ref[0, h, V_HEAD:VT_ROWS, :] = ones


def _mla_prep(z, tabs, gq, gkv, wqt, wqrt, wk, wvt, b, s, lc, tm):
    r = z.shape[0]
    cos_u, sin_u, cost_u, sint_u = tabs
    nxt, nct = s // tm, lc // tm
    ltot = s + lc

    def bidx(i):
        j = i - b * nxt
        return jnp.where(i < b * nxt, i // nxt, j // nct)

    def sblk(i):
        j = i - b * nxt
        return jnp.where(i < b * nxt, i % nxt, nxt + j % nct)

    full = lambda shape: pl.BlockSpec(shape, lambda i: (0,) * len(shape))
    return pl.pallas_call(
        _mla_prep_kernel,
        grid=(r // tm,),
        in_specs=[pl.BlockSpec((tm, Z_MLA_END), lambda i: (i, 0)),
                  pl.BlockSpec((tm, QK_ROPE), lambda i: (i, 0)),
                  pl.BlockSpec((tm, QK_ROPE), lambda i: (i, 0)),
                  pl.BlockSpec((QK_ROPE, tm), lambda i: (0, i)),
                  pl.BlockSpec((QK_ROPE, tm), lambda i: (0, i)),
                  full((1, Q_LORA)), full((1, KV_LORA)),
                  full(wqt.shape), full(wqrt.shape), full(wk.shape), full(wvt.shape)],
        out_specs=[pl.BlockSpec((1, HEADS, QK_DIM, tm), lambda i: (bidx(i), 0, 0, sblk(i))),
                   pl.BlockSpec((1, HEADS, tm, QK_DIM), lambda i: (bidx(i), 0, sblk(i), 0)),
                   pl.BlockSpec((1, HEADS, VT_ROWS, tm), lambda i: (bidx(i), 0, 0, sblk(i)))],
        out_shape=[jax.ShapeDtypeStruct((b, HEADS, QK_DIM, ltot), BF16),
                   jax.ShapeDtypeStruct((b, HEADS, ltot, QK_DIM), BF16),
                   jax.ShapeDtypeStruct((b, HEADS, VT_ROWS, ltot), BF16)],
        compiler_params=_cparams(("parallel",), VMEM_LIMIT),
        name="mla_prep",
    )(z, cos_u, sin_u, cost_u, sint_u, gq.reshape(1, -1), gkv.reshape(1, -1), wqt, wqrt, wk, wvt)


def _attn_kernel(qt_ref, k_ref, vt_ref, o_ref, s_ref, *, tk, nk):
    qt = qt_ref[0, 0]
    tq = qt.shape[1]

    def scores(c, slot):
        off = pl.multiple_of(c * tk, tk)
        s_ref[slot] = jnp.dot(k_ref[0, 0, pl.ds(off, tk), :], qt, preferred_element_type=F32)

    def update(c, slot, m, acc):
        off = pl.multiple_of(c * tk, tk)
        m_new = jnp.maximum(m, jnp.max(s_ref[slot], axis=0, keepdims=True))
        p = jnp.exp2(s_ref[slot] - m_new).astype(BF16)
        alpha = jnp.exp2(m - m_new)
        pv = jnp.dot(vt_ref[0, 0, :, pl.ds(off, tk)], p, preferred_element_type=F32)
        return m_new, alpha * acc + pv

    def body(i, carry):
        m, acc = carry
        c = 2 * i
        scores(c + 1, 1)
        m, acc = update(c, 0, m, acc)
        scores(c + 2, 0)
        return update(c + 1, 1, m, acc)

    m0 = jnp.full((1, tq), -jnp.inf, F32)
    acc0 = jnp.zeros((VT_ROWS, tq), F32)
    scores(0, 0)
    m, acc = lax.fori_loop(0, (nk - 1) // 2, body, (m0, acc0))
    if nk % 2 == 1:
        _, acc = update(nk - 1, 0, m, acc)
    else:
        scores(nk - 1, 1)
        m, acc = update(nk - 2, 0, m, acc)
        _, acc = update(nk - 1, 1, m, acc)
    o = acc[0:V_HEAD] / acc[V_HEAD:V_HEAD + 1]
    o_ref[...] = o.T.astype(o_ref.dtype)


def _attn_call(qt, k, vt, *, b, nq, tq, q_blk0, kv_len, kv_blk0, out_rows, row_blk, prev):
    tk = _pick(kv_len, (1280, 1024, 512, 256, 128))
    nk = kv_len // tk
    return _pcall(
        functools.partial(_attn_kernel, tk=tk, nk=nk),
        prev=prev,
        grid=(b, HEADS, nq),
        in_specs=[pl.BlockSpec((1, 1, QK_DIM, tq), lambda bb, h, i: (bb, h, 0, q_blk0 + i)),
                  pl.BlockSpec((1, 1, kv_len, QK_DIM), lambda bb, h, i: (bb, h, kv_blk0, 0)),
                  pl.BlockSpec((1, 1, VT_ROWS, kv_len), lambda bb, h, i: (bb, h, 0, kv_blk0))],
        args=[qt, k, vt],
        out_specs=pl.BlockSpec((tq, V_HEAD), lambda bb, h, i: (row_blk(bb, i), h)),
        out_shape=jax.ShapeDtypeStruct((out_rows, WIDTH), BF16),
        scratch_shapes=[pltpu.VMEM((2, tk, tq), F32)],
        compiler_params=_cparams(("parallel", "parallel", "arbitrary"), VMEM_LIMIT),
        name="mla_attn",
    )


def _mlstm_gates_kernel(z_ref, b_ref, gc_ref, gr_ref):
    g = z_ref[...] + b_ref[...]
    n = g.shape[0]
    lane = lax.broadcasted_iota(jnp.int32, g.shape, 1)
    row = lax.broadcasted_iota(jnp.int32, g.shape, 0)
    lf = _log_sigmoid(g)
    pre = lf
    suf = lf
    k = 1
    while k < n:
        pre = pre + jnp.where(row >= k, pltpu.roll(pre, k, 0), 0.0)
        suf = suf + jnp.where(row < n - k, pltpu.roll(suf, n - k, 0), 0.0)
        k *= 2
    bwd = lane >= 2 * HEADS
    cum = jnp.where(bwd, suf, pre)
    r = g - pltpu.roll(cum, LANE - HEADS, 1)
    pmax = r
    smax = r
    k = 1
    while k < n:
        pmax = jnp.maximum(pmax, jnp.where(row >= k, pltpu.roll(pmax, k, 0), -jnp.inf))
        smax = jnp.maximum(smax, jnp.where(row < n - k, pltpu.roll(smax, n - k, 0), -jnp.inf))
        k *= 2
    cmax = jnp.where(bwd, smax, pmax)
    is_f = ((lane >= HEADS) & (lane < 2 * HEADS)) | ((lane >= 3 * HEADS) & (lane < 4 * HEADS))
    gc = jnp.where(is_f, cum, g)
    gc = jnp.where((lane >= 4 * HEADS) & (lane < 8 * HEADS), pltpu.roll(cmax, 4 * HEADS, 1), gc)
    gc = jnp.where((lane >= 8 * HEADS) & (lane < 12 * HEADS), pltpu.roll(r, 8 * HEADS, 1), gc)
    gc_ref[...] = gc
    gr_ref[0] = gc.T


def _mlstm_gates(z, bias_pad):
    r = z.shape[0]
    nchunk = r // ML_CHUNK
    return pl.pallas_call(
        _mlstm_gates_kernel,
        grid=(nchunk,),
        in_specs=[pl.BlockSpec((ML_CHUNK, LANE), lambda i: (i, Z_G // LANE)),
                  pl.BlockSpec((1, LANE), lambda i: (0, 0))],
        out_specs=[pl.BlockSpec((ML_CHUNK, LANE), lambda i: (i, 0)),
                   pl.BlockSpec((1, LANE, ML_CHUNK), lambda i: (i, 0, 0))],
        out_shape=[jax.ShapeDtypeStruct((r, LANE), F32),
                   jax.ShapeDtypeStruct((nchunk, LANE, ML_CHUNK), F32)],
        compiler_params=_cparams(("parallel",)),
        name="mlstm_gates",
    )(z, bias_pad)


def _mlstm_kernel(qf_ref, kf_ref, vf_ref, gcf_ref, grf_ref,
                  qb_ref, kb_ref, vb_ref, gcb_ref, grb_ref,
                  hf_ref, hb_ref, c_ref, m_ref):
    L = ML_CHUNK

    @pl.when(pl.program_id(1) == 0)
    def _():
        c_ref[...] = jnp.zeros(c_ref.shape, F32)
        m_ref[...] = jnp.zeros(m_ref.shape, F32)

    row = lax.broadcasted_iota(jnp.int32, (L, L), 0)
    col = lax.broadcasted_iota(jnp.int32, (L, L), 1)
    lane = lax.broadcasted_iota(jnp.int32, (L, LANE), 1)
    ones_col = jnp.where(lane == 0, 1.0, 0.0).astype(BF16)
    nt = (((1,), (1,)), ((), ()))
    tn = (((0,), (0,)), ((), ()))
    dirs = ((qf_ref, kf_ref, vf_ref, gcf_ref, grf_ref, hf_ref, col <= row),
            (qb_ref, kb_ref, vb_ref, gcb_ref, grb_ref, hb_ref, col >= row))
    ln_ks = math.log(ML_DQK ** -0.5)
    for d, (q_ref, k_ref, v_ref, gc_ref, gr_ref, o_ref, mask) in enumerate(dirs):
        gc = gc_ref[...]
        gr = gr_ref[0]
        st = []
        for h in range(HEADS):
            li = 2 * HEADS * d + h
            lb = li + HEADS
            lm = li + 4 * HEADS
            lr = li + 8 * HEADS
            bcol = gc[:, lb:lb + 1]
            g_tot = bcol[L - 1:L] if d == 0 else bcol[0:1]
            wend = g_tot - bcol + gc[:, li:li + 1]
            a = jnp.max(wend, axis=0, keepdims=True)
            m_old = m_ref[d * HEADS + h][0:1, 0:1]
            ucol = jnp.maximum(m_old, gc[:, lm:lm + 1])
            m_new = jnp.maximum(g_tot + m_old, a)
            st.append(dict(
                e=jnp.exp(wend - a + ln_ks), ucol=ucol, rrow=gr[lr:lr + 1, :],
                di=jnp.exp(m_old - ucol), em=jnp.exp(-(bcol + ucol)), m_new=m_new,
                dec=jnp.exp(g_tot + m_old - m_new), inp=jnp.exp(a - m_new)))
        for h, t in enumerate(st):
            sl = slice(h * ML_DQK, (h + 1) * ML_DQK)
            q = q_ref[:, sl].astype(BF16)
            k = k_ref[:, sl]
            t["vext"] = jnp.concatenate([v_ref[:, sl].astype(BF16), ones_col], axis=1)
            t["qk"] = lax.dot_general(q, k.astype(BF16), nt, preferred_element_type=F32)
            t["d_c"] = lax.dot_general((k * t["e"]).astype(BF16), t["vext"], tn,
                                       preferred_element_type=F32)
            t["qc"] = jnp.dot(q, c_ref[d * HEADS + h].astype(BF16), preferred_element_type=F32)
        for h, t in enumerate(st):
            dexp = jnp.exp(jnp.where(mask, t["rrow"] - t["ucol"] + ln_ks, -jnp.inf))
            t["s"] = (t["qk"] * dexp).astype(BF16)
        for h, t in enumerate(st):
            sl = slice(h * ML_DQK, (h + 1) * ML_DQK)
            res = t["di"] * t["qc"] + jnp.dot(t["s"], t["vext"], preferred_element_type=F32)
            den = res[:, ML_DQK:ML_DQK + 1]
            o_ref[:, sl] = res[:, 0:ML_DQK] / jnp.maximum(jnp.abs(den), t["em"])
            c_ref[d * HEADS + h] = t["dec"] * c_ref[d * HEADS + h] + t["inp"] * t["d_c"]
            m_ref[d * HEADS + h] = jnp.broadcast_to(t["m_new"], (SUBLANE, LANE))


def _mlstm(z, gc, gr, b, s, lc):
    r = z.shape[0]
    L = ML_CHUNK
    nxc, ncc = s // L, lc // L
    x0 = lambda bb: bb * nxc
    c0 = lambda bb: b * nxc + bb * ncc

    def blk_f(bb, st):
        return jnp.where(st < ncc, c0(bb) + st, x0(bb) + st - ncc)

    def blk_b(bb, st):
        return jnp.where(st < ncc, c0(bb) + ncc - 1 - st, x0(bb) + nxc - 1 - (st - ncc))

    def zspec(blk, cb):
        return pl.BlockSpec((L, WIDTH), lambda bb, st: (blk(bb, st), cb))

    def dir_specs(blk):
        return [zspec(blk, Z_MQ // WIDTH), zspec(blk, Z_MK // WIDTH), zspec(blk, Z_MV // WIDTH),
                pl.BlockSpec((L, LANE), lambda bb, st: (blk(bb, st), 0)),
                pl.BlockSpec((1, LANE, L), lambda bb, st: (blk(bb, st), 0, 0))]

    return pl.pallas_call(
        _mlstm_kernel,
        grid=(b, nxc + ncc),
        in_specs=dir_specs(blk_f) + dir_specs(blk_b),
        out_specs=[pl.BlockSpec((L, WIDTH), lambda bb, st: (blk_f(bb, st), 0)),
                   pl.BlockSpec((L, WIDTH), lambda bb, st: (blk_b(bb, st), 0))],
        out_shape=[jax.ShapeDtypeStruct((r, WIDTH), F32), jax.ShapeDtypeStruct((r, WIDTH), F32)],
        scratch_shapes=[pltpu.VMEM((2 * HEADS, ML_DQK, 2 * LANE), F32),
                        pltpu.VMEM((2 * HEADS, SUBLANE, LANE), F32)],
        compiler_params=_cparams(("arbitrary", "arbitrary"), VMEM_LIMIT),
        name="mlstm",
    )(z, z, z, gc, gr, z, z, z, gc, gr)


def _rg_ab_kernel(cur_ref, prev_ref, next_ref, cw_ref, cb_ref, w_ref, ba_ref, bx_ref, lam_ref,
                  af_ref, bf_ref, ab_ref, bb_ref, xe_ref, *, tm, s_len, c_len, rows_x):
    row0 = pl.program_id(0) * tm
    in_x = row0 < rows_x
    seq = jnp.where(in_x, s_len, c_len)
    off = jnp.where(in_x, row0, row0 - rows_x)
    first = lax.rem(off, seq) == 0
    last = lax.rem(off + tm, seq) == 0
    xe_ref[0:SUBLANE, :] = jnp.where(first, 0.0, prev_ref[...])
    xe_ref[SUBLANE:SUBLANE + tm, :] = cur_ref[...]
    xe_ref[SUBLANE + tm:2 * SUBLANE + tm, :] = jnp.where(last, 0.0, next_ref[...])
    lp = RG_CONV // 2
    xc = cb_ref[...] + cw_ref[0:1, :] * xe_ref[pl.ds(SUBLANE - lp, tm), :]
    for j in range(1, RG_CONV):
        xc = xc + cw_ref[j:j + 1, :] * xe_ref[pl.ds(SUBLANE - lp + j, tm), :]
    outs = ((af_ref, bf_ref), (ab_ref, bb_ref))
    for g in range(RG_BLOCKS):
        sl = slice(g * RG_BW, (g + 1) * RG_BW)
        xg = xc[:, sl]
        o = jnp.dot(xg.astype(BF16), w_ref[g], preferred_element_type=F32)
        for d in range(2):
            r = jax.nn.sigmoid(o[:, (2 * d) * RG_BW:(2 * d + 1) * RG_BW] + ba_ref[d:d + 1, sl])
            i = jax.nn.sigmoid(o[:, (2 * d + 1) * RG_BW:(2 * d + 2) * RG_BW] + bx_ref[d:d + 1, sl])
            log_a = (-RG_C) * r * _softplus(-lam_ref[d:d + 1, sl])
            a = jnp.exp(log_a)
            one_m_a2 = -jnp.tanh(log_a) * (a * a + 1.0)
            outs[d][0][:, sl] = a
            outs[d][1][:, sl] = jnp.sqrt(one_m_a2) * (i * xg)


def _rg_coeffs(z, cw, cb, w_rg, ba, bx, lam, b, s, lc, tm):
    r = z.shape[0]
    per = tm // SUBLANE
    nblk8 = r // SUBLANE
    cbk = Z_RX // WIDTH
    full = lambda shape: pl.BlockSpec(shape, lambda i: (0,) * len(shape))
    kernel = functools.partial(_rg_ab_kernel, tm=tm, s_len=s, c_len=lc, rows_x=b * s)
    o_spec = pl.BlockSpec((tm, WIDTH), lambda i: (i, 0))
    o_shape = jax.ShapeDtypeStruct((r, WIDTH), F32)
    return pl.pallas_call(
        kernel,
        grid=(r // tm,),
        in_specs=[pl.BlockSpec((tm, WIDTH), lambda i: (i, cbk)),
                  pl.BlockSpec((SUBLANE, WIDTH), lambda i: (jnp.maximum(i * per - 1, 0), cbk)),
                  pl.BlockSpec((SUBLANE, WIDTH), lambda i: (jnp.minimum((i + 1) * per, nblk8 - 1), cbk)),
                  full((RG_CONV, WIDTH)), full((1, WIDTH)), full(w_rg.shape),
                  full((2, WIDTH)), full((2, WIDTH)), full((2, WIDTH))],
        out_specs=[o_spec] * 4,
        out_shape=[o_shape] * 4,
        scratch_shapes=[pltpu.VMEM((tm + 2 * SUBLANE, WIDTH), F32)],
        compiler_params=_cparams(("parallel",), VMEM_LIMIT),
        name="rglru_coeffs",
    )(z, z, z, cw, cb.reshape(1, WIDTH), w_rg, ba, bx, lam)


def _rg_scan_kernel(af_ref, bf_ref, ab_ref, bb_ref, hf_ref, hb_ref, sf_ref, sb_ref, *, tt):
    @pl.when(pl.program_id(1) == 0)
    def _():
        sf_ref[...] = jnp.zeros(sf_ref.shape, F32)
        sb_ref[...] = jnp.zeros(sb_ref.shape, F32)

    def body(t, carry):
        hf, hb = carry
        hf = af_ref[pl.ds(t, 1), :] * hf + bf_ref[pl.ds(t, 1), :]
        hf_ref[pl.ds(t, 1), :] = hf
        tb = tt - 1 - t
        hb = ab_ref[pl.ds(tb, 1), :] * hb + bb_ref[pl.ds(tb, 1), :]
        hb_ref[pl.ds(tb, 1), :] = hb
        return hf, hb

    hf, hb = lax.fori_loop(0, tt, body, (sf_ref[...], sb_ref[...]), unroll=8)
    sf_ref[...] = hf
    sb_ref[...] = hb


def _rg_scan(af, bf, ab, bb, b, s, lc, tt):
    r = af.shape[0]
    nxt, nct = s // tt, lc // tt
    x0 = lambda bi: bi * nxt
    c0 = lambda bi: b * nxt + bi * nct

    def blk_f(bi, st):
        return jnp.where(st < nct, c0(bi) + st, x0(bi) + st - nct)

    def blk_b(bi, st):
        return jnp.where(st < nct, c0(bi) + nct - 1 - st, x0(bi) + nxt - 1 - (st - nct))

    sf = pl.BlockSpec((tt, WIDTH), lambda bi, st: (blk_f(bi, st), 0))
    sb = pl.BlockSpec((tt, WIDTH), lambda bi, st: (blk_b(bi, st), 0))
    o_shape = jax.ShapeDtypeStruct((r, WIDTH), F32)
    return pl.pallas_call(
        functools.partial(_rg_scan_kernel, tt=tt),
        grid=(b, nxt + nct),
        in_specs=[sf, sf, sb, sb],
        out_specs=[sf, sb],
        out_shape=[o_shape, o_shape],
        scratch_shapes=[pltpu.VMEM((1, WIDTH), F32), pltpu.VMEM((1, WIDTH), F32)],
        compiler_params=_cparams(("arbitrary", "arbitrary")),
        name="rglru_scan",
    )(af, bf, ab, bb)


def _merge_kernel(ya_ref, mhf_ref, mhb_ref, zo_ref, ng_ref, rhf_ref, rhb_ref, zg_ref,
                  zm0_ref, zm1_ref, zm2_ref, w0_ref, w1_ref, w2_ref, m_ref, yb_ref, yr_ref):
    @pl.when(pl.program_id(1) == 0)
    def _():
        hsum = mhf_ref[...] + mhb_ref[...]
        for h in range(HEADS):
            sl = slice(h * V_HEAD, (h + 1) * V_HEAD)
            hn = _rms(hsum[:, sl], ng_ref[:, sl])
            yb_ref[:, sl] = (hn * jax.nn.sigmoid(zo_ref[:, sl])).astype(BF16)
        yr_ref[...] = ((rhf_ref[...] + rhb_ref[...]) * _gelu_tanh(zg_ref[...])).astype(BF16)

    m = (jax.nn.sigmoid(zm0_ref[...]) * jnp.dot(ya_ref[...], w0_ref[0], preferred_element_type=F32)
         + jax.nn.sigmoid(zm1_ref[...]) * jnp.dot(yb_ref[...], w1_ref[0], preferred_element_type=F32)
         + jax.nn.sigmoid(zm2_ref[...]) * jnp.dot(yr_ref[...], w2_ref[0], preferred_element_type=F32))
    m_ref[...] = m.astype(BF16)


def _merge(rows, d, ya, mhf, mhb, z, ng, rhf, rhb, wb, tm):
    tn = _pick(d, (512, 256, 128))
    wide = lambda: pl.BlockSpec((tm, WIDTH), lambda i, j: (i, 0))
    zblk = lambda cb: pl.BlockSpec((tm, WIDTH), lambda i, j: (i, cb))
    zm = lambda br: pl.BlockSpec((tm, tn), lambda i, j: (i, (Z_MERGE + br * d) // tn + j))
    wspec = lambda br: pl.BlockSpec((1, WIDTH, tn), lambda i, j: (br, 0, j))
    return pl.pallas_call(
        _merge_kernel,
        grid=(rows // tm, d // tn),
        in_specs=[wide(), wide(), wide(), zblk(Z_MO // WIDTH),
                  pl.BlockSpec((1, WIDTH), lambda i, j: (0, 0)),
                  wide(), wide(), zblk(Z_RG // WIDTH),
                  zm(0), zm(1), zm(2), wspec(0), wspec(1), wspec(2)],
        out_specs=pl.BlockSpec((tm, tn), lambda i, j: (i, j)),
        out_shape=jax.ShapeDtypeStruct((rows, d), BF16),
        scratch_shapes=[pltpu.VMEM((tm, WIDTH), BF16), pltpu.VMEM((tm, WIDTH), BF16)],
        compiler_params=_cparams(("parallel", "arbitrary"), VMEM_LIMIT),
        name="branch_merge",
    )(ya, mhf, mhb, z, ng.reshape(1, WIDTH), rhf, rhb, z, z, z, z, wb, wb, wb)


def _outproj_kernel(m_ref, w_ref, x_ref, gate_ref, o_ref):
    o_ref[...] = x_ref[...] + gate_ref[0] * jnp.dot(m_ref[...], w_ref[...], preferred_element_type=F32)


def _out_projection(rows, xu, m, w_out, mods3, mod_row, span, prev):
    tm, blk0, nblk = span
    d = xu.shape[1]
    tn = _pick(d, (1024, 512, 256, 128))
    return _pcall(
        _outproj_kernel,
        prev=prev,
        grid=(nblk, d // tn),
        in_specs=[pl.BlockSpec((tm, d), lambda i, j: (blk0 + i, 0)),
                  pl.BlockSpec((d, tn), lambda i, j: (0, j)),
                  pl.BlockSpec((tm, tn), lambda i, j: (blk0 + i, j)),
                  pl.BlockSpec((1, 1, tn), lambda i, j: (mod_row((blk0 + i) * tm), 0, (2 * d) // tn + j))],
        args=[m, w_out, xu, mods3],
        out_specs=pl.BlockSpec((tm, tn), lambda i, j: (blk0 + i, j)),
        out_shape=jax.ShapeDtypeStruct((rows, d), F32),
        compiler_params=_cparams(("parallel", "parallel"), VMEM_LIMIT),
        name="out_proj",
    )


def _mlp_kernel(x_ref, mod_ref, g_ref, w1_ref, w2_ref, fg_ref, o_ref, xn_ref, *, d, final):
    j = pl.program_id(1)

    @pl.when(j == 0)
    def _():
        y = _rms(x_ref[...], g_ref[...])
        shift = mod_ref[0, :, 3 * d:4 * d]
        scale = mod_ref[0, :, 4 * d:5 * d]
        xn_ref[...] = (y * (1.0 + scale) + shift).astype(BF16)
        o_ref[...] = jnp.zeros(o_ref.shape, F32)

    h = jnp.dot(xn_ref[...], w1_ref[...], preferred_element_type=F32)
    h = jnp.square(jnp.maximum(h, 0.0)).astype(BF16)
    o_ref[...] += jnp.dot(h, w2_ref[...], preferred_element_type=F32)

    @pl.when(j == pl.num_programs(1) - 1)
    def _():
        out = x_ref[...] + mod_ref[0, :, 5 * d:6 * d] * o_ref[...]
        if final:
            out = _rms(out, fg_ref[...])
        o_ref[...] = out


def _mlp(rows, x1, mods3, g, w1, w2, fg, mod_row, span, prev, final):
    tm, blk0, nblk = span
    d = x1.shape[1]
    dff = w1.shape[1]
    tf = _pick(dff, (512, 256, 128))
    return _pcall(
        functools.partial(_mlp_kernel, d=d, final=final),
        prev=prev,
        grid=(nblk, dff // tf),
        in_specs=[pl.BlockSpec((tm, d), lambda i, j: (blk0 + i, 0)),
                  pl.BlockSpec((1, 1, N_MOD * d), lambda i, j: (mod_row((blk0 + i) * tm), 0, 0)),
                  pl.BlockSpec((1, d), lambda i, j: (0, 0)),
                  pl.BlockSpec((d, tf), lambda i, j: (0, j)),
                  pl.BlockSpec((tf, d), lambda i, j: (j, 0)),
                  pl.BlockSpec((1, d), lambda i, j: (0, 0))],
        args=[x1, mods3, g.reshape(1, d), w1, w2, fg.reshape(1, d)],
        out_specs=pl.BlockSpec((tm, d), lambda i, j: (blk0 + i, 0)),
        out_shape=jax.ShapeDtypeStruct((rows, d), F32),
        scratch_shapes=[pltpu.VMEM((tm, d), BF16)],
        compiler_params=_cparams(("parallel", "arbitrary"), VMEM_LIMIT),
        name="mlp",
    )


def _rope_perm_cols(w):
    n = QK_ROPE // 4
    return jnp.concatenate([-w[..., n:2 * n], w[..., 0:n], -w[..., 3 * n:4 * n], w[..., 2 * n:3 * n]], axis=-1)


def _layer_weights(l, d, w_in, w_uq, w_ukv, mlstm_gate_b, rg_wa, rg_wx, w_branch, w_out, w_mlp1, w_mlp2):
    wi = w_in[l]
    o = 0
    parts = {}
    for name, wdt in (("mla_q", Q_LORA), ("mla_kv", KV_LORA), ("mla_kr", QK_ROPE), ("ml_q", WIDTH),
                      ("ml_k", WIDTH), ("ml_v", WIDTH), ("ml_o", WIDTH), ("ml_g", 4 * HEADS),
                      ("rg_x", WIDTH), ("rg_gate", WIDTH), ("merge", 3 * d)):
        parts[name] = wi[:, o:o + wdt]
        o += wdt
    g_pad = jnp.pad(parts["ml_g"], ((0, 0), (0, LANE - 4 * HEADS)))
    w_in_p = jnp.concatenate(
        [parts["mla_q"], parts["mla_kv"], parts["mla_kr"], _rope_perm_cols(parts["mla_kr"]), g_pad,
         parts["ml_q"], parts["ml_k"], parts["ml_v"], parts["ml_o"], parts["rg_x"], parts["rg_gate"],
         parts["merge"]], axis=1).astype(BF16)
    wq = w_uq[l].reshape(Q_LORA, HEADS, QK_DIM)
    wqt = jnp.transpose(wq, (1, 2, 0)).astype(BF16)
    wqrt = jnp.transpose(_rope_perm_cols(wq[:, :, QK_NOPE:]), (1, 2, 0)).astype(BF16)
    wkv = w_ukv[l].reshape(KV_LORA, HEADS, QK_NOPE + V_HEAD)
    wk = wkv[:, :, :QK_NOPE].reshape(KV_LORA, HEADS * QK_NOPE).astype(BF16)
    wvt = jnp.transpose(wkv[:, :, QK_NOPE:], (1, 2, 0)).astype(BF16)
    gate_b = jnp.pad(mlstm_gate_b[l].reshape(1, 4 * HEADS), ((0, 0), (0, LANE - 4 * HEADS)))
    w_rg = jnp.concatenate([rg_wa[l, 0], rg_wx[l, 0], rg_wa[l, 1], rg_wx[l, 1]], axis=-1).astype(BF16)
    return dict(w_in=w_in_p, wqt=wqt, wqrt=wqrt, wk=wk, wvt=wvt, gate_b=gate_b, w_rg=w_rg,
                wb=w_branch[l].astype(BF16), w_out=w_out[l].astype(BF16),
                w1=w_mlp1[l].astype(BF16), w2=w_mlp2[l].astype(BF16))


def _rope_tables(b, s, lc):
    t = jnp.arange(s, dtype=jnp.int32)
    row = (t // GRID_W).astype(F32)
    col = (t % GRID_W).astype(F32)
    n_freq = QK_ROPE // 4
    inv = ROPE_BASE ** (-jnp.arange(n_freq, dtype=F32) / n_freq)
    ang_r = row[:, None] * inv[None, :]
    ang_c = col[:, None] * inv[None, :]
    cos = jnp.concatenate([jnp.cos(ang_r)] * 2 + [jnp.cos(ang_c)] * 2, axis=1)
    sin = jnp.concatenate([jnp.sin(ang_r)] * 2 + [jnp.sin(ang_c)] * 2, axis=1)
    cos_u = jnp.concatenate([jnp.tile(cos, (b, 1)), jnp.ones((b * lc, QK_ROPE), F32)], axis=0)
    sin_u = jnp.concatenate([jnp.tile(sin, (b, 1)), jnp.zeros((b * lc, QK_ROPE), F32)], axis=0)
    return cos_u, sin_u, cos_u.T, sin_u.T


def kernel(x, c, ctx, c_ctx, norm1_g, norm2_g, w_mod, b_mod, w_in, q_norm_g, w_uq, kv_norm_g, w_ukv,
           mlstm_gate_b, mlstm_norm_g, rg_conv_w, rg_conv_b, rg_wa, rg_ba, rg_wx, rg_bx, rg_lam,
           w_branch, w_out, w_mlp1, w_mlp2, final_g):
    b, s, d = x.shape
    lc = ctx.shape[1]
    depth = w_in.shape[0]
    rows_x, rows_c = b * s, b * lc
    r = rows_x + rows_c

    tm_x = _pick(s, (1024, 512, 256, 128))
    tm_c = _pick(math.gcd(rows_x, rows_c), (512, 256, 128))
    tm_mid = _pick(math.gcd(rows_x, rows_c), (512, 256, 128))
    tm_seq = _pick(math.gcd(s, lc), (256, 128))
    tq = _pick(s, (512, 256, 128))
    span_x = (tm_x, 0, rows_x // tm_x)
    span_c = (tm_c, rows_x // tm_c, rows_c // tm_c)

    def mod_row(row0):
        return jnp.where(row0 < rows_x, 1 + row0 // s, 0)

    def dense(fn, with_ctx):
        out = fn(span_x, None)
        return fn(span_c, out) if with_ctx else out

    xu = jnp.concatenate([x.reshape(rows_x, d), ctx.reshape(rows_c, d)], axis=0)
    cc = jnp.concatenate([c_ctx[None, :], c, jnp.zeros((SUBLANE - 1 - b, d), F32)], axis=0)
    tabs = _rope_tables(b, s, lc)

    for l in range(depth):
        last = l == depth - 1
        w = _layer_weights(l, d, w_in, w_uq, w_ukv, mlstm_gate_b, rg_wa, rg_wx, w_branch, w_out,
                           w_mlp1, w_mlp2)
        mods3 = _modulation(cc, w_mod[l], b_mod[l]).reshape(SUBLANE, 1, N_MOD * d)
        z = dense(lambda span, prev: _in_projection(xu, mods3, norm1_g[l], w["w_in"], mod_row, span, prev),
                  True)

        qt, kk, vt = _mla_prep(z, tabs, q_norm_g[l], kv_norm_g[l], w["wqt"], w["wqrt"], w["wk"],
                               w["wvt"], b, s, lc, tm_seq)
        ya = _attn_call(qt, kk, vt, b=b, nq=s // tq, tq=tq, q_blk0=0, kv_len=s + lc, kv_blk0=0,
                        out_rows=r, row_blk=lambda bb, i: bb * (s // tq) + i, prev=None)
        if not last:
            ya = _attn_call(qt, kk, vt, b=b, nq=1, tq=lc, q_blk0=s // lc, kv_len=lc, kv_blk0=s // lc,
                            out_rows=r, row_blk=lambda bb, i: rows_x // lc + bb, prev=ya)

        gc, gr = _mlstm_gates(z, w["gate_b"])
        mhf, mhb = _mlstm(z, gc, gr, b, s, lc)

        af, bf, ab, bb_ = _rg_coeffs(z, rg_conv_w[l], rg_conv_b[l], w["w_rg"], rg_ba[l], rg_bx[l],
                                     rg_lam[l], b, s, lc, tm_seq)
        rhf, rhb = _rg_scan(af, bf, ab, bb_, b, s, lc, tm_seq)

        rows = rows_x if last else r
        m = _merge(rows, d, ya, mhf, mhb, z, mlstm_norm_g[l], rhf, rhb, w["wb"], tm_mid)
        x1 = dense(lambda span, prev: _out_projection(rows, xu, m, w["w_out"], mods3, mod_row, span, prev),
                   not last)
        xu = dense(lambda span, prev: _mlp(rows, x1, mods3, norm2_g[l], w["w1"], w["w2"], final_g, mod_row,
                                           span, prev, last), not last)

    return xu.reshape(b, s, d)
```

```python
import functools
import math

import jax
import jax.numpy as jnp
from jax import lax
from jax.experimental import pallas as pl
from jax.experimental.pallas import tpu as pltpu

F32 = jnp.float32
BF16 = jnp.bfloat16

HEADS = 8
Q_LORA = 512
KV_LORA = 256
QK_NOPE = 128
QK_ROPE = 64
V_HEAD = 128
QK_DIM = QK_NOPE + QK_ROPE
ROPE_BASE = 10000.0
GRID_W = 64
ML_DQK = 128
ML_CHUNK = 128
WIDTH = 1024
RG_BLOCKS = 8
RG_BW = WIDTH // RG_BLOCKS
RG_CONV = 4
RG_C = 8.0
N_MOD = 6
EPS = 1e-6

LANE = 128
SUBLANE = 8
VT_ROWS = V_HEAD + 16
ML_ROWS = ML_DQK + 16

Z_Q = 0
Z_KV = Z_Q + Q_LORA
Z_KR = Z_KV + KV_LORA
Z_KRP = Z_KR + QK_ROPE
Z_MLA_END = Z_KRP + QK_ROPE
Z_G = Z_MLA_END
Z_MQ = Z_G + LANE
Z_MK = Z_MQ + WIDTH
Z_MV = Z_MK + WIDTH
Z_MO = Z_MV + WIDTH
Z_RX = Z_MO + WIDTH
Z_RG = Z_RX + WIDTH
Z_MERGE = Z_RG + WIDTH

Q_SCALE = (QK_DIM ** -0.5) * math.log2(math.e)
VMEM_LIMIT = 56 * 1024 * 1024


def _cparams(sem, vmem=None):
    return pltpu.CompilerParams(dimension_semantics=sem, vmem_limit_bytes=vmem)


def _pick(n, cands):
    for c in cands:
        if n % c == 0:
            return c
    raise ValueError(f"no tile in {cands} divides {n}")


def _pcall(kernel, *, prev, in_specs, args, **kw):
    if prev is None:
        return pl.pallas_call(kernel, in_specs=in_specs, **kw)(*args)
    n_in = len(args)

    def aliased(*refs):
        kernel(*refs[:n_in], *refs[n_in + 1:])

    return pl.pallas_call(aliased, in_specs=in_specs + [pl.BlockSpec(memory_space=pl.ANY)],
                          input_output_aliases={n_in: 0}, **kw)(*args, prev)


def _rms(x, g):
    return x * lax.rsqrt(jnp.mean(x * x, axis=-1, keepdims=True) + EPS) * g


def _sigmoid(x):
    return 0.5 * jnp.tanh(0.5 * x) + 0.5


def _softplus(x):
    return jnp.maximum(x, 0.0) + jnp.log1p(jnp.exp(-jnp.abs(x)))


def _log_sigmoid(x):
    return jnp.minimum(x, 0.0) - jnp.log1p(jnp.exp(-jnp.abs(x)))


def _gelu_tanh(x):
    return 0.5 * x * (1.0 + jnp.tanh(math.sqrt(2.0 / math.pi) * (x + 0.044715 * (x * x * x))))


def _mod_kernel(c_ref, w_ref, b_ref, o_ref):
    c = c_ref[...]
    sc = c * jax.nn.sigmoid(c)
    o_ref[...] = jnp.dot(sc, w_ref[...], preferred_element_type=F32,
                         precision=lax.Precision.HIGHEST) + b_ref[...]


def _modulation(cc, w_mod, b_mod):
    rows, d = cc.shape
    n = w_mod.shape[1]
    tn = _pick(n, (1024, 512, 256, 128))
    return pl.pallas_call(
        _mod_kernel,
        grid=(n // tn,),
        in_specs=[pl.BlockSpec((rows, d), lambda j: (0, 0)),
                  pl.BlockSpec((d, tn), lambda j: (0, j)),
                  pl.BlockSpec((1, tn), lambda j: (0, j))],
        out_specs=pl.BlockSpec((rows, tn), lambda j: (0, j)),
        out_shape=jax.ShapeDtypeStruct((rows, n), F32),
        compiler_params=_cparams(("parallel",), VMEM_LIMIT),
        name="adaln_mod",
    )(cc, w_mod, b_mod.reshape(1, n))


def _inproj_kernel(x_ref, mod_ref, g_ref, w_ref, z_ref, xn_ref, *, d):
    @pl.when(pl.program_id(1) == 0)
    def _():
        y = _rms(x_ref[...], g_ref[...])
        shift = mod_ref[0, :, 0:d]
        scale = mod_ref[0, :, d:2 * d]
        xn_ref[...] = (y * (1.0 + scale) + shift).astype(BF16)

    z_ref[...] = jnp.dot(xn_ref[...], w_ref[...], preferred_element_type=F32)


def _in_projection(r, src, mods3, g, w_in_p, mod_row, span, prev):
    tm, blk0, nblk = span
    xs, src0 = src
    d = xs.shape[1]
    n = w_in_p.shape[1]
    tn = _pick(n, (1024, 512, 256, 128))
    return _pcall(
        functools.partial(_inproj_kernel, d=d),
        prev=prev,
        grid=(nblk, n // tn),
        in_specs=[pl.BlockSpec((tm, d), lambda i, j: (src0 + i, 0)),
                  pl.BlockSpec((1, 1, N_MOD * d), lambda i, j: (mod_row((blk0 + i) * tm), 0, 0)),
                  pl.BlockSpec((1, d), lambda i, j: (0, 0)),
                  pl.BlockSpec((d, tn), lambda i, j: (0, j))],
        args=[xs, mods3, g.reshape(1, d), w_in_p],
        out_specs=pl.BlockSpec((tm, tn), lambda i, j: (blk0 + i, j)),
        out_shape=jax.ShapeDtypeStruct((r, n), F32),
        scratch_shapes=[pltpu.VMEM((tm, d), BF16)],
        compiler_params=_cparams(("parallel", "arbitrary"), VMEM_LIMIT),
        name="in_proj",
    )


def _mla_prep_kernel(z_ref, cos_ref, sin_ref, cost_ref, sint_ref, gq_ref, gkv_ref,
                     wqt_ref, wqrt_ref, wk_ref, wvt_ref, qt_ref, k_ref, vt_ref):
    tm = z_ref.shape[0]
    qn = _rms(z_ref[:, Z_Q:Z_Q + Q_LORA], gq_ref[...]).astype(BF16)
    kvn = _rms(z_ref[:, Z_KV:Z_KV + KV_LORA], gkv_ref[...]).astype(BF16)
    kr = (z_ref[:, Z_KR:Z_KR + QK_ROPE] * cos_ref[...]
          + z_ref[:, Z_KRP:Z_KRP + QK_ROPE] * sin_ref[...]).astype(BF16)
    kn = jnp.dot(kvn, wk_ref[...], preferred_element_type=F32)
    nt = (((1,), (1,)), ((), ()))
    ones = jnp.ones((VT_ROWS - V_HEAD, tm), BF16)
    for h in range(HEADS):
        q_t = lax.dot_general(wqt_ref[h], qn, nt, preferred_element_type=F32)
        qr_t = lax.dot_general(wqrt_ref[h], qn, nt, preferred_element_type=F32)
        q_rot = q_t[QK_NOPE:QK_DIM] * cost_ref[...] + qr_t * sint_ref[...]
        qt_ref[0, h, 0:QK_NOPE, :] = (q_t[0:QK_NOPE] * Q_SCALE).astype(BF16)
        qt_ref[0, h, QK_NOPE:QK_DIM, :] = (q_rot * Q_SCALE).astype(BF16)
        k_ref[0, h, :, 0:QK_NOPE] = kn[:, h * QK_NOPE:(h + 1) * QK_NOPE].astype(BF16)
        k_ref[0, h, :, QK_NOPE:QK_DIM] = kr
        v_t = lax.dot_general(wvt_ref[h], kvn, nt, preferred_element_type=F32)
        vt_ref[0, h, 0:V_HEAD, :] = v_t.astype(BF16)
        vt_ref[0, h, V_HEAD:VT_ROWS, :] = ones


def _mla_prep(z, tabs, gq, gkv, wqt, wqrt, wk, wvt, b, s, lc, tm):
    r = z.shape[0]
    cos_u, sin_u, cost_u, sint_u = tabs
    nxt, nct = s // tm, lc // tm
    ltot = s + lc

    def bidx(i):
        j = i - b * nxt
        return jnp.where(i < b * nxt, i // nxt, j // nct)

    def sblk(i):
        j = i - b * nxt
        return jnp.where(i < b * nxt, i % nxt, nxt + j % nct)

    full = lambda shape: pl.BlockSpec(shape, lambda i: (0,) * len(shape))
    return pl.pallas_call(
        _mla_prep_kernel,
        grid=(r // tm,),
        in_specs=[pl.BlockSpec((tm, Z_MLA_END), lambda i: (i, 0)),
                  pl.BlockSpec((tm, QK_ROPE), lambda i: (i, 0)),
                  pl.BlockSpec((tm, QK_ROPE), lambda i: (i, 0)),
                  pl.BlockSpec((QK_ROPE, tm), lambda i: (0, i)),
                  pl.BlockSpec((QK_ROPE, tm), lambda i: (0, i)),
                  full((1, Q_LORA)), full((1, KV_LORA)),
                  full(wqt.shape), full(wqrt.shape), full(wk.shape), full(wvt.shape)],
        out_specs=[pl.BlockSpec((1, HEADS, QK_DIM, tm), lambda i: (bidx(i), 0, 0, sblk(i))),
                   pl.BlockSpec((1, HEADS, tm, QK_DIM), lambda i: (bidx(i), 0, sblk(i), 0)),
                   pl.BlockSpec((1, HEADS, VT_ROWS, tm), lambda i: (bidx(i), 0, 0, sblk(i)))],
        out_shape=[jax.ShapeDtypeStruct((b, HEADS, QK_DIM, ltot), BF16),
                   jax.ShapeDtypeStruct((b, HEADS, ltot, QK_DIM), BF16),
                   jax.ShapeDtypeStruct((b, HEADS, VT_ROWS, ltot), BF16)],
        compiler_params=_cparams(("parallel",), VMEM_LIMIT),
        name="mla_prep",
    )(z, cos_u, sin_u, cost_u, sint_u, gq.reshape(1, -1), gkv.reshape(1, -1), wqt, wqrt, wk, wvt)


def _attn_kernel(qt_ref, k_ref, vt_ref, o_ref, s_ref, *, tk, nk):
    qt = qt_ref[0, 0]
    tq = qt.shape[1]

    def scores(c, slot):
        off = pl.multiple_of(c * tk, tk)
        s = jnp.dot(k_ref[0, 0, pl.ds(off, tk), :], qt, preferred_element_type=F32)
        s_ref[slot] = s
        return jnp.max(s, axis=0, keepdims=True)

    def update(c, slot, m, acc, cmax):
        off = pl.multiple_of(c * tk, tk)
        m_new = jnp.maximum(m, cmax)
        p = jnp.exp2(s_ref[slot] - m_new).astype(BF16)
        alpha = jnp.exp2(m - m_new)
        pv = jnp.dot(vt_ref[0, 0, :, pl.ds(off, tk)], p, preferred_element_type=F32)
        return m_new, alpha * acc + pv

    def body(i, carry):
        m, acc, cmax0 = carry
        c = 2 * i
        cmax1 = scores(c + 1, 1)
        m, acc = update(c, 0, m, acc, cmax0)
        cmax0 = scores(c + 2, 0)
        m, acc = update(c + 1, 1, m, acc, cmax1)
        return m, acc, cmax0

    m0 = jnp.full((1, tq), -jnp.inf, F32)
    acc0 = jnp.zeros((VT_ROWS, tq), F32)
    m, acc, cmax0 = lax.fori_loop(0, (nk - 1) // 2, body, (m0, acc0, scores(0, 0)))
    if nk % 2 == 1:
        _, acc = update(nk - 1, 0, m, acc, cmax0)
    else:
        cmax1 = scores(nk - 1, 1)
        m, acc = update(nk - 2, 0, m, acc, cmax0)
        _, acc = update(nk - 1, 1, m, acc, cmax1)
    o = acc[0:V_HEAD] / acc[V_HEAD:V_HEAD + 1]
    o_ref[...] = o.T.astype(o_ref.dtype)


def _attn_call(qt, k, vt, *, b, nq, tq, q_blk0, kv_len, kv_blk0, out_rows, row_blk, prev):
    tk = _pick(kv_len, (1280, 1024, 512, 256, 128))
    nk = kv_len // tk
    return _pcall(
        functools.partial(_attn_kernel, tk=tk, nk=nk),
        prev=prev,
        grid=(b, HEADS, nq),
        in_specs=[pl.BlockSpec((1, 1, QK_DIM, tq), lambda bb, h, i: (bb, h, 0, q_blk0 + i)),
                  pl.BlockSpec((1, 1, kv_len, QK_DIM), lambda bb, h, i: (bb, h, kv_blk0, 0)),
                  pl.BlockSpec((1, 1, VT_ROWS, kv_len), lambda bb, h, i: (bb, h, 0, kv_blk0))],
        args=[qt, k, vt],
        out_specs=pl.BlockSpec((tq, V_HEAD), lambda bb, h, i: (row_blk(bb, i), h)),
        out_shape=jax.ShapeDtypeStruct((out_rows, WIDTH), BF16),
        scratch_shapes=[pltpu.VMEM((2, tk, tq), F32)],
        compiler_params=_cparams(("parallel", "parallel", "arbitrary"), VMEM_LIMIT),
        name="mla_attn",
    )


def _mlstm_gates_kernel(z_ref, b_ref, gc_ref, gr_ref):
    g = z_ref[...] + b_ref[...]
    n = g.shape[0]
    lane = lax.broadcasted_iota(jnp.int32, g.shape, 1)
    row = lax.broadcasted_iota(jnp.int32, g.shape, 0)
    lf = _log_sigmoid(g)
    pre = lf
    suf = lf
    k = 1
    while k < n:
        pre = pre + jnp.where(row >= k, pltpu.roll(pre, k, 0), 0.0)
        suf = suf + jnp.where(row < n - k, pltpu.roll(suf, n - k, 0), 0.0)
        k *= 2
    bwd = lane >= 2 * HEADS
    cum = jnp.where(bwd, suf, pre)
    r = g - pltpu.roll(cum, LANE - HEADS, 1)
    pmax = r
    smax = r
    k = 1
    while k < n:
        pmax = jnp.maximum(pmax, jnp.where(row >= k, pltpu.roll(pmax, k, 0), -jnp.inf))
        smax = jnp.maximum(smax, jnp.where(row < n - k, pltpu.roll(smax, n - k, 0), -jnp.inf))
        k *= 2
    cmax = jnp.where(bwd, smax, pmax)
    is_f = ((lane >= HEADS) & (lane < 2 * HEADS)) | ((lane >= 3 * HEADS) & (lane < 4 * HEADS))
    gc = jnp.where(is_f, cum, g)
    gc = jnp.where((lane >= 4 * HEADS) & (lane < 8 * HEADS), pltpu.roll(cmax, 4 * HEADS, 1), gc)
    gc = jnp.where((lane >= 8 * HEADS) & (lane < 12 * HEADS), pltpu.roll(r, 8 * HEADS, 1), gc)
    gc_ref[...] = gc
    gr_ref[0] = gc.T


def _mlstm_gates(z, bias_pad):
    r = z.shape[0]
    nchunk = r // ML_CHUNK
    return pl.pallas_call(
        _mlstm_gates_kernel,
        grid=(nchunk,),
        in_specs=[pl.BlockSpec((ML_CHUNK, LANE), lambda i: (i, Z_G // LANE)),
                  pl.BlockSpec((1, LANE), lambda i: (0, 0))],
        out_specs=[pl.BlockSpec((ML_CHUNK, LANE), lambda i: (i, 0)),
                   pl.BlockSpec((1, LANE, ML_CHUNK), lambda i: (i, 0, 0))],
        out_shape=[jax.ShapeDtypeStruct((r, LANE), F32),
                   jax.ShapeDtypeStruct((nchunk, LANE, ML_CHUNK), F32)],
        compiler_params=_cparams(("parallel",)),
        name="mlstm_gates",
    )(z, bias_pad)


def _mlstm_kernel(qf_ref, kf_ref, vf_ref, gcf_ref, grf_ref,
                  qb_ref, kb_ref, vb_ref, gcb_ref, grb_ref,
                  hf_ref, hb_ref, c_ref, m_ref):
    L = ML_CHUNK

    @pl.when(pl.program_id(1) == 0)
    def _():
        c_ref[...] = jnp.zeros(c_ref.shape, F32)
        m_ref[...] = jnp.zeros(m_ref.shape, F32)

    row = lax.broadcasted_iota(jnp.int32, (L, L), 0)
    col = lax.broadcasted_iota(jnp.int32, (L, L), 1)
    nt = (((1,), (1,)), ((), ()))
    ones_rows = jnp.ones((ML_ROWS - ML_DQK, L), BF16)
    dirs = ((qf_ref, kf_ref, vf_ref, gcf_ref, grf_ref, hf_ref, row <= col),
            (qb_ref, kb_ref, vb_ref, gcb_ref, grb_ref, hb_ref, row >= col))
    ln_ks = math.log(ML_DQK ** -0.5)
    st = []
    for d, (q_ref, k_ref, v_ref, gc_ref, gr_ref, o_ref, mask) in enumerate(dirs):
        gc = gc_ref[...]
        gr = gr_ref[0]
        for h in range(HEADS):
            li = 2 * HEADS * d + h
            lb = li + HEADS
            lm = li + 4 * HEADS
            lr = li + 8 * HEADS
            brow = gr[lb:lb + 1, :]
            g_tot = brow[:, L - 1:L] if d == 0 else brow[:, 0:1]
            wend = g_tot - brow + gr[li:li + 1, :]
            a = jnp.max(wend, axis=1, keepdims=True)
            m_old = m_ref[d * HEADS + h][0:1, 0:1]
            urow = jnp.maximum(m_old, gr[lm:lm + 1, :])
            m_new = jnp.maximum(g_tot + m_old, a)
            st.append(dict(
                idx=d * HEADS + h, sl=slice(h * ML_DQK, (h + 1) * ML_DQK),
                q_ref=q_ref, k_ref=k_ref, v_ref=v_ref, o_ref=o_ref, mask=mask,
                e=jnp.exp(wend - a + ln_ks), urow=urow, rcol=gc[:, lr:lr + 1],
                di=jnp.exp(m_old - urow), em=jnp.exp(-(brow + urow)), m_new=m_new,
                dec=jnp.exp(g_tot + m_old - m_new), inp=jnp.exp(a - m_new)))
    for t in st:
        sl = t["sl"]
        q = t["q_ref"][:, sl].astype(BF16)
        k = t["k_ref"][:, sl].astype(BF16)
        v_t = t["v_ref"][:, sl].T
        t["vext"] = jnp.concatenate([v_t.astype(BF16), ones_rows], axis=0)
        vext_e = jnp.concatenate([(v_t * t["e"]).astype(BF16), jnp.broadcast_to(t["e"], ones_rows.shape).astype(BF16)],
                                 axis=0)
        t["sT"] = lax.dot_general(k, q, nt, preferred_element_type=F32)
        t["d_c"] = jnp.dot(vext_e, k, preferred_element_type=F32)
        t["qc"] = lax.dot_general(c_ref[t["idx"]].astype(BF16), q, nt, preferred_element_type=F32)
    for t in st:
        dexp = jnp.exp(jnp.where(t["mask"], t["rcol"] - t["urow"] + ln_ks, -jnp.inf))
        t["p"] = (t["sT"] * dexp).astype(BF16)
    for t in st:
        res = t["di"] * t["qc"] + jnp.dot(t["vext"], t["p"], preferred_element_type=F32)
        den = res[ML_DQK:ML_DQK + 1, :]
        h_t = res[0:ML_DQK, :] / jnp.maximum(jnp.abs(den), t["em"])
        t["o_ref"][:, t["sl"]] = h_t.T
        c_ref[t["idx"]] = t["dec"] * c_ref[t["idx"]] + t["inp"] * t["d_c"]
        m_ref[t["idx"]] = jnp.broadcast_to(t["m_new"], (SUBLANE, LANE))


def _mlstm(z, gc, gr, b, s, lc):
    r = z.shape[0]
    L = ML_CHUNK
    nxc, ncc = s // L, lc // L
    x0 = lambda bb: bb * nxc
    c0 = lambda bb: b * nxc + bb * ncc

    def blk_f(bb, st):
        return jnp.where(st < ncc, c0(bb) + st, x0(bb) + st - ncc)

    def blk_b(bb, st):
        return jnp.where(st < ncc, c0(bb) + ncc - 1 - st, x0(bb) + nxc - 1 - (st - ncc))

    def zspec(blk, cb):
        return pl.BlockSpec((L, WIDTH), lambda bb, st: (blk(bb, st), cb))

    def dir_specs(blk):
        return [zspec(blk, Z_MQ // WIDTH), zspec(blk, Z_MK // WIDTH), zspec(blk, Z_MV // WIDTH),
                pl.BlockSpec((L, LANE), lambda bb, st: (blk(bb, st), 0)),
                pl.BlockSpec((1, LANE, L), lambda bb, st: (blk(bb, st), 0, 0))]

    return pl.pallas_call(
        _mlstm_kernel,
        grid=(b, nxc + ncc),
        in_specs=dir_specs(blk_f) + dir_specs(blk_b),
        out_specs=[pl.BlockSpec((L, WIDTH), lambda bb, st: (blk_f(bb, st), 0)),
                   pl.BlockSpec((L, WIDTH), lambda bb, st: (blk_b(bb, st), 0))],
        out_shape=[jax.ShapeDtypeStruct((r, WIDTH), F32), jax.ShapeDtypeStruct((r, WIDTH), F32)],
        scratch_shapes=[pltpu.VMEM((2 * HEADS, ML_ROWS, ML_DQK), F32),
                        pltpu.VMEM((2 * HEADS, SUBLANE, LANE), F32)],
        compiler_params=_cparams(("arbitrary", "arbitrary"), VMEM_LIMIT),
        name="mlstm",
    )(z, z, z, gc, gr, z, z, z, gc, gr)


def _rg_ab_kernel(cur_ref, prev_ref, next_ref, cw_ref, cb_ref, w_ref, ba_ref, bx_ref, lam_ref,
                  af_ref, bf_ref, ab_ref, bb_ref, xe_ref, *, tm, s_len, c_len, rows_x):
    row0 = pl.program_id(0) * tm
    in_x = row0 < rows_x
    seq = jnp.where(in_x, s_len, c_len)
    off = jnp.where(in_x, row0, row0 - rows_x)
    first = lax.rem(off, seq) == 0
    last = lax.rem(off + tm, seq) == 0
    xe_ref[0:SUBLANE, :] = jnp.where(first, 0.0, prev_ref[...])
    xe_ref[SUBLANE:SUBLANE + tm, :] = cur_ref[...]
    xe_ref[SUBLANE + tm:2 * SUBLANE + tm, :] = jnp.where(last, 0.0, next_ref[...])
    lp = RG_CONV // 2
    xc = cb_ref[...] + cw_ref[0:1, :] * xe_ref[pl.ds(SUBLANE - lp, tm), :]
    for j in range(1, RG_CONV):
        xc = xc + cw_ref[j:j + 1, :] * xe_ref[pl.ds(SUBLANE - lp + j, tm), :]
    outs = ((af_ref, bf_ref), (ab_ref, bb_ref))
    for g in range(RG_BLOCKS):
        sl = slice(g * RG_BW, (g + 1) * RG_BW)
        xg = xc[:, sl]
        o = jnp.dot(xg.astype(BF16), w_ref[g], preferred_element_type=F32)
        for d in range(2):
            r = _sigmoid(o[:, (2 * d) * RG_BW:(2 * d + 1) * RG_BW] + ba_ref[d:d + 1, sl])
            i = _sigmoid(o[:, (2 * d + 1) * RG_BW:(2 * d + 2) * RG_BW] + bx_ref[d:d + 1, sl])
            log_a = (-RG_C) * r * _softplus(-lam_ref[d:d + 1, sl])
            a = jnp.exp(log_a)
            one_m_a2 = -jnp.tanh(log_a) * (a * a + 1.0)
            outs[d][0][:, sl] = a
            outs[d][1][:, sl] = jnp.sqrt(one_m_a2) * (i * xg)


def _rg_coeffs(z, cw, cb, w_rg, ba, bx, lam, b, s, lc, tm):
    r = z.shape[0]
    per = tm // SUBLANE
    nblk8 = r // SUBLANE
    cbk = Z_RX // WIDTH
    full = lambda shape: pl.BlockSpec(shape, lambda i: (0,) * len(shape))
    kernel = functools.partial(_rg_ab_kernel, tm=tm, s_len=s, c_len=lc, rows_x=b * s)
    o_spec = pl.BlockSpec((tm, WIDTH), lambda i: (i, 0))
    o_shape = jax.ShapeDtypeStruct((r, WIDTH), F32)
    return pl.pallas_call(
        kernel,
        grid=(r // tm,),
        in_specs=[pl.BlockSpec((tm, WIDTH), lambda i: (i, cbk)),
                  pl.BlockSpec((SUBLANE, WIDTH), lambda i: (jnp.maximum(i * per - 1, 0), cbk)),
                  pl.BlockSpec((SUBLANE, WIDTH), lambda i: (jnp.minimum((i + 1) * per, nblk8 - 1), cbk)),
                  full((RG_CONV, WIDTH)), full((1, WIDTH)), full(w_rg.shape),
                  full((2, WIDTH)), full((2, WIDTH)), full((2, WIDTH))],
        out_specs=[o_spec] * 4,
        out_shape=[o_shape] * 4,
        scratch_shapes=[pltpu.VMEM((tm + 2 * SUBLANE, WIDTH), F32)],
        compiler_params=_cparams(("parallel",), VMEM_LIMIT),
        name="rglru_coeffs",
    )(z, z, z, cw, cb.reshape(1, WIDTH), w_rg, ba, bx, lam)


def _rg_scan_kernel(af_ref, bf_ref, ab_ref, bb_ref, hf_ref, hb_ref, sf_ref, sb_ref, *, tt):
    @pl.when(pl.program_id(1) == 0)
    def _():
        sf_ref[...] = jnp.zeros(sf_ref.shape, F32)
        sb_ref[...] = jnp.zeros(sb_ref.shape, F32)

    def body(t, carry):
        hf, hb = carry
        hf = af_ref[pl.ds(t, 1), :] * hf + bf_ref[pl.ds(t, 1), :]
        hf_ref[pl.ds(t, 1), :] = hf
        tb = tt - 1 - t
        hb = ab_ref[pl.ds(tb, 1), :] * hb + bb_ref[pl.ds(tb, 1), :]
        hb_ref[pl.ds(tb, 1), :] = hb
        return hf, hb

    hf, hb = lax.fori_loop(0, tt, body, (sf_ref[...], sb_ref[...]), unroll=8)
    sf_ref[...] = hf
    sb_ref[...] = hb


def _rg_scan(af, bf, ab, bb, b, s, lc, tt):
    r = af.shape[0]
    nxt, nct = s // tt, lc // tt
    x0 = lambda bi: bi * nxt
    c0 = lambda bi: b * nxt + bi * nct

    def blk_f(bi, st):
        return jnp.where(st < nct, c0(bi) + st, x0(bi) + st - nct)

    def blk_b(bi, st):
        return jnp.where(st < nct, c0(bi) + nct - 1 - st, x0(bi) + nxt - 1 - (st - nct))

    sf = pl.BlockSpec((tt, WIDTH), lambda bi, st: (blk_f(bi, st), 0))
    sb = pl.BlockSpec((tt, WIDTH), lambda bi, st: (blk_b(bi, st), 0))
    o_shape = jax.ShapeDtypeStruct((r, WIDTH), F32)
    return pl.pallas_call(
        functools.partial(_rg_scan_kernel, tt=tt),
        grid=(b, nxt + nct),
        in_specs=[sf, sf, sb, sb],
        out_specs=[sf, sb],
        out_shape=[o_shape, o_shape],
        scratch_shapes=[pltpu.VMEM((1, WIDTH), F32), pltpu.VMEM((1, WIDTH), F32)],
        compiler_params=_cparams(("arbitrary", "arbitrary")),
        name="rglru_scan",
    )(af, bf, ab, bb)


def _merge_kernel(ya_ref, mhf_ref, mhb_ref, zo_ref, ng_ref, rhf_ref, rhb_ref, zg_ref,
                  zm0_ref, zm1_ref, zm2_ref, w0_ref, w1_ref, w2_ref, m_ref, yb_ref, yr_ref):
    @pl.when(pl.program_id(1) == 0)
    def _():
        hsum = mhf_ref[...] + mhb_ref[...]
        for h in range(HEADS):
            sl = slice(h * V_HEAD, (h + 1) * V_HEAD)
            hn = _rms(hsum[:, sl], ng_ref[:, sl])
            yb_ref[:, sl] = (hn * _sigmoid(zo_ref[:, sl])).astype(BF16)
        yr_ref[...] = ((rhf_ref[...] + rhb_ref[...]) * _gelu_tanh(zg_ref[...])).astype(BF16)

    m = (_sigmoid(zm0_ref[...]) * jnp.dot(ya_ref[...], w0_ref[0], preferred_element_type=F32)
         + _sigmoid(zm1_ref[...]) * jnp.dot(yb_ref[...], w1_ref[0], preferred_element_type=F32)
         + _sigmoid(zm2_ref[...]) * jnp.dot(yr_ref[...], w2_ref[0], preferred_element_type=F32))
    m_ref[...] = m.astype(BF16)


def _merge(rows, d, ya, mhf, mhb, z, ng, rhf, rhb, wb, tm):
    tn = _pick(d, (512, 256, 128))
    wide = lambda: pl.BlockSpec((tm, WIDTH), lambda i, j: (i, 0))
    zblk = lambda cb: pl.BlockSpec((tm, WIDTH), lambda i, j: (i, cb))
    zm = lambda br: pl.BlockSpec((tm, tn), lambda i, j: (i, (Z_MERGE + br * d) // tn + j))
    wspec = lambda br: pl.BlockSpec((1, WIDTH, tn), lambda i, j: (br, 0, j))
    return pl.pallas_call(
        _merge_kernel,
        grid=(rows // tm, d // tn),
        in_specs=[wide(), wide(), wide(), zblk(Z_MO // WIDTH),
                  pl.BlockSpec((1, WIDTH), lambda i, j: (0, 0)),
                  wide(), wide(), zblk(Z_RG // WIDTH),
                  zm(0), zm(1), zm(2), wspec(0), wspec(1), wspec(2)],
        out_specs=pl.BlockSpec((tm, tn), lambda i, j: (i, j)),
        out_shape=jax.ShapeDtypeStruct((rows, d), BF16),
        scratch_shapes=[pltpu.VMEM((tm, WIDTH), BF16), pltpu.VMEM((tm, WIDTH), BF16)],
        compiler_params=_cparams(("parallel", "arbitrary"), VMEM_LIMIT),
        name="branch_merge",
    )(ya, mhf, mhb, z, ng.reshape(1, WIDTH), rhf, rhb, z, z, z, z, wb, wb, wb)


def _outproj_kernel(m_ref, w_ref, x_ref, gate_ref, o_ref):
    o_ref[...] = x_ref[...] + gate_ref[0] * jnp.dot(m_ref[...], w_ref[...], preferred_element_type=F32)


def _out_projection(rows, src, m, w_out, mods3, mod_row, span, prev):
    tm, blk0, nblk = span
    xs, src0 = src
    d = xs.shape[1]
    tn = _pick(d, (1024, 512, 256, 128))
    return _pcall(
        _outproj_kernel,
        prev=prev,
        grid=(nblk, d // tn),
        in_specs=[pl.BlockSpec((tm, d), lambda i, j: (blk0 + i, 0)),
                  pl.BlockSpec((d, tn), lambda i, j: (0, j)),
                  pl.BlockSpec((tm, tn), lambda i, j: (src0 + i, j)),
                  pl.BlockSpec((1, 1, tn), lambda i, j: (mod_row((blk0 + i) * tm), 0, (2 * d) // tn + j))],
        args=[m, w_out, xs, mods3],
        out_specs=pl.BlockSpec((tm, tn), lambda i, j: (blk0 + i, j)),
        out_shape=jax.ShapeDtypeStruct((rows, d), F32),
        compiler_params=_cparams(("parallel", "parallel"), VMEM_LIMIT),
        name="out_proj",
    )


def _mlp_kernel(x_ref, mod_ref, g_ref, w1_ref, w2_ref, fg_ref, o_ref, xn_ref, *, d, final):
    j = pl.program_id(1)

    @pl.when(j == 0)
    def _():
        y = _rms(x_ref[...], g_ref[...])
        shift = mod_ref[0, :, 3 * d:4 * d]
        scale = mod_ref[0, :, 4 * d:5 * d]
        xn_ref[...] = (y * (1.0 + scale) + shift).astype(BF16)
        o_ref[...] = jnp.zeros(o_ref.shape, F32)

    h = jnp.dot(xn_ref[...], w1_ref[...], preferred_element_type=F32)
    h = jnp.square(jnp.maximum(h, 0.0)).astype(BF16)
    o_ref[...] += jnp.dot(h, w2_ref[...], preferred_element_type=F32)

    @pl.when(j == pl.num_programs(1) - 1)
    def _():
        out = x_ref[...] + mod_ref[0, :, 5 * d:6 * d] * o_ref[...]
        if final:
            out = _rms(out, fg_ref[...])
        o_ref[...] = out


def _mlp(rows, x1, mods3, g, w1, w2, fg, mod_row, span, prev, final):
    tm, blk0, nblk = span
    d = x1.shape[1]
    dff = w1.shape[1]
    tf = _pick(dff, (512, 256, 128))
    return _pcall(
        functools.partial(_mlp_kernel, d=d, final=final),
        prev=prev,
        grid=(nblk, dff // tf),
        in_specs=[pl.BlockSpec((tm, d), lambda i, j: (blk0 + i, 0)),
                  pl.BlockSpec((1, 1, N_MOD * d), lambda i, j: (mod_row((blk0 + i) * tm), 0, 0)),
                  pl.BlockSpec((1, d), lambda i, j: (0, 0)),
                  pl.BlockSpec((d, tf), lambda i, j: (0, j)),
                  pl.BlockSpec((tf, d), lambda i, j: (j, 0)),
                  pl.BlockSpec((1, d), lambda i, j: (0, 0))],
        args=[x1, mods3, g.reshape(1, d), w1, w2, fg.reshape(1, d)],
        out_specs=pl.BlockSpec((tm, d), lambda i, j: (blk0 + i, 0)),
        out_shape=jax.ShapeDtypeStruct((rows, d), F32),
        scratch_shapes=[pltpu.VMEM((tm, d), BF16)],
        compiler_params=_cparams(("parallel", "arbitrary"), VMEM_LIMIT),
        name="mlp",
    )


def _rope_perm_cols(w):
    n = QK_ROPE // 4
    return jnp.concatenate([-w[..., n:2 * n], w[..., 0:n], -w[..., 3 * n:4 * n], w[..., 2 * n:3 * n]], axis=-1)


def _layer_weights(l, d, w_in, w_uq, w_ukv, mlstm_gate_b, rg_wa, rg_wx, w_branch, w_out, w_mlp1, w_mlp2):
    wi = w_in[l]
    o = 0
    parts = {}
    for name, wdt in (("mla_q", Q_LORA), ("mla_kv", KV_LORA), ("mla_kr", QK_ROPE), ("ml_q", WIDTH),
                      ("ml_k", WIDTH), ("ml_v", WIDTH), ("ml_o", WIDTH), ("ml_g", 4 * HEADS),
                      ("rg_x", WIDTH), ("rg_gate", WIDTH), ("merge", 3 * d)):
        parts[name] = wi[:, o:o + wdt]
        o += wdt
    g_pad = jnp.pad(parts["ml_g"], ((0, 0), (0, LANE - 4 * HEADS)))
    w_in_p = jnp.concatenate(
        [parts["mla_q"], parts["mla_kv"], parts["mla_kr"], _rope_perm_cols(parts["mla_kr"]), g_pad,
         parts["ml_q"], parts["ml_k"], parts["ml_v"], parts["ml_o"], parts["rg_x"], parts["rg_gate"],
         parts["merge"]], axis=1).astype(BF16)
    wq = w_uq[l].reshape(Q_LORA, HEADS, QK_DIM)
    wqt = jnp.transpose(wq, (1, 2, 0)).astype(BF16)
    wqrt = jnp.transpose(_rope_perm_cols(wq[:, :, QK_NOPE:]), (1, 2, 0)).astype(BF16)
    wkv = w_ukv[l].reshape(KV_LORA, HEADS, QK_NOPE + V_HEAD)
    wk = wkv[:, :, :QK_NOPE].reshape(KV_LORA, HEADS * QK_NOPE).astype(BF16)
    wvt = jnp.transpose(wkv[:, :, QK_NOPE:], (1, 2, 0)).astype(BF16)
    gate_b = jnp.pad(mlstm_gate_b[l].reshape(1, 4 * HEADS), ((0, 0), (0, LANE - 4 * HEADS)))
    w_rg = jnp.concatenate([rg_wa[l, 0], rg_wx[l, 0], rg_wa[l, 1], rg_wx[l, 1]], axis=-1).astype(BF16)
    return dict(w_in=w_in_p, wqt=wqt, wqrt=wqrt, wk=wk, wvt=wvt, gate_b=gate_b, w_rg=w_rg,
                wb=w_branch[l].astype(BF16), w_out=w_out[l].astype(BF16),
                w1=w_mlp1[l].astype(BF16), w2=w_mlp2[l].astype(BF16))


def _rope_tables(b, s, lc):
    t = jnp.arange(s, dtype=jnp.int32)
    row = (t // GRID_W).astype(F32)
    col = (t % GRID_W).astype(F32)
    n_freq = QK_ROPE // 4
    inv = ROPE_BASE ** (-jnp.arange(n_freq, dtype=F32) / n_freq)
    ang_r = row[:, None] * inv[None, :]
    ang_c = col[:, None] * inv[None, :]
    cos = jnp.concatenate([jnp.cos(ang_r)] * 2 + [jnp.cos(ang_c)] * 2, axis=1)
    sin = jnp.concatenate([jnp.sin(ang_r)] * 2 + [jnp.sin(ang_c)] * 2, axis=1)
    cos_u = jnp.concatenate([jnp.tile(cos, (b, 1)), jnp.ones((b * lc, QK_ROPE), F32)], axis=0)
    sin_u = jnp.concatenate([jnp.tile(sin, (b, 1)), jnp.zeros((b * lc, QK_ROPE), F32)], axis=0)
    return cos_u, sin_u, cos_u.T, sin_u.T


def kernel(x, c, ctx, c_ctx, norm1_g, norm2_g, w_mod, b_mod, w_in, q_norm_g, w_uq, kv_norm_g, w_ukv,
           mlstm_gate_b, mlstm_norm_g, rg_conv_w, rg_conv_b, rg_wa, rg_ba, rg_wx, rg_bx, rg_lam,
           w_branch, w_out, w_mlp1, w_mlp2, final_g):
    b, s, d = x.shape
    lc = ctx.shape[1]
    depth = w_in.shape[0]
    rows_x, rows_c = b * s, b * lc
    r = rows_x + rows_c

    tm_x = _pick(s, (1024, 512, 256, 128))
    tm_c = _pick(math.gcd(rows_x, rows_c), (512, 256, 128))
    tm_mid = _pick(math.gcd(rows_x, rows_c), (512, 256, 128))
    tm_seq = _pick(math.gcd(s, lc), (256, 128))
    tq = _pick(s, (512, 256, 128))
    span_x = (tm_x, 0, rows_x // tm_x)
    span_c = (tm_c, rows_x // tm_c, rows_c // tm_c)

    def mod_row(row0):
        return jnp.where(row0 < rows_x, 1 + row0 // s, 0)

    def dense(fn, with_ctx):
        out = fn(span_x, src_x, None)
        return fn(span_c, src_c, out) if with_ctx else out

    src_x = (x.reshape(rows_x, d), 0)
    src_c = (ctx.reshape(rows_c, d), 0)
    cc = jnp.concatenate([c_ctx[None, :], c, jnp.zeros((SUBLANE - 1 - b, d), F32)], axis=0)
    tabs = _rope_tables(b, s, lc)

    for l in range(depth):
        last = l == depth - 1
        w = _layer_weights(l, d, w_in, w_uq, w_ukv, mlstm_gate_b, rg_wa, rg_wx, w_branch, w_out,
                           w_mlp1, w_mlp2)
        mods3 = _modulation(cc, w_mod[l], b_mod[l]).reshape(SUBLANE, 1, N_MOD * d)
        z = dense(lambda span, src, prev: _in_projection(r, src, mods3, norm1_g[l], w["w_in"], mod_row, span,
                                                         prev), True)

        qt, kk, vt = _mla_prep(z, tabs, q_norm_g[l], kv_norm_g[l], w["wqt"], w["wqrt"], w["wk"],
                               w["wvt"], b, s, lc, tm_seq)
        ya = _attn_call(qt, kk, vt, b=b, nq=s // tq, tq=tq, q_blk0=0, kv_len=s + lc, kv_blk0=0,
                        out_rows=r, row_blk=lambda bb, i: bb * (s // tq) + i, prev=None)
        if not last:
            ya = _attn_call(qt, kk, vt, b=b, nq=1, tq=lc, q_blk0=s // lc, kv_len=lc, kv_blk0=s // lc,
                            out_rows=r, row_blk=lambda bb, i: rows_x // lc + bb, prev=ya)

        gc, gr = _mlstm_gates(z, w["gate_b"])
        mhf, mhb = _mlstm(z, gc, gr, b, s, lc)

        af, bf, ab, bb_ = _rg_coeffs(z, rg_conv_w[l], rg_conv_b[l], w["w_rg"], rg_ba[l], rg_bx[l],
                                     rg_lam[l], b, s, lc, tm_seq)
        rhf, rhb = _rg_scan(af, bf, ab, bb_, b, s, lc, tm_seq)

        rows = rows_x if last else r
        m = _merge(rows, d, ya, mhf, mhb, z, mlstm_norm_g[l], rhf, rhb, w["wb"], tm_mid)
        x1 = dense(lambda span, src, prev: _out_projection(rows, src, m, w["w_out"], mods3, mod_row, span,
                                                           prev), not last)
        xu = dense(lambda span, src, prev: _mlp(rows, x1, mods3, norm2_g[l], w["w1"], w["w2"], final_g,
                                                mod_row, span, prev, last), not last)
        src_x = (xu, span_x[1])
        src_c = (xu, span_c[1])

    return xu.reshape(b, s, d)
```

```python
import functools
import math

import jax
import jax.numpy as jnp
from jax import lax
from jax.experimental import pallas as pl
from jax.experimental.pallas import tpu as pltpu

F32 = jnp.float32
BF16 = jnp.bfloat16

HEADS = 8
Q_LORA = 512
KV_LORA = 256
QK_NOPE = 128
QK_ROPE = 64
V_HEAD = 128
QK_DIM = QK_NOPE + QK_ROPE
ROPE_BASE = 10000.0
GRID_W = 64
ML_DQK = 128
ML_CHUNK = 128
WIDTH = 1024
RG_BLOCKS = 8
RG_BW = WIDTH // RG_BLOCKS
RG_CONV = 4
RG_C = 8.0
N_MOD = 6
EPS = 1e-6

LANE = 128
SUBLANE = 8
VT_ROWS = V_HEAD + 16
ML_ROWS = ML_DQK + 16

Z_Q = 0
Z_KV = Z_Q + Q_LORA
Z_KR = Z_KV + KV_LORA
Z_KRP = Z_KR + QK_ROPE
Z_MLA_END = Z_KRP + QK_ROPE
Z_G = Z_MLA_END
Z_MQ = Z_G + LANE
Z_MK = Z_MQ + WIDTH
Z_MV = Z_MK + WIDTH
Z_MO = Z_MV + WIDTH
Z_RX = Z_MO + WIDTH
Z_RG = Z_RX + WIDTH
Z_MERGE = Z_RG + WIDTH

Q_SCALE = (QK_DIM ** -0.5) * math.log2(math.e)
VMEM_LIMIT = 56 * 1024 * 1024


def _cparams(sem, vmem=None):
    return pltpu.CompilerParams(dimension_semantics=sem, vmem_limit_bytes=vmem)


def _pick(n, cands):
    for c in cands:
        if n % c == 0:
            return c
    raise ValueError(f"no tile in {cands} divides {n}")


def _pcall(kernel, *, prev, in_specs, args, **kw):
    if prev is None:
        return pl.pallas_call(kernel, in_specs=in_specs, **kw)(*args)
    prevs = list(prev) if isinstance(prev, (tuple, list)) else [prev]
    n_in, n_prev = len(args), len(prevs)

    def aliased(*refs):
        kernel(*refs[:n_in], *refs[n_in + n_prev:])

    return pl.pallas_call(aliased, in_specs=in_specs + [pl.BlockSpec(memory_space=pl.ANY)] * n_prev,
                          input_output_aliases={n_in + k: k for k in range(n_prev)}, **kw)(*args, *prevs)


def _rms(x, g):
    return x * lax.rsqrt(jnp.mean(x * x, axis=-1, keepdims=True) + EPS) * g


def _sigmoid(x):
    return 0.5 * jnp.tanh(0.5 * x) + 0.5


def _softplus(x):
    return jnp.maximum(x, 0.0) + jnp.log1p(jnp.exp(-jnp.abs(x)))


def _log_sigmoid(x):
    return jnp.minimum(x, 0.0) - jnp.log1p(jnp.exp(-jnp.abs(x)))


def _gelu_tanh(x):
    return 0.5 * x * (1.0 + jnp.tanh(math.sqrt(2.0 / math.pi) * (x + 0.044715 * (x * x * x))))


def _mod_kernel(c_ref, w_ref, b_ref, o_ref):
    c = c_ref[...]
    sc = c * jax.nn.sigmoid(c)
    o_ref[...] = jnp.dot(sc, w_ref[...], preferred_element_type=F32,
                         precision=lax.Precision.HIGHEST) + b_ref[...]


def _modulation(cc, w_mod, b_mod):
    rows, d = cc.shape
    n = w_mod.shape[1]
    tn = _pick(n, (1024, 512, 256, 128))
    return pl.pallas_call(
        _mod_kernel,
        grid=(n // tn,),
        in_specs=[pl.BlockSpec((rows, d), lambda j: (0, 0)),
                  pl.BlockSpec((d, tn), lambda j: (0, j)),
                  pl.BlockSpec((1, tn), lambda j: (0, j))],
        out_specs=pl.BlockSpec((rows, tn), lambda j: (0, j)),
        out_shape=jax.ShapeDtypeStruct((rows, n), F32),
        compiler_params=_cparams(("parallel",), VMEM_LIMIT),
        name="adaln_mod",
    )(cc, w_mod, b_mod.reshape(1, n))


def _inproj_kernel(x_ref, mod_ref, g_ref, w_ref, z_ref, zm_ref, xn_ref, *, d, nz):
    @pl.when(pl.program_id(1) == 0)
    def _():
        y = _rms(x_ref[...], g_ref[...])
        shift = mod_ref[0, :, 0:d]
        scale = mod_ref[0, :, d:2 * d]
        xn_ref[...] = (y * (1.0 + scale) + shift).astype(BF16)

    @pl.when(pl.program_id(1) < nz)
    def _():
        z_ref[...] = jnp.dot(xn_ref[...], w_ref[...], preferred_element_type=F32)

    @pl.when(pl.program_id(1) >= nz)
    def _():
        zm_ref[...] = _sigmoid(jnp.dot(xn_ref[...], w_ref[...], preferred_element_type=F32)).astype(BF16)


def _in_projection(r, src, mods3, g, w_in_p, mod_row, span, prev):
    tm, blk0, nblk = span
    xs, src0 = src
    d = xs.shape[1]
    n = w_in_p.shape[1]
    tn = _pick(math.gcd(Z_MERGE, n - Z_MERGE), (1024, 512, 256, 128))
    nz = Z_MERGE // tn
    return _pcall(
        functools.partial(_inproj_kernel, d=d, nz=nz),
        prev=prev,
        grid=(nblk, n // tn),
        in_specs=[pl.BlockSpec((tm, d), lambda i, j: (src0 + i, 0)),
                  pl.BlockSpec((1, 1, N_MOD * d), lambda i, j: (mod_row((blk0 + i) * tm), 0, 0)),
                  pl.BlockSpec((1, d), lambda i, j: (0, 0)),
                  pl.BlockSpec((d, tn), lambda i, j: (0, j))],
        args=[xs, mods3, g.reshape(1, d), w_in_p],
        out_specs=[pl.BlockSpec((tm, tn), lambda i, j: (blk0 + i, jnp.minimum(j, nz - 1))),
                   pl.BlockSpec((tm, tn), lambda i, j: (blk0 + i, jnp.maximum(j - nz, 0)))],
        out_shape=[jax.ShapeDtypeStruct((r, Z_MERGE), F32), jax.ShapeDtypeStruct((r, n - Z_MERGE), BF16)],
        scratch_shapes=[pltpu.VMEM((tm, d), BF16)],
        compiler_params=_cparams(("parallel", "arbitrary"), VMEM_LIMIT),
        name="in_proj",
    )


def _mla_prep_kernel(z_ref, cos_ref, sin_ref, cost_ref, sint_ref, gq_ref, gkv_ref,
                     wqt_ref, wqrt_ref, wk_ref, wvt_ref, qt_ref, k_ref, vt_ref):
    tm = z_ref.shape[0]
    qn = _rms(z_ref[:, Z_Q:Z_Q + Q_LORA], gq_ref[...]).astype(BF16)
    kvn = _rms(z_ref[:, Z_KV:Z_KV + KV_LORA], gkv_ref[...]).astype(BF16)
    kr = (z_ref[:, Z_KR:Z_KR + QK_ROPE] * cos_ref[...]
          + z_ref[:, Z_KRP:Z_KRP + QK_ROPE] * sin_ref[...]).astype(BF16)
    kn = jnp.dot(kvn, wk_ref[...], preferred_element_type=F32)
    nt = (((1,), (1,)), ((), ()))
    ones = jnp.ones((VT_ROWS - V_HEAD, tm), BF16)
    q_all = lax.dot_general(wqt_ref[...], qn, nt, preferred_element_type=F32)
    qr_all = lax.dot_general(wqrt_ref[...], qn, nt, preferred_element_type=F32)
    v_all = lax.dot_general(wvt_ref[...], kvn, nt, preferred_element_type=F32)
    for h in range(HEADS):
        q_t = q_all[h * QK_DIM:(h + 1) * QK_DIM]
        q_rot = q_t[QK_NOPE:QK_DIM] * cost_ref[...] + qr_all[h * QK_ROPE:(h + 1) * QK_ROPE] * sint_ref[...]
        qt_ref[0, h, 0:QK_NOPE, :] = (q_t[0:QK_NOPE] * Q_SCALE).astype(BF16)
        qt_ref[0, h, QK_NOPE:QK_DIM, :] = (q_rot * Q_SCALE).astype(BF16)
        k_ref[0, h, :, 0:QK_NOPE] = kn[:, h * QK_NOPE:(h + 1) * QK_NOPE].astype(BF16)
        k_ref[0, h, :, QK_NOPE:QK_DIM] = kr
        vt_ref[0, h, 0:V_HEAD, :] = v_all[h * V_HEAD:(h + 1) * V_HEAD].astype(BF16)
        vt_ref[0, h, V_HEAD:VT_ROWS, :] = ones


def _mla_prep(z, tabs, gq, gkv, wqt, wqrt, wk, wvt, b, s, lc, tm):
    r = z.shape[0]
    cos_u, sin_u, cost_u, sint_u = tabs
    nxt, nct = s // tm, lc // tm
    ltot = s + lc

    def bidx(i):
        j = i - b * nxt
        return jnp.where(i < b * nxt, i // nxt, j // nct)

    def sblk(i):
        j = i - b * nxt
        return jnp.where(i < b * nxt, i % nxt, nxt + j % nct)

    full = lambda shape: pl.BlockSpec(shape, lambda i: (0,) * len(shape))
    return pl.pallas_call(
        _mla_prep_kernel,
        grid=(r // tm,),
        in_specs=[pl.BlockSpec((tm, Z_MLA_END), lambda i: (i, 0)),
                  pl.BlockSpec((tm, QK_ROPE), lambda i: (i, 0)),
                  pl.BlockSpec((tm, QK_ROPE), lambda i: (i, 0)),
                  pl.BlockSpec((QK_ROPE, tm), lambda i: (0, i)),
                  pl.BlockSpec((QK_ROPE, tm), lambda i: (0, i)),
                  full((1, Q_LORA)), full((1, KV_LORA)),
                  full(wqt.shape), full(wqrt.shape), full(wk.shape), full(wvt.shape)],
        out_specs=[pl.BlockSpec((1, HEADS, QK_DIM, tm), lambda i: (bidx(i), 0, 0, sblk(i))),
                   pl.BlockSpec((1, HEADS, tm, QK_DIM), lambda i: (bidx(i), 0, sblk(i), 0)),
                   pl.BlockSpec((1, HEADS, VT_ROWS, tm), lambda i: (bidx(i), 0, 0, sblk(i)))],
        out_shape=[jax.ShapeDtypeStruct((b, HEADS, QK_DIM, ltot), BF16),
                   jax.ShapeDtypeStruct((b, HEADS, ltot, QK_DIM), BF16),
                   jax.ShapeDtypeStruct((b, HEADS, VT_ROWS, ltot), BF16)],
        compiler_params=_cparams(("parallel",), VMEM_LIMIT),
        name="mla_prep",
    )(z, cos_u, sin_u, cost_u, sint_u, gq.reshape(1, -1), gkv.reshape(1, -1), wqt, wqrt, wk, wvt)


def _attn_kernel(qt_ref, k_ref, vt_ref, o_ref, s_ref, *, tk, nk):
    qt = qt_ref[0, 0]
    tq = qt.shape[1]

    def scores(c, slot):
        off = pl.multiple_of(c * tk, tk)
        s = jnp.dot(k_ref[0, 0, pl.ds(off, tk), :], qt, preferred_element_type=F32)
        s_ref[slot] = s
        return jnp.max(s, axis=0, keepdims=True)

    def update(c, slot, m, acc, cmax):
        off = pl.multiple_of(c * tk, tk)
        m_new = jnp.maximum(m, cmax)
        p = jnp.exp2(s_ref[slot] - m_new).astype(BF16)
        alpha = jnp.exp2(m - m_new)
        pv = jnp.dot(vt_ref[0, 0, :, pl.ds(off, tk)], p, preferred_element_type=F32)
        return m_new, alpha * acc + pv

    def body(i, carry):
        m, acc, cmax0 = carry
        c = 2 * i
        cmax1 = scores(c + 1, 1)
        m, acc = update(c, 0, m, acc, cmax0)
        cmax0 = scores(c + 2, 0)
        m, acc = update(c + 1, 1, m, acc, cmax1)
        return m, acc, cmax0

    m0 = jnp.full((1, tq), -jnp.inf, F32)
    acc0 = jnp.zeros((VT_ROWS, tq), F32)
    m, acc, cmax0 = lax.fori_loop(0, (nk - 1) // 2, body, (m0, acc0, scores(0, 0)))
    if nk % 2 == 1:
        _, acc = update(nk - 1, 0, m, acc, cmax0)
    else:
        cmax1 = scores(nk - 1, 1)
        m, acc = update(nk - 2, 0, m, acc, cmax0)
        _, acc = update(nk - 1, 1, m, acc, cmax1)
    o = acc[0:V_HEAD] / acc[V_HEAD:V_HEAD + 1]
    o_ref[...] = o.T.astype(o_ref.dtype)


def _attn_call(qt, k, vt, *, b, nq, tq, q_blk0, kv_len, kv_blk0, out_rows, row_blk, prev):
    tk = _pick(kv_len, (1280, 1024, 512, 256, 128))
    nk = kv_len // tk
    return _pcall(
        functools.partial(_attn_kernel, tk=tk, nk=nk),
        prev=prev,
        grid=(b, HEADS, nq),
        in_specs=[pl.BlockSpec((1, 1, QK_DIM, tq), lambda bb, h, i: (bb, h, 0, q_blk0 + i)),
                  pl.BlockSpec((1, 1, kv_len, QK_DIM), lambda bb, h, i: (bb, h, kv_blk0, 0)),
                  pl.BlockSpec((1, 1, VT_ROWS, kv_len), lambda bb, h, i: (bb, h, 0, kv_blk0))],
        args=[qt, k, vt],
        out_specs=pl.BlockSpec((tq, V_HEAD), lambda bb, h, i: (row_blk(bb, i), h)),
        out_shape=jax.ShapeDtypeStruct((out_rows, WIDTH), BF16),
        scratch_shapes=[pltpu.VMEM((2, tk, tq), F32)],
        compiler_params=_cparams(("parallel", "parallel", "arbitrary"), VMEM_LIMIT),
        name="mla_attn",
    )


def _mlstm_gates_kernel(z_ref, b_ref, gc_ref, gr_ref):
    for c in range(gr_ref.shape[0]):
        rows = slice(c * ML_CHUNK, (c + 1) * ML_CHUNK)
        gc = _mlstm_gates_chunk(z_ref[rows, :] + b_ref[...])
        gc_ref[rows, :] = gc
        gr_ref[c] = gc.T


def _mlstm_gates_chunk(g):
    n = g.shape[0]
    lane = lax.broadcasted_iota(jnp.int32, g.shape, 1)
    row = lax.broadcasted_iota(jnp.int32, g.shape, 0)
    lf = _log_sigmoid(g)
    pre = lf
    suf = lf
    k = 1
    while k < n:
        pre = pre + jnp.where(row >= k, pltpu.roll(pre, k, 0), 0.0)
        suf = suf + jnp.where(row < n - k, pltpu.roll(suf, n - k, 0), 0.0)
        k *= 2
    bwd = lane >= 2 * HEADS
    cum = jnp.where(bwd, suf, pre)
    r = g - pltpu.roll(cum, LANE - HEADS, 1)
    pmax = r
    smax = r
    k = 1
    while k < n:
        pmax = jnp.maximum(pmax, jnp.where(row >= k, pltpu.roll(pmax, k, 0), -jnp.inf))
        smax = jnp.maximum(smax, jnp.where(row < n - k, pltpu.roll(smax, n - k, 0), -jnp.inf))
        k *= 2
    cmax = jnp.where(bwd, smax, pmax)
    is_f = ((lane >= HEADS) & (lane < 2 * HEADS)) | ((lane >= 3 * HEADS) & (lane < 4 * HEADS))
    gc = jnp.where(is_f, cum, g)
    gc = jnp.where((lane >= 4 * HEADS) & (lane < 8 * HEADS), pltpu.roll(cmax, 4 * HEADS, 1), gc)
    return jnp.where((lane >= 8 * HEADS) & (lane < 12 * HEADS), pltpu.roll(r, 8 * HEADS, 1), gc)


def _mlstm_gates(z, bias_pad):
    r = z.shape[0]
    nchunk = r // ML_CHUNK
    per = _pick(nchunk, (4, 2, 1))
    return pl.pallas_call(
        _mlstm_gates_kernel,
        grid=(nchunk // per,),
        in_specs=[pl.BlockSpec((per * ML_CHUNK, LANE), lambda i: (i, Z_G // LANE)),
                  pl.BlockSpec((1, LANE), lambda i: (0, 0))],
        out_specs=[pl.BlockSpec((per * ML_CHUNK, LANE), lambda i: (i, 0)),
                   pl.BlockSpec((per, LANE, ML_CHUNK), lambda i: (i, 0, 0))],
        out_shape=[jax.ShapeDtypeStruct((r, LANE), F32),
                   jax.ShapeDtypeStruct((nchunk, LANE, ML_CHUNK), F32)],
        compiler_params=_cparams(("parallel",)),
        name="mlstm_gates",
    )(z, bias_pad)


def _mlstm_kernel(qf_ref, kf_ref, vf_ref, gcf_ref, grf_ref,
                  qb_ref, kb_ref, vb_ref, gcb_ref, grb_ref,
                  hf_ref, hb_ref, c_ref, m_ref):
    L = ML_CHUNK

    @pl.when(pl.program_id(1) == 0)
    def _():
        c_ref[...] = jnp.zeros(c_ref.shape, F32)
        m_ref[...] = jnp.zeros(m_ref.shape, F32)

    row = lax.broadcasted_iota(jnp.int32, (L, L), 0)
    col = lax.broadcasted_iota(jnp.int32, (L, L), 1)
    nt = (((1,), (1,)), ((), ()))
    ones_rows = jnp.ones((ML_ROWS - ML_DQK, L), BF16)
    dirs = ((qf_ref, kf_ref, vf_ref, gcf_ref, grf_ref, hf_ref, row <= col),
            (qb_ref, kb_ref, vb_ref, gcb_ref, grb_ref, hb_ref, row >= col))
    ln_ks = math.log(ML_DQK ** -0.5)
    st = []
    for d, (q_ref, k_ref, v_ref, gc_ref, gr_ref, o_ref, mask) in enumerate(dirs):
        gc = gc_ref[...]
        gr = gr_ref[0]
        for h in range(HEADS):
            li = 2 * HEADS * d + h
            lb = li + HEADS
            lm = li + 4 * HEADS
            lr = li + 8 * HEADS
            brow = gr[lb:lb + 1, :]
            g_tot = brow[:, L - 1:L] if d == 0 else brow[:, 0:1]
            wend = g_tot - brow + gr[li:li + 1, :]
            a = jnp.max(wend, axis=1, keepdims=True)
            m_old = m_ref[d * HEADS + h][0:1, 0:1]
            urow = jnp.maximum(m_old, gr[lm:lm + 1, :])
            m_new = jnp.maximum(g_tot + m_old, a)
            st.append(dict(
                idx=d * HEADS + h, sl=slice(h * ML_DQK, (h + 1) * ML_DQK),
                q_ref=q_ref, k_ref=k_ref, v_ref=v_ref, o_ref=o_ref, mask=mask,
                e=jnp.exp(wend - a + ln_ks), urow=urow, rcol=gc[:, lr:lr + 1],
                di=jnp.exp(m_old - urow), em=jnp.exp(-(brow + urow)), m_new=m_new,
                dec=jnp.exp(g_tot + m_old - m_new), inp=jnp.exp(a - m_new)))
    for t in st:
        sl = t["sl"]
        q = t["q_ref"][:, sl].astype(BF16)
        k = t["k_ref"][:, sl].astype(BF16)
        v_t = t["v_ref"][:, sl].T
        t["vext"] = jnp.concatenate([v_t.astype(BF16), ones_rows], axis=0)
        vext_e = jnp.concatenate([(v_t * t["e"]).astype(BF16), jnp.broadcast_to(t["e"], ones_rows.shape).astype(BF16)],
                                 axis=0)
        t["sT"] = lax.dot_general(k, q, nt, preferred_element_type=F32)
        t["d_c"] = jnp.dot(vext_e, k, preferred_element_type=F32)
        t["qc"] = lax.dot_general(c_ref[t["idx"]].astype(BF16), q, nt, preferred_element_type=F32)
    for t in st:
        dexp = jnp.exp(jnp.where(t["mask"], t["rcol"] - t["urow"] + ln_ks, -jnp.inf))
        t["p"] = (t["sT"] * dexp).astype(BF16)
    for t in st:
        res = t["di"] * t["qc"] + jnp.dot(t["vext"], t["p"], preferred_element_type=F32)
        den = res[ML_DQK:ML_DQK + 1, :]
        h_t = res[0:ML_DQK, :] / jnp.maximum(jnp.abs(den), t["em"])
        t["o_ref"][:, t["sl"]] = h_t.T
        c_ref[t["idx"]] = t["dec"] * c_ref[t["idx"]] + t["inp"] * t["d_c"]
        m_ref[t["idx"]] = jnp.broadcast_to(t["m_new"], (SUBLANE, LANE))


def _mlstm(z, gc, gr, b, s, lc):
    r = z.shape[0]
    L = ML_CHUNK
    nxc, ncc = s // L, lc // L
    x0 = lambda bb: bb * nxc
    c0 = lambda bb: b * nxc + bb * ncc

    def blk_f(bb, st):
        return jnp.where(st < ncc, c0(bb) + st, x0(bb) + st - ncc)

    def blk_b(bb, st):
        return jnp.where(st < ncc, c0(bb) + ncc - 1 - st, x0(bb) + nxc - 1 - (st - ncc))

    def zspec(blk, cb):
        return pl.BlockSpec((L, WIDTH), lambda bb, st: (blk(bb, st), cb))

    def dir_specs(blk):
        return [zspec(blk, Z_MQ // WIDTH), zspec(blk, Z_MK // WIDTH), zspec(blk, Z_MV // WIDTH),
                pl.BlockSpec((L, LANE), lambda bb, st: (blk(bb, st), 0)),
                pl.BlockSpec((1, LANE, L), lambda bb, st: (blk(bb, st), 0, 0))]

    return pl.pallas_call(
        _mlstm_kernel,
        grid=(b, nxc + ncc),
        in_specs=dir_specs(blk_f) + dir_specs(blk_b),
        out_specs=[pl.BlockSpec((L, WIDTH), lambda bb, st: (blk_f(bb, st), 0)),
                   pl.BlockSpec((L, WIDTH), lambda bb, st: (blk_b(bb, st), 0))],
        out_shape=[jax.ShapeDtypeStruct((r, WIDTH), F32), jax.ShapeDtypeStruct((r, WIDTH), F32)],
        scratch_shapes=[pltpu.VMEM((2 * HEADS, ML_ROWS, ML_DQK), F32),
                        pltpu.VMEM((2 * HEADS, SUBLANE, LANE), F32)],
        compiler_params=_cparams(("arbitrary", "arbitrary"), VMEM_LIMIT),
        name="mlstm",
    )(z, z, z, gc, gr, z, z, z, gc, gr)


def _rg_ab_kernel(cur_ref, prev_ref, next_ref, cw_ref, cb_ref, w_ref, ba_ref, bx_ref, lam_ref,
                  af_ref, bf_ref, ab_ref, bb_ref, xe_ref, *, tm, s_len, c_len, rows_x):
    row0 = pl.program_id(0) * tm
    in_x = row0 < rows_x
    seq = jnp.where(in_x, s_len, c_len)
    off = jnp.where(in_x, row0, row0 - rows_x)
    first = lax.rem(off, seq) == 0
    last = lax.rem(off + tm, seq) == 0
    xe_ref[0:SUBLANE, :] = jnp.where(first, 0.0, prev_ref[...])
    xe_ref[SUBLANE:SUBLANE + tm, :] = cur_ref[...]
    xe_ref[SUBLANE + tm:2 * SUBLANE + tm, :] = jnp.where(last, 0.0, next_ref[...])
    lp = RG_CONV // 2
    xc = cb_ref[...] + cw_ref[0:1, :] * xe_ref[pl.ds(SUBLANE - lp, tm), :]
    for j in range(1, RG_CONV):
        xc = xc + cw_ref[j:j + 1, :] * xe_ref[pl.ds(SUBLANE - lp + j, tm), :]
    outs = ((af_ref, bf_ref), (ab_ref, bb_ref))
    for g in range(RG_BLOCKS):
        sl = slice(g * RG_BW, (g + 1) * RG_BW)
        xg = xc[:, sl]
        o = jnp.dot(xg.astype(BF16), w_ref[g], preferred_element_type=F32)
        for d in range(2):
            r = _sigmoid(o[:, (2 * d) * RG_BW:(2 * d + 1) * RG_BW] + ba_ref[d:d + 1, sl])
            i = _sigmoid(o[:, (2 * d + 1) * RG_BW:(2 * d + 2) * RG_BW] + bx_ref[d:d + 1, sl])
            log_a = (-RG_C) * r * _softplus(-lam_ref[d:d + 1, sl])
            a = jnp.exp(log_a)
            one_m_a2 = -jnp.tanh(log_a) * (a * a + 1.0)
            outs[d][0][:, sl] = a
            outs[d][1][:, sl] = jnp.sqrt(one_m_a2) * (i * xg)


def _rg_coeffs(z, cw, cb, w_rg, ba, bx, lam, b, s, lc, tm):
    r = z.shape[0]
    per = tm // SUBLANE
    nblk8 = r // SUBLANE
    cbk = Z_RX // WIDTH
    full = lambda shape: pl.BlockSpec(shape, lambda i: (0,) * len(shape))
    kernel = functools.partial(_rg_ab_kernel, tm=tm, s_len=s, c_len=lc, rows_x=b * s)
    o_spec = pl.BlockSpec((tm, WIDTH), lambda i: (i, 0))
    o_shape = jax.ShapeDtypeStruct((r, WIDTH), F32)
    return pl.pallas_call(
        kernel,
        grid=(r // tm,),
        in_specs=[pl.BlockSpec((tm, WIDTH), lambda i: (i, cbk)),
                  pl.BlockSpec((SUBLANE, WIDTH), lambda i: (jnp.maximum(i * per - 1, 0), cbk)),
                  pl.BlockSpec((SUBLANE, WIDTH), lambda i: (jnp.minimum((i + 1) * per, nblk8 - 1), cbk)),
                  full((RG_CONV, WIDTH)), full((1, WIDTH)), full(w_rg.shape),
                  full((2, WIDTH)), full((2, WIDTH)), full((2, WIDTH))],
        out_specs=[o_spec] * 4,
        out_shape=[o_shape] * 4,
        scratch_shapes=[pltpu.VMEM((tm + 2 * SUBLANE, WIDTH), F32)],
        compiler_params=_cparams(("parallel",), VMEM_LIMIT),
        name="rglru_coeffs",
    )(z, z, z, cw, cb.reshape(1, WIDTH), w_rg, ba, bx, lam)


def _rg_scan_kernel(af_ref, bf_ref, ab_ref, bb_ref, hf_ref, hb_ref, sf_ref, sb_ref, *, tt):
    @pl.when(pl.program_id(1) == 0)
    def _():
        sf_ref[...] = jnp.zeros(sf_ref.shape, F32)
        sb_ref[...] = jnp.zeros(sb_ref.shape, F32)

    def body(blk, carry):
        hf, hb = carry
        t0 = pl.multiple_of(blk * SUBLANE, SUBLANE)
        a8, b8 = af_ref[pl.ds(t0, SUBLANE), :], bf_ref[pl.ds(t0, SUBLANE), :]
        rows = []
        for j in range(SUBLANE):
            hf = a8[j:j + 1, :] * hf + b8[j:j + 1, :]
            rows.append(hf)
        hf_ref[pl.ds(t0, SUBLANE), :] = jnp.concatenate(rows, axis=0)
        tb0 = pl.multiple_of(tt - SUBLANE - blk * SUBLANE, SUBLANE)
        a8, b8 = ab_ref[pl.ds(tb0, SUBLANE), :], bb_ref[pl.ds(tb0, SUBLANE), :]
        rows = [None] * SUBLANE
        for j in reversed(range(SUBLANE)):
            hb = a8[j:j + 1, :] * hb + b8[j:j + 1, :]
            rows[j] = hb
        hb_ref[pl.ds(tb0, SUBLANE), :] = jnp.concatenate(rows, axis=0)
        return hf, hb

    hf, hb = lax.fori_loop(0, tt // SUBLANE, body, (sf_ref[...], sb_ref[...]))
    sf_ref[...] = hf
    sb_ref[...] = hb


def _rg_scan(af, bf, ab, bb, b, s, lc, tt):
    r = af.shape[0]
    nxt, nct = s // tt, lc // tt
    x0 = lambda bi: bi * nxt
    c0 = lambda bi: b * nxt + bi * nct

    def blk_f(bi, st):
        return jnp.where(st < nct, c0(bi) + st, x0(bi) + st - nct)

    def blk_b(bi, st):
        return jnp.where(st < nct, c0(bi) + nct - 1 - st, x0(bi) + nxt - 1 - (st - nct))

    sf = pl.BlockSpec((tt, WIDTH), lambda bi, st: (blk_f(bi, st), 0))
    sb = pl.BlockSpec((tt, WIDTH), lambda bi, st: (blk_b(bi, st), 0))
    o_shape = jax.ShapeDtypeStruct((r, WIDTH), F32)
    return pl.pallas_call(
        functools.partial(_rg_scan_kernel, tt=tt),
        grid=(b, nxt + nct),
        in_specs=[sf, sf, sb, sb],
        out_specs=[sf, sb],
        out_shape=[o_shape, o_shape],
        scratch_shapes=[pltpu.VMEM((1, WIDTH), F32), pltpu.VMEM((1, WIDTH), F32)],
        compiler_params=_cparams(("arbitrary", "arbitrary")),
        name="rglru_scan",
    )(af, bf, ab, bb)


def _merge_kernel(ya_ref, mhf_ref, mhb_ref, zo_ref, ng_ref, rhf_ref, rhb_ref, zg_ref,
                  gm0_ref, gm1_ref, gm2_ref, w0_ref, w1_ref, w2_ref, m_ref, yb_ref, yr_ref):
    @pl.when(pl.program_id(1) == 0)
    def _():
        hsum = mhf_ref[...] + mhb_ref[...]
        for h in range(HEADS):
            sl = slice(h * V_HEAD, (h + 1) * V_HEAD)
            hn = _rms(hsum[:, sl], ng_ref[:, sl])
            yb_ref[:, sl] = (hn * _sigmoid(zo_ref[:, sl])).astype(BF16)
        yr_ref[...] = ((rhf_ref[...] + rhb_ref[...]) * _gelu_tanh(zg_ref[...])).astype(BF16)

    m = (gm0_ref[...].astype(F32) * jnp.dot(ya_ref[...], w0_ref[0], preferred_element_type=F32)
         + gm1_ref[...].astype(F32) * jnp.dot(yb_ref[...], w1_ref[0], preferred_element_type=F32)
         + gm2_ref[...].astype(F32) * jnp.dot(yr_ref[...], w2_ref[0], preferred_element_type=F32))
    m_ref[...] = m.astype(BF16)


def _merge(rows, d, ya, mhf, mhb, z, gm, ng, rhf, rhb, wb, tm):
    tn = _pick(d, (512, 256, 128))
    wide = lambda: pl.BlockSpec((tm, WIDTH), lambda i, j: (i, 0))
    zblk = lambda cb: pl.BlockSpec((tm, WIDTH), lambda i, j: (i, cb))
    zm = lambda br: pl.BlockSpec((tm, tn), lambda i, j: (i, (br * d) // tn + j))
    wspec = lambda br: pl.BlockSpec((1, WIDTH, tn), lambda i, j: (br, 0, j))
    return pl.pallas_call(
        _merge_kernel,
        grid=(rows // tm, d // tn),
        in_specs=[wide(), wide(), wide(), zblk(Z_MO // WIDTH),
                  pl.BlockSpec((1, WIDTH), lambda i, j: (0, 0)),
                  wide(), wide(), zblk(Z_RG // WIDTH),
                  zm(0), zm(1), zm(2), wspec(0), wspec(1), wspec(2)],
        out_specs=pl.BlockSpec((tm, tn), lambda i, j: (i, j)),
        out_shape=jax.ShapeDtypeStruct((rows, d), BF16),
        scratch_shapes=[pltpu.VMEM((tm, WIDTH), BF16), pltpu.VMEM((tm, WIDTH), BF16)],
        compiler_params=_cparams(("parallel", "arbitrary"), VMEM_LIMIT),
        name="branch_merge",
    )(ya, mhf, mhb, z, ng.reshape(1, WIDTH), rhf, rhb, z, gm, gm, gm, wb, wb, wb)


def _outproj_kernel(m_ref, w_ref, x_ref, gate_ref, o_ref):
    o_ref[...] = x_ref[...] + gate_ref[0] * jnp.dot(m_ref[...], w_ref[...], preferred_element_type=F32)


def _out_projection(rows, src, m, w_out, mods3, mod_row, span, prev):
    tm, blk0, nblk = span
    xs, src0 = src
    d = xs.shape[1]
    tn = _pick(d, (1024, 512, 256, 128))
    return _pcall(
        _outproj_kernel,
        prev=prev,
        grid=(nblk, d // tn),
        in_specs=[pl.BlockSpec((tm, d), lambda i, j: (blk0 + i, 0)),
                  pl.BlockSpec((d, tn), lambda i, j: (0, j)),
                  pl.BlockSpec((tm, tn), lambda i, j: (src0 + i, j)),
                  pl.BlockSpec((1, 1, tn), lambda i, j: (mod_row((blk0 + i) * tm), 0, (2 * d) // tn + j))],
        args=[m, w_out, xs, mods3],
        out_specs=pl.BlockSpec((tm, tn), lambda i, j: (blk0 + i, j)),
        out_shape=jax.ShapeDtypeStruct((rows, d), F32),
        compiler_params=_cparams(("parallel", "parallel"), VMEM_LIMIT),
        name="out_proj",
    )


def _mlp_kernel(x_ref, mod_ref, g_ref, w1_ref, w2_ref, fg_ref, o_ref, xn_ref, *, d, final):
    j = pl.program_id(1)

    @pl.when(j == 0)
    def _():
        y = _rms(x_ref[...], g_ref[...])
        shift = mod_ref[0, :, 3 * d:4 * d]
        scale = mod_ref[0, :, 4 * d:5 * d]
        xn_ref[...] = (y * (1.0 + scale) + shift).astype(BF16)
        o_ref[...] = jnp.zeros(o_ref.shape, F32)

    h = jnp.dot(xn_ref[...], w1_ref[...], preferred_element_type=F32)
    h = jnp.square(jnp.maximum(h, 0.0)).astype(BF16)
    o_ref[...] += jnp.dot(h, w2_ref[...], preferred_element_type=F32)

    @pl.when(j == pl.num_programs(1) - 1)
    def _():
        out = x_ref[...] + mod_ref[0, :, 5 * d:6 * d] * o_ref[...]
        if final:
            out = _rms(out, fg_ref[...])
        o_ref[...] = out


def _mlp(rows, x1, mods3, g, w1, w2, fg, mod_row, span, prev, final):
    tm, blk0, nblk = span
    d = x1.shape[1]
    dff = w1.shape[1]
    tf = _pick(dff, (512, 256, 128))
    return _pcall(
        functools.partial(_mlp_kernel, d=d, final=final),
        prev=prev,
        grid=(nblk, dff // tf),
        in_specs=[pl.BlockSpec((tm, d), lambda i, j: (blk0 + i, 0)),
                  pl.BlockSpec((1, 1, N_MOD * d), lambda i, j: (mod_row((blk0 + i) * tm), 0, 0)),
                  pl.BlockSpec((1, d), lambda i, j: (0, 0)),
                  pl.BlockSpec((d, tf), lambda i, j: (0, j)),
                  pl.BlockSpec((tf, d), lambda i, j: (j, 0)),
                  pl.BlockSpec((1, d), lambda i, j: (0, 0))],
        args=[x1, mods3, g.reshape(1, d), w1, w2, fg.reshape(1, d)],
        out_specs=pl.BlockSpec((tm, d), lambda i, j: (blk0 + i, 0)),
        out_shape=jax.ShapeDtypeStruct((rows, d), F32),
        scratch_shapes=[pltpu.VMEM((tm, d), BF16)],
        compiler_params=_cparams(("parallel", "arbitrary"), VMEM_LIMIT),
        name="mlp",
    )


def _rope_perm_cols(w):
    n = QK_ROPE // 4
    return jnp.concatenate([-w[..., n:2 * n], w[..., 0:n], -w[..., 3 * n:4 * n], w[..., 2 * n:3 * n]], axis=-1)


def _layer_weights(l, d, w_in, w_uq, w_ukv, mlstm_gate_b, rg_wa, rg_wx, w_branch, w_out, w_mlp1, w_mlp2):
    wi = w_in[l]
    o = 0
    parts = {}
    for name, wdt in (("mla_q", Q_LORA), ("mla_kv", KV_LORA), ("mla_kr", QK_ROPE), ("ml_q", WIDTH),
                      ("ml_k", WIDTH), ("ml_v", WIDTH), ("ml_o", WIDTH), ("ml_g", 4 * HEADS),
                      ("rg_x", WIDTH), ("rg_gate", WIDTH), ("merge", 3 * d)):
        parts[name] = wi[:, o:o + wdt]
        o += wdt
    g_pad = jnp.pad(parts["ml_g"], ((0, 0), (0, LANE - 4 * HEADS)))
    w_in_p = jnp.concatenate(
        [parts["mla_q"], parts["mla_kv"], parts["mla_kr"], _rope_perm_cols(parts["mla_kr"]), g_pad,
         parts["ml_q"], parts["ml_k"], parts["ml_v"], parts["ml_o"], parts["rg_x"], parts["rg_gate"],
         parts["merge"]], axis=1).astype(BF16)
    wq = w_uq[l].reshape(Q_LORA, HEADS, QK_DIM)
    wqt = jnp.transpose(wq, (1, 2, 0)).reshape(HEADS * QK_DIM, Q_LORA).astype(BF16)
    wqrt = jnp.transpose(_rope_perm_cols(wq[:, :, QK_NOPE:]), (1, 2, 0)).reshape(HEADS * QK_ROPE, Q_LORA).astype(BF16)
    wkv = w_ukv[l].reshape(KV_LORA, HEADS, QK_NOPE + V_HEAD)
    wk = wkv[:, :, :QK_NOPE].reshape(KV_LORA, HEADS * QK_NOPE).astype(BF16)
    wvt = jnp.transpose(wkv[:, :, QK_NOPE:], (1, 2, 0)).reshape(HEADS * V_HEAD, KV_LORA).astype(BF16)
    gate_b = jnp.pad(mlstm_gate_b[l].reshape(1, 4 * HEADS), ((0, 0), (0, LANE - 4 * HEADS)))
    w_rg = jnp.concatenate([rg_wa[l, 0], rg_wx[l, 0], rg_wa[l, 1], rg_wx[l, 1]], axis=-1).astype(BF16)
    return dict(w_in=w_in_p, wqt=wqt, wqrt=wqrt, wk=wk, wvt=wvt, gate_b=gate_b, w_rg=w_rg,
                wb=w_branch[l].astype(BF16), w_out=w_out[l].astype(BF16),
                w1=w_mlp1[l].astype(BF16), w2=w_mlp2[l].astype(BF16))


def _rope_tables(b, s, lc):
    t = jnp.arange(s, dtype=jnp.int32)
    row = (t // GRID_W).astype(F32)
    col = (t % GRID_W).astype(F32)
    n_freq = QK_ROPE // 4
    inv = ROPE_BASE ** (-jnp.arange(n_freq, dtype=F32) / n_freq)
    ang_r = row[:, None] * inv[None, :]
    ang_c = col[:, None] * inv[None, :]
    cos = jnp.concatenate([jnp.cos(ang_r)] * 2 + [jnp.cos(ang_c)] * 2, axis=1)
    sin = jnp.concatenate([jnp.sin(ang_r)] * 2 + [jnp.sin(ang_c)] * 2, axis=1)
    cos_u = jnp.concatenate([jnp.tile(cos, (b, 1)), jnp.ones((b * lc, QK_ROPE), F32)], axis=0)
    sin_u = jnp.concatenate([jnp.tile(sin, (b, 1)), jnp.zeros((b * lc, QK_ROPE), F32)], axis=0)
    return cos_u, sin_u, cos_u.T, sin_u.T


def kernel(x, c, ctx, c_ctx, norm1_g, norm2_g, w_mod, b_mod, w_in, q_norm_g, w_uq, kv_norm_g, w_ukv,
           mlstm_gate_b, mlstm_norm_g, rg_conv_w, rg_conv_b, rg_wa, rg_ba, rg_wx, rg_bx, rg_lam,
           w_branch, w_out, w_mlp1, w_mlp2, final_g):
    b, s, d = x.shape
    lc = ctx.shape[1]
    depth = w_in.shape[0]
    rows_x, rows_c = b * s, b * lc
    r = rows_x + rows_c

    tm_x = _pick(s, (1024, 512, 256, 128))
    tm_c = _pick(math.gcd(rows_x, rows_c), (512, 256, 128))
    tm_mid = _pick(math.gcd(rows_x, rows_c), (512, 256, 128))
    tm_seq = _pick(math.gcd(s, lc), (256, 128))
    tq = _pick(s, (1024, 512, 256, 128))
    span_x = (tm_x, 0, rows_x // tm_x)
    span_c = (tm_c, rows_x // tm_c, rows_c // tm_c)

    def mod_row(row0):
        return jnp.where(row0 < rows_x, 1 + row0 // s, 0)

    def dense(fn, with_ctx):
        out = fn(span_x, src_x, None)
        return fn(span_c, src_c, out) if with_ctx else out

    src_x = (x.reshape(rows_x, d), 0)
    src_c = (ctx.reshape(rows_c, d), 0)
    cc = jnp.concatenate([c_ctx[None, :], c, jnp.zeros((SUBLANE - 1 - b, d), F32)], axis=0)
    tabs = _rope_tables(b, s, lc)

    for l in range(depth):
        last = l == depth - 1
        w = _layer_weights(l, d, w_in, w_uq, w_ukv, mlstm_gate_b, rg_wa, rg_wx, w_branch, w_out,
                           w_mlp1, w_mlp2)
        mods3 = _modulation(cc, w_mod[l], b_mod[l]).reshape(SUBLANE, 1, N_MOD * d)
        z, gm = dense(lambda span, src, prev: _in_projection(r, src, mods3, norm1_g[l], w["w_in"], mod_row,
                                                             span, prev), True)

        qt, kk, vt = _mla_prep(z, tabs, q_norm_g[l], kv_norm_g[l], w["wqt"], w["wqrt"], w["wk"],
                               w["wvt"], b, s, lc, tm_seq)
        ya = _attn_call(qt, kk, vt, b=b, nq=s // tq, tq=tq, q_blk0=0, kv_len=s + lc, kv_blk0=0,
                        out_rows=r, row_blk=lambda bb, i: bb * (s // tq) + i, prev=None)
        if not last:
            ya = _attn_call(qt, kk, vt, b=b, nq=1, tq=lc, q_blk0=s // lc, kv_len=lc, kv_blk0=s // lc,
                            out_rows=r, row_blk=lambda bb, i: rows_x // lc + bb, prev=ya)

        gc, gr = _mlstm_gates(z, w["gate_b"])
        mhf, mhb = _mlstm(z, gc, gr, b, s, lc)

        af, bf, ab, bb_ = _rg_coeffs(z, rg_conv_w[l], rg_conv_b[l], w["w_rg"], rg_ba[l], rg_bx[l],
                                     rg_lam[l], b, s, lc, tm_seq)
        rhf, rhb = _rg_scan(af, bf, ab, bb_, b, s, lc, tm_seq)

        rows = rows_x if last else r
        m = _merge(rows, d, ya, mhf, mhb, z, gm, mlstm_norm_g[l], rhf, rhb, w["wb"], tm_mid)
        x1 = dense(lambda span, src, prev: _out_projection(rows, src, m, w["w_out"], mods3, mod_row, span,
                                                           prev), not last)
        xu = dense(lambda span, src, prev: _mlp(rows, x1, mods3, norm2_g[l], w["w1"], w["w2"], final_g,
                                                mod_row, span, prev, last), not last)
        src_x = (xu, span_x[1])
        src_c = (xu, span_c[1])

    return xu.reshape(b, s, d)
```

```python
import functools
import math

import jax
import jax.numpy as jnp
from jax import lax
from jax.experimental import pallas as pl
from jax.experimental.pallas import tpu as pltpu

F32 = jnp.float32
BF16 = jnp.bfloat16

HEADS = 8
Q_LORA = 512
KV_LORA = 256
QK_NOPE = 128
QK_ROPE = 64
V_HEAD = 128
QK_DIM = QK_NOPE + QK_ROPE
ROPE_BASE = 10000.0
GRID_W = 64
ML_DQK = 128
ML_CHUNK = 128
WIDTH = 1024
RG_BLOCKS = 8
RG_BW = WIDTH // RG_BLOCKS
RG_CONV = 4
RG_C = 8.0
N_MOD = 6
EPS = 1e-6

LANE = 128
SUBLANE = 8
VT_ROWS = V_HEAD + 16
ML_ROWS = ML_DQK + 16

Z_Q = 0
Z_KV = Z_Q + Q_LORA
Z_KR = Z_KV + KV_LORA
Z_KRP = Z_KR + QK_ROPE
Z_MLA_END = Z_KRP + QK_ROPE
Z_G = Z_MLA_END
Z_MQ = Z_G + LANE
Z_MK = Z_MQ + WIDTH
Z_MV = Z_MK + WIDTH
Z_MO = Z_MV + WIDTH
Z_RX = Z_MO + WIDTH
Z_RG = Z_RX + WIDTH
Z_MERGE = Z_RG + WIDTH

Q_SCALE = (QK_DIM ** -0.5) * math.log2(math.e)
VMEM_LIMIT = 56 * 1024 * 1024


def _cparams(sem, vmem=None):
    return pltpu.CompilerParams(dimension_semantics=sem, vmem_limit_bytes=vmem)


def _pick(n, cands):
    for c in cands:
        if n % c == 0:
            return c
    raise ValueError(f"no tile in {cands} divides {n}")


def _pcall(kernel, *, prev, in_specs, args, **kw):
    if prev is None:
        return pl.pallas_call(kernel, in_specs=in_specs, **kw)(*args)
    prevs = list(prev) if isinstance(prev, (tuple, list)) else [prev]
    n_in, n_prev = len(args), len(prevs)

    def aliased(*refs):
        kernel(*refs[:n_in], *refs[n_in + n_prev:])

    return pl.pallas_call(aliased, in_specs=in_specs + [pl.BlockSpec(memory_space=pl.ANY)] * n_prev,
                          input_output_aliases={n_in + k: k for k in range(n_prev)}, **kw)(*args, *prevs)


def _rms(x, g):
    return x * lax.rsqrt(jnp.mean(x * x, axis=-1, keepdims=True) + EPS) * g


def _sigmoid(x):
    return 0.5 * jnp.tanh(0.5 * x) + 0.5


def _softplus(x):
    return jnp.maximum(x, 0.0) + jnp.log1p(jnp.exp(-jnp.abs(x)))


def _log_sigmoid(x):
    return jnp.minimum(x, 0.0) - jnp.log1p(jnp.exp(-jnp.abs(x)))


def _gelu_tanh(x):
    return 0.5 * x * (1.0 + jnp.tanh(math.sqrt(2.0 / math.pi) * (x + 0.044715 * (x * x * x))))


def _mod_kernel(c_ref, w_ref, b_ref, o_ref):
    c = c_ref[...]
    sc = c * jax.nn.sigmoid(c)
    o_ref[...] = jnp.dot(sc, w_ref[0], preferred_element_type=F32,
                         precision=lax.Precision.HIGHEST) + b_ref[0]


def _modulation(cc, w_mod, b_mod, l):
    rows, d = cc.shape
    depth, _, n = w_mod.shape
    tn = _pick(n, (1024, 512, 256, 128))
    return pl.pallas_call(
        _mod_kernel,
        grid=(n // tn,),
        in_specs=[pl.BlockSpec((rows, d), lambda j: (0, 0)),
                  pl.BlockSpec((1, d, tn), lambda j: (l, 0, j)),
                  pl.BlockSpec((1, 1, tn), lambda j: (l, 0, j))],
        out_specs=pl.BlockSpec((rows, tn), lambda j: (0, j)),
        out_shape=jax.ShapeDtypeStruct((rows, n), F32),
        compiler_params=_cparams(("parallel",), VMEM_LIMIT),
        name="adaln_mod",
    )(cc, w_mod, b_mod.reshape(depth, 1, n))


def _inproj_kernel(x_ref, mod_ref, g_ref, w_ref, z_ref, zm_ref, xn_ref, *, d, nz):
    @pl.when(pl.program_id(1) == 0)
    def _():
        y = _rms(x_ref[...], g_ref[...])
        shift = mod_ref[0, :, 0:d]
        scale = mod_ref[0, :, d:2 * d]
        xn_ref[...] = (y * (1.0 + scale) + shift).astype(BF16)

    @pl.when(pl.program_id(1) < nz)
    def _():
        z_ref[...] = jnp.dot(xn_ref[...], w_ref[...], preferred_element_type=F32)

    @pl.when(pl.program_id(1) >= nz)
    def _():
        zm_ref[...] = _sigmoid(jnp.dot(xn_ref[...], w_ref[...], preferred_element_type=F32)).astype(BF16)


def _in_projection(r, src, mods3, g, w_in_p, mod_row, span, prev):
    tm, blk0, nblk = span
    xs, src0 = src
    d = xs.shape[1]
    n = w_in_p.shape[1]
    tn = _pick(math.gcd(Z_MERGE, n - Z_MERGE), (1024, 512, 256, 128))
    nz = Z_MERGE // tn
    return _pcall(
        functools.partial(_inproj_kernel, d=d, nz=nz),
        prev=prev,
        grid=(nblk, n // tn),
        in_specs=[pl.BlockSpec((tm, d), lambda i, j: (src0 + i, 0)),
                  pl.BlockSpec((1, 1, N_MOD * d), lambda i, j: (mod_row((blk0 + i) * tm), 0, 0)),
                  pl.BlockSpec((1, d), lambda i, j: (0, 0)),
                  pl.BlockSpec((d, tn), lambda i, j: (0, j))],
        args=[xs, mods3, g.reshape(1, d), w_in_p],
        out_specs=[pl.BlockSpec((tm, tn), lambda i, j: (blk0 + i, jnp.minimum(j, nz - 1))),
                   pl.BlockSpec((tm, tn), lambda i, j: (blk0 + i, jnp.maximum(j - nz, 0)))],
        out_shape=[jax.ShapeDtypeStruct((r, Z_MERGE), F32), jax.ShapeDtypeStruct((r, n - Z_MERGE), BF16)],
        scratch_shapes=[pltpu.VMEM((tm, d), BF16)],
        compiler_params=_cparams(("parallel", "arbitrary"), VMEM_LIMIT),
        name="in_proj",
    )


def _mla_prep_kernel(z_ref, cos_ref, sin_ref, cost_ref, sint_ref, gq_ref, gkv_ref,
                     wqt_ref, wqrt_ref, wk_ref, wvt_ref, qt_ref, k_ref, vt_ref):
    tm = z_ref.shape[0]
    qn = _rms(z_ref[:, Z_Q:Z_Q + Q_LORA], gq_ref[...]).astype(BF16)
    kvn = _rms(z_ref[:, Z_KV:Z_KV + KV_LORA], gkv_ref[...]).astype(BF16)
    kr = (z_ref[:, Z_KR:Z_KR + QK_ROPE] * cos_ref[...]
          + z_ref[:, Z_KRP:Z_KRP + QK_ROPE] * sin_ref[...]).astype(BF16)
    kn = jnp.dot(kvn, wk_ref[...], preferred_element_type=F32)
    nt = (((1,), (1,)), ((), ()))
    ones = jnp.ones((VT_ROWS - V_HEAD, tm), BF16)
    q_all = lax.dot_general(wqt_ref[...], qn, nt, preferred_element_type=F32)
    qr_all = lax.dot_general(wqrt_ref[...], qn, nt, preferred_element_type=F32)
    v_all = lax.dot_general(wvt_ref[...], kvn, nt, preferred_element_type=F32)
    for h in range(HEADS):
        q_t = q_all[h * QK_DIM:(h + 1) * QK_DIM]
        q_rot = q_t[QK_NOPE:QK_DIM] * cost_ref[...] + qr_all[h * QK_ROPE:(h + 1) * QK_ROPE] * sint_ref[...]
        qt_ref[0, h, 0:QK_NOPE, :] = (q_t[0:QK_NOPE] * Q_SCALE).astype(BF16)
        qt_ref[0, h, QK_NOPE:QK_DIM, :] = (q_rot * Q_SCALE).astype(BF16)
        k_ref[0, h, :, 0:QK_NOPE] = kn[:, h * QK_NOPE:(h + 1) * QK_NOPE].astype(BF16)
        k_ref[0, h, :, QK_NOPE:QK_DIM] = kr
        vt_ref[0, h, 0:V_HEAD, :] = v_all[h * V_HEAD:(h + 1) * V_HEAD].astype(BF16)
        vt_ref[0, h, V_HEAD:VT_ROWS, :] = ones


def _mla_prep(z, tabs, gq, gkv, wqt, wqrt, wk, wvt, b, s, lc, tm):
    r = z.shape[0]
    cos_u, sin_u, cost_u, sint_u = tabs
    nxt, nct = s // tm, lc // tm
    ltot = s + lc

    def bidx(i):
        j = i - b * nxt
        return jnp.where(i < b * nxt, i // nxt, j // nct)

    def sblk(i):
        j = i - b * nxt
        return jnp.where(i < b * nxt, i % nxt, nxt + j % nct)

    full = lambda shape: pl.BlockSpec(shape, lambda i: (0,) * len(shape))
    return pl.pallas_call(
        _mla_prep_kernel,
        grid=(r // tm,),
        in_specs=[pl.BlockSpec((tm, Z_MLA_END), lambda i: (i, 0)),
                  pl.BlockSpec((tm, QK_ROPE), lambda i: (i, 0)),
                  pl.BlockSpec((tm, QK_ROPE), lambda i: (i, 0)),
                  pl.BlockSpec((QK_ROPE, tm), lambda i: (0, i)),
                  pl.BlockSpec((QK_ROPE, tm), lambda i: (0, i)),
                  full((1, Q_LORA)), full((1, KV_LORA)),
                  full(wqt.shape), full(wqrt.shape), full(wk.shape), full(wvt.shape)],
        out_specs=[pl.BlockSpec((1, HEADS, QK_DIM, tm), lambda i: (bidx(i), 0, 0, sblk(i))),
                   pl.BlockSpec((1, HEADS, tm, QK_DIM), lambda i: (bidx(i), 0, sblk(i), 0)),
                   pl.BlockSpec((1, HEADS, VT_ROWS, tm), lambda i: (bidx(i), 0, 0, sblk(i)))],
        out_shape=[jax.ShapeDtypeStruct((b, HEADS, QK_DIM, ltot), BF16),
                   jax.ShapeDtypeStruct((b, HEADS, ltot, QK_DIM), BF16),
                   jax.ShapeDtypeStruct((b, HEADS, VT_ROWS, ltot), BF16)],
        compiler_params=_cparams(("parallel",), VMEM_LIMIT),
        name="mla_prep",
    )(z, cos_u, sin_u, cost_u, sint_u, gq.reshape(1, -1), gkv.reshape(1, -1), wqt, wqrt, wk, wvt)


def _attn_kernel(qt_ref, k_ref, vt_ref, o_ref, s_ref, *, tk, nk):
    qt = qt_ref[0, 0]
    tq = qt.shape[1]

    def scores(c, slot):
        off = pl.multiple_of(c * tk, tk)
        s = jnp.dot(k_ref[0, 0, pl.ds(off, tk), :], qt, preferred_element_type=F32)
        s_ref[slot] = s
        return jnp.max(s, axis=0, keepdims=True)

    def update(c, slot, m, acc, cmax):
        off = pl.multiple_of(c * tk, tk)
        m_new = jnp.maximum(m, cmax)
        p = jnp.exp2(s_ref[slot] - m_new).astype(BF16)
        alpha = jnp.exp2(m - m_new)
        pv = jnp.dot(vt_ref[0, 0, :, pl.ds(off, tk)], p, preferred_element_type=F32)
        return m_new, alpha * acc + pv

    def body(i, carry):
        m, acc, cmax0 = carry
        c = 2 * i
        cmax1 = scores(c + 1, 1)
        m, acc = update(c, 0, m, acc, cmax0)
        cmax0 = scores(c + 2, 0)
        m, acc = update(c + 1, 1, m, acc, cmax1)
        return m, acc, cmax0

    m0 = jnp.full((1, tq), -jnp.inf, F32)
    acc0 = jnp.zeros((VT_ROWS, tq), F32)
    m, acc, cmax0 = lax.fori_loop(0, (nk - 1) // 2, body, (m0, acc0, scores(0, 0)))
    if nk % 2 == 1:
        _, acc = update(nk - 1, 0, m, acc, cmax0)
    else:
        cmax1 = scores(nk - 1, 1)
        m, acc = update(nk - 2, 0, m, acc, cmax0)
        _, acc = update(nk - 1, 1, m, acc, cmax1)
    o = acc[0:V_HEAD] / acc[V_HEAD:V_HEAD + 1]
    o_ref[...] = o.T.astype(o_ref.dtype)


def _attn_call(qt, k, vt, *, b, nq, tq, q_blk0, kv_len, kv_blk0, out_rows, row_blk, prev):
    tk = _pick(kv_len, (1280, 1024, 512, 256, 128))
    nk = kv_len // tk
    return _pcall(
        functools.partial(_attn_kernel, tk=tk, nk=nk),
        prev=prev,
        grid=(b, HEADS, nq),
        in_specs=[pl.BlockSpec((1, 1, QK_DIM, tq), lambda bb, h, i: (bb, h, 0, q_blk0 + i)),
                  pl.BlockSpec((1, 1, kv_len, QK_DIM), lambda bb, h, i: (bb, h, kv_blk0, 0)),
                  pl.BlockSpec((1, 1, VT_ROWS, kv_len), lambda bb, h, i: (bb, h, 0, kv_blk0))],
        args=[qt, k, vt],
        out_specs=pl.BlockSpec((tq, V_HEAD), lambda bb, h, i: (row_blk(bb, i), h)),
        out_shape=jax.ShapeDtypeStruct((out_rows, WIDTH), BF16),
        scratch_shapes=[pltpu.VMEM((2, tk, tq), F32)],
        compiler_params=_cparams(("parallel", "parallel", "arbitrary"), VMEM_LIMIT),
        name="mla_attn",
    )


def _mlstm_gates_kernel(z_ref, b_ref, gc_ref, gr_ref):
    for c in range(gr_ref.shape[0]):
        rows = slice(c * ML_CHUNK, (c + 1) * ML_CHUNK)
        gc = _mlstm_gates_chunk(z_ref[rows, :] + b_ref[...])
        gc_ref[rows, :] = gc
        gr_ref[c] = gc.T


def _mlstm_gates_chunk(g):
    n = g.shape[0]
    lane = lax.broadcasted_iota(jnp.int32, g.shape, 1)
    row = lax.broadcasted_iota(jnp.int32, g.shape, 0)
    lf = _log_sigmoid(g)
    pre = lf
    suf = lf
    k = 1
    while k < n:
        pre = pre + jnp.where(row >= k, pltpu.roll(pre, k, 0), 0.0)
        suf = suf + jnp.where(row < n - k, pltpu.roll(suf, n - k, 0), 0.0)
        k *= 2
    bwd = lane >= 2 * HEADS
    cum = jnp.where(bwd, suf, pre)
    r = g - pltpu.roll(cum, LANE - HEADS, 1)
    pmax = r
    smax = r
    k = 1
    while k < n:
        pmax = jnp.maximum(pmax, jnp.where(row >= k, pltpu.roll(pmax, k, 0), -jnp.inf))
        smax = jnp.maximum(smax, jnp.where(row < n - k, pltpu.roll(smax, n - k, 0), -jnp.inf))
        k *= 2
    cmax = jnp.where(bwd, smax, pmax)
    is_f = ((lane >= HEADS) & (lane < 2 * HEADS)) | ((lane >= 3 * HEADS) & (lane < 4 * HEADS))
    gc = jnp.where(is_f, cum, g)
    gc = jnp.where((lane >= 4 * HEADS) & (lane < 8 * HEADS), pltpu.roll(cmax, 4 * HEADS, 1), gc)
    return jnp.where((lane >= 8 * HEADS) & (lane < 12 * HEADS), pltpu.roll(r, 8 * HEADS, 1), gc)


def _mlstm_gates(z, bias_pad):
    r = z.shape[0]
    nchunk = r // ML_CHUNK
    per = _pick(nchunk, (4, 2, 1))
    return pl.pallas_call(
        _mlstm_gates_kernel,
        grid=(nchunk // per,),
        in_specs=[pl.BlockSpec((per * ML_CHUNK, LANE), lambda i: (i, Z_G // LANE)),
                  pl.BlockSpec((1, LANE), lambda i: (0, 0))],
        out_specs=[pl.BlockSpec((per * ML_CHUNK, LANE), lambda i: (i, 0)),
                   pl.BlockSpec((per, LANE, ML_CHUNK), lambda i: (i, 0, 0))],
        out_shape=[jax.ShapeDtypeStruct((r, LANE), F32),
                   jax.ShapeDtypeStruct((nchunk, LANE, ML_CHUNK), F32)],
        compiler_params=_cparams(("parallel",)),
        name="mlstm_gates",
    )(z, bias_pad)


def _mlstm_kernel(qf_ref, kf_ref, vf_ref, gcf_ref, grf_ref,
                  qb_ref, kb_ref, vb_ref, gcb_ref, grb_ref,
                  hf_ref, hb_ref, c_ref, m_ref):
    L = ML_CHUNK

    @pl.when(pl.program_id(1) == 0)
    def _():
        c_ref[...] = jnp.zeros(c_ref.shape, F32)
        m_ref[...] = jnp.zeros(m_ref.shape, F32)

    row = lax.broadcasted_iota(jnp.int32, (L, L), 0)
    col = lax.broadcasted_iota(jnp.int32, (L, L), 1)
    nt = (((1,), (1,)), ((), ()))
    ones_rows = jnp.ones((ML_ROWS - ML_DQK, L), BF16)
    dirs = ((qf_ref, kf_ref, vf_ref, gcf_ref, grf_ref, hf_ref, row <= col),
            (qb_ref, kb_ref, vb_ref, gcb_ref, grb_ref, hb_ref, row >= col))
    ln_ks = math.log(ML_DQK ** -0.5)
    st = []
    for d, (q_ref, k_ref, v_ref, gc_ref, gr_ref, o_ref, mask) in enumerate(dirs):
        gc = gc_ref[...]
        gr = gr_ref[0]
        for h in range(HEADS):
            li = 2 * HEADS * d + h
            lb = li + HEADS
            lm = li + 4 * HEADS
            lr = li + 8 * HEADS
            brow = gr[lb:lb + 1, :]
            g_tot = brow[:, L - 1:L] if d == 0 else brow[:, 0:1]
            wend = g_tot - brow + gr[li:li + 1, :]
            a = jnp.max(wend, axis=1, keepdims=True)
            m_old = m_ref[d * HEADS + h][0:1, 0:1]
            urow = jnp.maximum(m_old, gr[lm:lm + 1, :])
            m_new = jnp.maximum(g_tot + m_old, a)
            st.append(dict(
                idx=d * HEADS + h, sl=slice(h * ML_DQK, (h + 1) * ML_DQK),
                q_ref=q_ref, k_ref=k_ref, v_ref=v_ref, o_ref=o_ref, mask=mask,
                e=jnp.exp(wend - a + ln_ks), urow=urow, rcol=gc[:, lr:lr + 1],
                di=jnp.exp(m_old - urow), em=jnp.exp(-(brow + urow)), m_new=m_new,
                dec=jnp.exp(g_tot + m_old - m_new), inp=jnp.exp(a - m_new)))
    for t in st:
        sl = t["sl"]
        q = t["q_ref"][:, sl].astype(BF16)
        k = t["k_ref"][:, sl].astype(BF16)
        v_t = t["v_ref"][:, sl].T
        t["vext"] = jnp.concatenate([v_t.astype(BF16), ones_rows], axis=0)
        vext_e = jnp.concatenate([(v_t * t["e"]).astype(BF16), jnp.broadcast_to(t["e"], ones_rows.shape).astype(BF16)],
                                 axis=0)
        t["sT"] = lax.dot_general(k, q, nt, preferred_element_type=F32)
        t["d_c"] = jnp.dot(vext_e, k, preferred_element_type=F32)
        t["qc"] = lax.dot_general(c_ref[t["idx"]].astype(BF16), q, nt, preferred_element_type=F32)
    for t in st:
        dexp = jnp.exp(jnp.where(t["mask"], t["rcol"] - t["urow"] + ln_ks, -jnp.inf))
        t["p"] = (t["sT"] * dexp).astype(BF16)
    for t in st:
        res = t["di"] * t["qc"] + jnp.dot(t["vext"], t["p"], preferred_element_type=F32)
        den = res[ML_DQK:ML_DQK + 1, :]
        h_t = res[0:ML_DQK, :] / jnp.maximum(jnp.abs(den), t["em"])
        t["o_ref"][:, t["sl"]] = h_t.T
        c_ref[t["idx"]] = t["dec"] * c_ref[t["idx"]] + t["inp"] * t["d_c"]
        m_ref[t["idx"]] = jnp.broadcast_to(t["m_new"], (SUBLANE, LANE))


def _mlstm(z, gc, gr, b, s, lc):
    r = z.shape[0]
    L = ML_CHUNK
    nxc, ncc = s // L, lc // L
    x0 = lambda bb: bb * nxc
    c0 = lambda bb: b * nxc + bb * ncc

    def blk_f(bb, st):
        return jnp.where(st < ncc, c0(bb) + st, x0(bb) + st - ncc)

    def blk_b(bb, st):
        return jnp.where(st < ncc, c0(bb) + ncc - 1 - st, x0(bb) + nxc - 1 - (st - ncc))

    def zspec(blk, cb):
        return pl.BlockSpec((L, WIDTH), lambda bb, st: (blk(bb, st), cb))

    def dir_specs(blk):
        return [zspec(blk, Z_MQ // WIDTH), zspec(blk, Z_MK // WIDTH), zspec(blk, Z_MV // WIDTH),
                pl.BlockSpec((L, LANE), lambda bb, st: (blk(bb, st), 0)),
                pl.BlockSpec((1, LANE, L), lambda bb, st: (blk(bb, st), 0, 0))]

    return pl.pallas_call(
        _mlstm_kernel,
        grid=(b, nxc + ncc),
        in_specs=dir_specs(blk_f) + dir_specs(blk_b),
        out_specs=[pl.BlockSpec((L, WIDTH), lambda bb, st: (blk_f(bb, st), 0)),
                   pl.BlockSpec((L, WIDTH), lambda bb, st: (blk_b(bb, st), 0))],
        out_shape=[jax.ShapeDtypeStruct((r, WIDTH), F32), jax.ShapeDtypeStruct((r, WIDTH), F32)],
        scratch_shapes=[pltpu.VMEM((2 * HEADS, ML_ROWS, ML_DQK), F32),
                        pltpu.VMEM((2 * HEADS, SUBLANE, LANE), F32)],
        compiler_params=_cparams(("arbitrary", "arbitrary"), VMEM_LIMIT),
        name="mlstm",
    )(z, z, z, gc, gr, z, z, z, gc, gr)


def _rg_ab_kernel(cur_ref, prev_ref, next_ref, cw_ref, cb_ref, w_ref, ba_ref, bx_ref, lam_ref,
                  af_ref, bf_ref, ab_ref, bb_ref, xe_ref, *, tm, s_len, c_len, rows_x):
    row0 = pl.program_id(0) * tm
    in_x = row0 < rows_x
    seq = jnp.where(in_x, s_len, c_len)
    off = jnp.where(in_x, row0, row0 - rows_x)
    first = lax.rem(off, seq) == 0
    last = lax.rem(off + tm, seq) == 0
    xe_ref[0:SUBLANE, :] = jnp.where(first, 0.0, prev_ref[...])
    xe_ref[SUBLANE:SUBLANE + tm, :] = cur_ref[...]
    xe_ref[SUBLANE + tm:2 * SUBLANE + tm, :] = jnp.where(last, 0.0, next_ref[...])
    lp = RG_CONV // 2
    xc = cb_ref[...] + cw_ref[0:1, :] * xe_ref[pl.ds(SUBLANE - lp, tm), :]
    for j in range(1, RG_CONV):
        xc = xc + cw_ref[j:j + 1, :] * xe_ref[pl.ds(SUBLANE - lp + j, tm), :]
    outs = ((af_ref, bf_ref), (ab_ref, bb_ref))
    for g in range(RG_BLOCKS):
        sl = slice(g * RG_BW, (g + 1) * RG_BW)
        xg = xc[:, sl]
        o = jnp.dot(xg.astype(BF16), w_ref[g], preferred_element_type=F32)
        for d in range(2):
            r = _sigmoid(o[:, (2 * d) * RG_BW:(2 * d + 1) * RG_BW] + ba_ref[d:d + 1, sl])
            i = _sigmoid(o[:, (2 * d + 1) * RG_BW:(2 * d + 2) * RG_BW] + bx_ref[d:d + 1, sl])
            log_a = (-RG_C) * r * _softplus(-lam_ref[d:d + 1, sl])
            a = jnp.exp(log_a)
            one_m_a2 = -jnp.tanh(log_a) * (a * a + 1.0)
            outs[d][0][:, sl] = a
            outs[d][1][:, sl] = jnp.sqrt(one_m_a2) * (i * xg)


def _rg_coeffs(z, cw, cb, w_rg, ba, bx, lam, b, s, lc, tm):
    r = z.shape[0]
    per = tm // SUBLANE
    nblk8 = r // SUBLANE
    cbk = Z_RX // WIDTH
    full = lambda shape: pl.BlockSpec(shape, lambda i: (0,) * len(shape))
    kernel = functools.partial(_rg_ab_kernel, tm=tm, s_len=s, c_len=lc, rows_x=b * s)
    o_spec = pl.BlockSpec((tm, WIDTH), lambda i: (i, 0))
    o_shape = jax.ShapeDtypeStruct((r, WIDTH), F32)
    return pl.pallas_call(
        kernel,
        grid=(r // tm,),
        in_specs=[pl.BlockSpec((tm, WIDTH), lambda i: (i, cbk)),
                  pl.BlockSpec((SUBLANE, WIDTH), lambda i: (jnp.maximum(i * per - 1, 0), cbk)),
                  pl.BlockSpec((SUBLANE, WIDTH), lambda i: (jnp.minimum((i + 1) * per, nblk8 - 1), cbk)),
                  full((RG_CONV, WIDTH)), full((1, WIDTH)), full(w_rg.shape),
                  full((2, WIDTH)), full((2, WIDTH)), full((2, WIDTH))],
        out_specs=[o_spec] * 4,
        out_shape=[o_shape] * 4,
        scratch_shapes=[pltpu.VMEM((tm + 2 * SUBLANE, WIDTH), F32)],
        compiler_params=_cparams(("parallel",), VMEM_LIMIT),
        name="rglru_coeffs",
    )(z, z, z, cw, cb.reshape(1, WIDTH), w_rg, ba, bx, lam)


def _rg_scan_kernel(af_ref, bf_ref, ab_ref, bb_ref, hf_ref, hb_ref, sf_ref, sb_ref, *, tt):
    @pl.when(pl.program_id(1) == 0)
    def _():
        sf_ref[...] = jnp.zeros(sf_ref.shape, F32)
        sb_ref[...] = jnp.zeros(sb_ref.shape, F32)

    def body(t, carry):
        hf, hb = carry
        hf = af_ref[pl.ds(t, 1), :] * hf + bf_ref[pl.ds(t, 1), :]
        hf_ref[pl.ds(t, 1), :] = hf
        tb = tt - 1 - t
        hb = ab_ref[pl.ds(tb, 1), :] * hb + bb_ref[pl.ds(tb, 1), :]
        hb_ref[pl.ds(tb, 1), :] = hb
        return hf, hb

    hf, hb = lax.fori_loop(0, tt, body, (sf_ref[...], sb_ref[...]), unroll=8)
    sf_ref[...] = hf
    sb_ref[...] = hb


def _rg_scan(af, bf, ab, bb, b, s, lc, tt):
    r = af.shape[0]
    nxt, nct = s // tt, lc // tt
    x0 = lambda bi: bi * nxt
    c0 = lambda bi: b * nxt + bi * nct

    def blk_f(bi, st):
        return jnp.where(st < nct, c0(bi) + st, x0(bi) + st - nct)

    def blk_b(bi, st):
        return jnp.where(st < nct, c0(bi) + nct - 1 - st, x0(bi) + nxt - 1 - (st - nct))

    sf = pl.BlockSpec((tt, WIDTH), lambda bi, st: (blk_f(bi, st), 0))
    sb = pl.BlockSpec((tt, WIDTH), lambda bi, st: (blk_b(bi, st), 0))
    o_shape = jax.ShapeDtypeStruct((r, WIDTH), F32)
    return pl.pallas_call(
        functools.partial(_rg_scan_kernel, tt=tt),
        grid=(b, nxt + nct),
        in_specs=[sf, sf, sb, sb],
        out_specs=[sf, sb],
        out_shape=[o_shape, o_shape],
        scratch_shapes=[pltpu.VMEM((1, WIDTH), F32), pltpu.VMEM((1, WIDTH), F32)],
        compiler_params=_cparams(("arbitrary", "arbitrary")),
        name="rglru_scan",
    )(af, bf, ab, bb)


def _merge_kernel(ya_ref, mhf_ref, mhb_ref, zo_ref, ng_ref, rhf_ref, rhb_ref, zg_ref,
                  gm0_ref, gm1_ref, gm2_ref, w0_ref, w1_ref, w2_ref, m_ref, yb_ref, yr_ref):
    @pl.when(pl.program_id(1) == 0)
    def _():
        hsum = mhf_ref[...] + mhb_ref[...]
        for h in range(HEADS):
            sl = slice(h * V_HEAD, (h + 1) * V_HEAD)
            hn = _rms(hsum[:, sl], ng_ref[:, sl])
            yb_ref[:, sl] = (hn * _sigmoid(zo_ref[:, sl])).astype(BF16)
        yr_ref[...] = ((rhf_ref[...] + rhb_ref[...]) * _gelu_tanh(zg_ref[...])).astype(BF16)

    m = (gm0_ref[...].astype(F32) * jnp.dot(ya_ref[...], w0_ref[0], preferred_element_type=F32)
         + gm1_ref[...].astype(F32) * jnp.dot(yb_ref[...], w1_ref[0], preferred_element_type=F32)
         + gm2_ref[...].astype(F32) * jnp.dot(yr_ref[...], w2_ref[0], preferred_element_type=F32))
    m_ref[...] = m.astype(BF16)


def _merge(rows, d, ya, mhf, mhb, z, gm, ng, rhf, rhb, wb, tm):
    tn = _pick(d, (512, 256, 128))
    wide = lambda: pl.BlockSpec((tm, WIDTH), lambda i, j: (i, 0))
    zblk = lambda cb: pl.BlockSpec((tm, WIDTH), lambda i, j: (i, cb))
    zm = lambda br: pl.BlockSpec((tm, tn), lambda i, j: (i, (br * d) // tn + j))
    wspec = lambda br: pl.BlockSpec((1, WIDTH, tn), lambda i, j: (br, 0, j))
    return pl.pallas_call(
        _merge_kernel,
        grid=(rows // tm, d // tn),
        in_specs=[wide(), wide(), wide(), zblk(Z_MO // WIDTH),
                  pl.BlockSpec((1, WIDTH), lambda i, j: (0, 0)),
                  wide(), wide(), zblk(Z_RG // WIDTH),
                  zm(0), zm(1), zm(2), wspec(0), wspec(1), wspec(2)],
        out_specs=pl.BlockSpec((tm, tn), lambda i, j: (i, j)),
        out_shape=jax.ShapeDtypeStruct((rows, d), BF16),
        scratch_shapes=[pltpu.VMEM((tm, WIDTH), BF16), pltpu.VMEM((tm, WIDTH), BF16)],
        compiler_params=_cparams(("parallel", "arbitrary"), VMEM_LIMIT),
        name="branch_merge",
    )(ya, mhf, mhb, z, ng.reshape(1, WIDTH), rhf, rhb, z, gm, gm, gm, wb, wb, wb)


def _outproj_kernel(m_ref, w_ref, x_ref, gate_ref, o_ref):
    o_ref[...] = x_ref[...] + gate_ref[0] * jnp.dot(m_ref[...], w_ref[...], preferred_element_type=F32)


def _out_projection(rows, src, m, w_out, mods3, mod_row, span, prev):
    tm, blk0, nblk = span
    xs, src0 = src
    d = xs.shape[1]
    tn = _pick(d, (1024, 512, 256, 128))
    return _pcall(
        _outproj_kernel,
        prev=prev,
        grid=(nblk, d // tn),
        in_specs=[pl.BlockSpec((tm, d), lambda i, j: (blk0 + i, 0)),
                  pl.BlockSpec((d, tn), lambda i, j: (0, j)),
                  pl.BlockSpec((tm, tn), lambda i, j: (src0 + i, j)),
                  pl.BlockSpec((1, 1, tn), lambda i, j: (mod_row((blk0 + i) * tm), 0, (2 * d) // tn + j))],
        args=[m, w_out, xs, mods3],
        out_specs=pl.BlockSpec((tm, tn), lambda i, j: (blk0 + i, j)),
        out_shape=jax.ShapeDtypeStruct((rows, d), F32),
        compiler_params=_cparams(("parallel", "parallel"), VMEM_LIMIT),
        name="out_proj",
    )


def _mlp_kernel(x_ref, mod_ref, g_ref, w1_ref, w2_ref, fg_ref, o_ref, xn_ref, *, d, final):
    j = pl.program_id(1)

    @pl.when(j == 0)
    def _():
        y = _rms(x_ref[...], g_ref[...])
        shift = mod_ref[0, :, 3 * d:4 * d]
        scale = mod_ref[0, :, 4 * d:5 * d]
        xn_ref[...] = (y * (1.0 + scale) + shift).astype(BF16)
        o_ref[...] = jnp.zeros(o_ref.shape, F32)

    h = jnp.dot(xn_ref[...], w1_ref[...], preferred_element_type=F32)
    h = jnp.square(jnp.maximum(h, 0.0)).astype(BF16)
    o_ref[...] += jnp.dot(h, w2_ref[...], preferred_element_type=F32)

    @pl.when(j == pl.num_programs(1) - 1)
    def _():
        out = x_ref[...] + mod_ref[0, :, 5 * d:6 * d] * o_ref[...]
        if final:
            out = _rms(out, fg_ref[...])
        o_ref[...] = out


def _mlp(rows, x1, mods3, g, w1, w2, fg, mod_row, span, prev, final):
    tm, blk0, nblk = span
    d = x1.shape[1]
    dff = w1.shape[1]
    tf = _pick(dff, (512, 256, 128))
    return _pcall(
        functools.partial(_mlp_kernel, d=d, final=final),
        prev=prev,
        grid=(nblk, dff // tf),
        in_specs=[pl.BlockSpec((tm, d), lambda i, j: (blk0 + i, 0)),
                  pl.BlockSpec((1, 1, N_MOD * d), lambda i, j: (mod_row((blk0 + i) * tm), 0, 0)),
                  pl.BlockSpec((1, d), lambda i, j: (0, 0)),
                  pl.BlockSpec((d, tf), lambda i, j: (0, j)),
                  pl.BlockSpec((tf, d), lambda i, j: (j, 0)),
                  pl.BlockSpec((1, d), lambda i, j: (0, 0))],
        args=[x1, mods3, g.reshape(1, d), w1, w2, fg.reshape(1, d)],
        out_specs=pl.BlockSpec((tm, d), lambda i, j: (blk0 + i, 0)),
        out_shape=jax.ShapeDtypeStruct((rows, d), F32),
        scratch_shapes=[pltpu.VMEM((tm, d), BF16)],
        compiler_params=_cparams(("parallel", "arbitrary"), VMEM_LIMIT),
        name="mlp",
    )


def _rope_perm_cols(w):
    n = QK_ROPE // 4
    return jnp.concatenate([-w[..., n:2 * n], w[..., 0:n], -w[..., 3 * n:4 * n], w[..., 2 * n:3 * n]], axis=-1)


def _layer_weights(l, d, w_in, w_uq, w_ukv, mlstm_gate_b, rg_wa, rg_wx, w_branch, w_out, w_mlp1, w_mlp2):
    wi = w_in[l].astype(BF16)
    o = 0
    parts = {}
    for name, wdt in (("mla_q", Q_LORA), ("mla_kv", KV_LORA), ("mla_kr", QK_ROPE), ("ml_q", WIDTH),
                      ("ml_k", WIDTH), ("ml_v", WIDTH), ("ml_o", WIDTH), ("ml_g", 4 * HEADS),
                      ("rg_x", WIDTH), ("rg_gate", WIDTH), ("merge", 3 * d)):
        parts[name] = wi[:, o:o + wdt]
        o += wdt
    g_pad = jnp.pad(parts["ml_g"], ((0, 0), (0, LANE - 4 * HEADS)))
    w_in_p = jnp.concatenate(
        [parts["mla_q"], parts["mla_kv"], parts["mla_kr"], _rope_perm_cols(parts["mla_kr"]), g_pad,
         parts["ml_q"], parts["ml_k"], parts["ml_v"], parts["ml_o"], parts["rg_x"], parts["rg_gate"],
         parts["merge"]], axis=1)
    wq = w_uq[l].reshape(Q_LORA, HEADS, QK_DIM)
    wqt = jnp.transpose(wq, (1, 2, 0)).reshape(HEADS * QK_DIM, Q_LORA).astype(BF16)
    wqrt = jnp.transpose(_rope_perm_cols(wq[:, :, QK_NOPE:]), (1, 2, 0)).reshape(HEADS * QK_ROPE, Q_LORA).astype(BF16)
    wkv = w_ukv[l].reshape(KV_LORA, HEADS, QK_NOPE + V_HEAD)
    wk = wkv[:, :, :QK_NOPE].reshape(KV_LORA, HEADS * QK_NOPE).astype(BF16)
    wvt = jnp.transpose(wkv[:, :, QK_NOPE:], (1, 2, 0)).reshape(HEADS * V_HEAD, KV_LORA).astype(BF16)
    gate_b = jnp.pad(mlstm_gate_b[l].reshape(1, 4 * HEADS), ((0, 0), (0, LANE - 4 * HEADS)))
    w_rg = jnp.concatenate([rg_wa[l, 0], rg_wx[l, 0], rg_wa[l, 1], rg_wx[l, 1]], axis=-1).astype(BF16)
    return dict(w_in=w_in_p, wqt=wqt, wqrt=wqrt, wk=wk, wvt=wvt, gate_b=gate_b, w_rg=w_rg,
                wb=w_branch[l].astype(BF16), w_out=w_out[l].astype(BF16),
                w1=w_mlp1[l].astype(BF16), w2=w_mlp2[l].astype(BF16))


def _rope_tables(b, s, lc):
    t = jnp.arange(s, dtype=jnp.int32)
    row = (t // GRID_W).astype(F32)
    col = (t % GRID_W).astype(F32)
    n_freq = QK_ROPE // 4
    inv = ROPE_BASE ** (-jnp.arange(n_freq, dtype=F32) / n_freq)
    ang_r = row[:, None] * inv[None, :]
    ang_c = col[:, None] * inv[None, :]
    cos = jnp.concatenate([jnp.cos(ang_r)] * 2 + [jnp.cos(ang_c)] * 2, axis=1)
    sin = jnp.concatenate([jnp.sin(ang_r)] * 2 + [jnp.sin(ang_c)] * 2, axis=1)
    cos_u = jnp.concatenate([jnp.tile(cos, (b, 1)), jnp.ones((b * lc, QK_ROPE), F32)], axis=0)
    sin_u = jnp.concatenate([jnp.tile(sin, (b, 1)), jnp.zeros((b * lc, QK_ROPE), F32)], axis=0)
    return cos_u, sin_u, cos_u.T, sin_u.T


def kernel(x, c, ctx, c_ctx, norm1_g, norm2_g, w_mod, b_mod, w_in, q_norm_g, w_uq, kv_norm_g, w_ukv,
           mlstm_gate_b, mlstm_norm_g, rg_conv_w, rg_conv_b, rg_wa, rg_ba, rg_wx, rg_bx, rg_lam,
           w_branch, w_out, w_mlp1, w_mlp2, final_g):
    b, s, d = x.shape
    lc = ctx.shape[1]
    depth = w_in.shape[0]
    rows_x, rows_c = b * s, b * lc
    r = rows_x + rows_c

    tm_x = _pick(s, (1024, 512, 256, 128))
    tm_c = _pick(math.gcd(rows_x, rows_c), (512, 256, 128))
    tm_mid = _pick(math.gcd(rows_x, rows_c), (512, 256, 128))
    tm_seq = _pick(math.gcd(s, lc), (256, 128))
    tq = _pick(s, (1024, 512, 256, 128))
    span_x = (tm_x, 0, rows_x // tm_x)
    span_c = (tm_c, rows_x // tm_c, rows_c // tm_c)

    def mod_row(row0):
        return jnp.where(row0 < rows_x, 1 + row0 // s, 0)

    def dense(fn, with_ctx):
        out = fn(span_x, src_x, None)
        return fn(span_c, src_c, out) if with_ctx else out

    src_x = (x.reshape(rows_x, d), 0)
    src_c = (ctx.reshape(rows_c, d), 0)
    cc = jnp.concatenate([c_ctx[None, :], c, jnp.zeros((SUBLANE - 1 - b, d), F32)], axis=0)
    tabs = _rope_tables(b, s, lc)

    for l in range(depth):
        last = l == depth - 1
        w = _layer_weights(l, d, w_in, w_uq, w_ukv, mlstm_gate_b, rg_wa, rg_wx, w_branch, w_out,
                           w_mlp1, w_mlp2)
        mods3 = _modulation(cc, w_mod, b_mod, l).reshape(SUBLANE, 1, N_MOD * d)
        z, gm = dense(lambda span, src, prev: _in_projection(r, src, mods3, norm1_g[l], w["w_in"], mod_row,
                                                             span, prev), True)

        qt, kk, vt = _mla_prep(z, tabs, q_norm_g[l], kv_norm_g[l], w["wqt"], w["wqrt"], w["wk"],
                               w["wvt"], b, s, lc, tm_seq)
        ya = _attn_call(qt, kk, vt, b=b, nq=s // tq, tq=tq, q_blk0=0, kv_len=s + lc, kv_blk0=0,
                        out_rows=r, row_blk=lambda bb, i: bb * (s // tq) + i, prev=None)
        if not last:
            ya = _attn_call(qt, kk, vt, b=b, nq=1, tq=lc, q_blk0=s // lc, kv_len=lc, kv_blk0=s // lc,
                            out_rows=r, row_blk=lambda bb, i: rows_x // lc + bb, prev=ya)

        gc, gr = _mlstm_gates(z, w["gate_b"])
        mhf, mhb = _mlstm(z, gc, gr, b, s, lc)

        af, bf, ab, bb_ = _rg_coeffs(z, rg_conv_w[l], rg_conv_b[l], w["w_rg"], rg_ba[l], rg_bx[l],
                                     rg_lam[l], b, s, lc, tm_seq)
        rhf, rhb = _rg_scan(af, bf, ab, bb_, b, s, lc, tm_seq)

        rows = rows_x if last else r
        m = _merge(rows, d, ya, mhf, mhb, z, gm, mlstm_norm_g[l], rhf, rhb, w["wb"], tm_mid)
        x1 = dense(lambda span, src, prev: _out_projection(rows, src, m, w["w_out"], mods3, mod_row, span,
                                                           prev), not last)
        xu = dense(lambda span, src, prev: _mlp(rows, x1, mods3, norm2_g[l], w["w1"], w["w2"], final_g,
                                                mod_row, span, prev, last), not last)
        src_x = (xu, span_x[1])
        src_c = (xu, span_c[1])

    return xu.reshape(b, s, d)
```

```python
import functools
import math

import jax
import jax.numpy as jnp
from jax import lax
from jax.experimental import pallas as pl
from jax.experimental.pallas import tpu as pltpu

F32 = jnp.float32
BF16 = jnp.bfloat16

HEADS = 8
Q_LORA = 512
KV_LORA = 256
QK_NOPE = 128
QK_ROPE = 64
V_HEAD = 128
QK_DIM = QK_NOPE + QK_ROPE
ROPE_BASE = 10000.0
GRID_W = 64
ML_DQK = 128
ML_CHUNK = 128
WIDTH = 1024
RG_BLOCKS = 8
RG_BW = WIDTH // RG_BLOCKS
RG_CONV = 4
RG_C = 8.0
N_MOD = 6
EPS = 1e-6

LANE = 128
SUBLANE = 8
BF16_ROWS = 16
VT_ROWS = V_HEAD + BF16_ROWS
ML_ROWS = ML_DQK + BF16_ROWS

Z_Q = 0
Z_KV = Z_Q + Q_LORA
Z_KR = Z_KV + KV_LORA
Z_KRP = Z_KR + QK_ROPE
Z_MLA_END = Z_KRP + QK_ROPE
Z_G = Z_MLA_END
Z_MQ = Z_G + LANE
Z_MK = Z_MQ + WIDTH
Z_MV = Z_MK + WIDTH
Z_MO = Z_MV + WIDTH
Z_RX = Z_MO + WIDTH
Z_RG = Z_RX + WIDTH
Z_MERGE = Z_RG + WIDTH

Q_SCALE = (QK_DIM ** -0.5) * math.log2(math.e)
VMEM_LIMIT = 56 * 1024 * 1024


def _cparams(sem, vmem=None):
    return pltpu.CompilerParams(dimension_semantics=sem, vmem_limit_bytes=vmem)


def _pick(n, cands):
    for c in cands:
        if n % c == 0:
            return c
    raise ValueError(f"no tile in {cands} divides {n}")


def _pcall(kernel, *, prev, in_specs, args, **kw):
    if prev is None:
        return pl.pallas_call(kernel, in_specs=in_specs, **kw)(*args)
    prevs = list(prev) if isinstance(prev, (tuple, list)) else [prev]
    n_in, n_prev = len(args), len(prevs)

    def aliased(*refs):
        kernel(*refs[:n_in], *refs[n_in + n_prev:])

    return pl.pallas_call(aliased, in_specs=in_specs + [pl.BlockSpec(memory_space=pl.ANY)] * n_prev,
                          input_output_aliases={n_in + k: k for k in range(n_prev)}, **kw)(*args, *prevs)


def _rms(x, g):
    return x * lax.rsqrt(jnp.mean(x * x, axis=-1, keepdims=True) + EPS) * g


def _sigmoid(x):
    return 0.5 * jnp.tanh(0.5 * x) + 0.5


def _softplus(x):
    return jnp.maximum(x, 0.0) + jnp.log1p(jnp.exp(-jnp.abs(x)))


def _log_sigmoid(x):
    return jnp.minimum(x, 0.0) - jnp.log1p(jnp.exp(-jnp.abs(x)))


def _gelu_tanh(x):
    return 0.5 * x * (1.0 + jnp.tanh(math.sqrt(2.0 / math.pi) * (x + 0.044715 * (x * x * x))))


def _mod_kernel(c_ref, w_ref, b_ref, o_ref):
    c = c_ref[...]
    sc = c * jax.nn.sigmoid(c)
    o_ref[...] = jnp.dot(sc, w_ref[0], preferred_element_type=F32,
                         precision=lax.Precision.HIGHEST) + b_ref[0]


def _modulation(cc, w_mod, b_mod, l):
    rows, d = cc.shape
    depth, _, n = w_mod.shape
    tn = _pick(n, (1024, 512, 256, 128))
    return pl.pallas_call(
        _mod_kernel,
        grid=(n // tn,),
        in_specs=[pl.BlockSpec((rows, d), lambda j: (0, 0)),
                  pl.BlockSpec((1, d, tn), lambda j: (l, 0, j)),
                  pl.BlockSpec((1, 1, tn), lambda j: (l, 0, j))],
        out_specs=pl.BlockSpec((rows, tn), lambda j: (0, j)),
        out_shape=jax.ShapeDtypeStruct((rows, n), F32),
        compiler_params=_cparams(("parallel",), VMEM_LIMIT),
        name="adaln_mod",
    )(cc, w_mod, b_mod.reshape(depth, 1, n))


def _inproj_kernel(x_ref, mod_ref, g_ref, w_ref, z_ref, zm_ref, xn_ref, *, d, nz):
    @pl.when(pl.program_id(1) == 0)
    def _():
        y = _rms(x_ref[...], g_ref[...])
        shift = mod_ref[0, :, 0:d]
        scale = mod_ref[0, :, d:2 * d]
        xn_ref[...] = (y * (1.0 + scale) + shift).astype(BF16)

    @pl.when(pl.program_id(1) < nz)
    def _():
        z_ref[...] = jnp.dot(xn_ref[...], w_ref[...], preferred_element_type=F32)

    @pl.when(pl.program_id(1) >= nz)
    def _():
        zm_ref[...] = _sigmoid(jnp.dot(xn_ref[...], w_ref[...], preferred_element_type=F32)).astype(BF16)


def _in_projection(r, src, mods3, g, w_in_p, mod_row, span, prev):
    tm, blk0, nblk = span
    xs, src0 = src
    d = xs.shape[1]
    n = w_in_p.shape[1]
    tn = _pick(math.gcd(Z_MERGE, n - Z_MERGE), (1024, 512, 256, 128))
    nz = Z_MERGE // tn
    return _pcall(
        functools.partial(_inproj_kernel, d=d, nz=nz),
        prev=prev,
        grid=(nblk, n // tn),
        in_specs=[pl.BlockSpec((tm, d), lambda i, j: (src0 + i, 0)),
                  pl.BlockSpec((1, 1, N_MOD * d), lambda i, j: (mod_row((blk0 + i) * tm), 0, 0)),
                  pl.BlockSpec((1, d), lambda i, j: (0, 0)),
                  pl.BlockSpec((d, tn), lambda i, j: (0, j))],
        args=[xs, mods3, g.reshape(1, d), w_in_p],
        out_specs=[pl.BlockSpec((tm, tn), lambda i, j: (blk0 + i, jnp.minimum(j, nz - 1))),
                   pl.BlockSpec((tm, tn), lambda i, j: (blk0 + i, jnp.maximum(j - nz, 0)))],
        out_shape=[jax.ShapeDtypeStruct((r, Z_MERGE), F32), jax.ShapeDtypeStruct((r, n - Z_MERGE), BF16)],
        scratch_shapes=[pltpu.VMEM((tm, d), BF16)],
        compiler_params=_cparams(("parallel", "arbitrary"), VMEM_LIMIT),
        name="in_proj",
    )


def _mla_prep_kernel(z_ref, cos_ref, sin_ref, cost_ref, sint_ref, gq_ref, gkv_ref,
                     wqt_ref, wqrt_ref, wk_ref, wvt_ref, qt_ref, k_ref, vt_ref):
    tm = z_ref.shape[0]
    qn = _rms(z_ref[:, Z_Q:Z_Q + Q_LORA], gq_ref[...]).astype(BF16)
    kvn = _rms(z_ref[:, Z_KV:Z_KV + KV_LORA], gkv_ref[...]).astype(BF16)
    kr = (z_ref[:, Z_KR:Z_KR + QK_ROPE] * cos_ref[...]
          + z_ref[:, Z_KRP:Z_KRP + QK_ROPE] * sin_ref[...]).astype(BF16)
    kn = jnp.dot(kvn, wk_ref[...], preferred_element_type=F32)
    nt = (((1,), (1,)), ((), ()))
    ones = jnp.ones((VT_ROWS - V_HEAD, tm), BF16)
    q_all = lax.dot_general(wqt_ref[...], qn, nt, preferred_element_type=F32)
    qr_all = lax.dot_general(wqrt_ref[...], qn, nt, preferred_element_type=F32)
    v_all = lax.dot_general(wvt_ref[...], kvn, nt, preferred_element_type=F32)
    for h in range(HEADS):
        q_t = q_all[h * QK_DIM:(h + 1) * QK_DIM]
        q_rot = q_t[QK_NOPE:QK_DIM] * cost_ref[...] + qr_all[h * QK_ROPE:(h + 1) * QK_ROPE] * sint_ref[...]
        qt_ref[0, h, 0:QK_NOPE, :] = (q_t[0:QK_NOPE] * Q_SCALE).astype(BF16)
        qt_ref[0, h, QK_NOPE:QK_DIM, :] = (q_rot * Q_SCALE).astype(BF16)
        k_ref[0, h, :, 0:QK_NOPE] = kn[:, h * QK_NOPE:(h + 1) * QK_NOPE].astype(BF16)
        k_ref[0, h, :, QK_NOPE:QK_DIM] = kr
        vt_ref[0, h, 0:V_HEAD, :] = v_all[h * V_HEAD:(h + 1) * V_HEAD].astype(BF16)
        vt_ref[0, h, V_HEAD:VT_ROWS, :] = ones


def _mla_prep(z, tabs, gq, gkv, wqt, wqrt, wk, wvt, b, s, lc, tm):
    r = z.shape[0]
    cos_u, sin_u, cost_u, sint_u = tabs
    nxt, nct = s // tm, lc // tm
    ltot = s + lc

    def bidx(i):
        j = i - b * nxt
        return jnp.where(i < b * nxt, i // nxt, j // nct)

    def sblk(i):
        j = i - b * nxt
        return jnp.where(i < b * nxt, i % nxt, nxt + j % nct)

    full = lambda shape: pl.BlockSpec(shape, lambda i: (0,) * len(shape))
    return pl.pallas_call(
        _mla_prep_kernel,
        grid=(r // tm,),
        in_specs=[pl.BlockSpec((tm, Z_MLA_END), lambda i: (i, 0)),
                  pl.BlockSpec((tm, QK_ROPE), lambda i: (i, 0)),
                  pl.BlockSpec((tm, QK_ROPE), lambda i: (i, 0)),
                  pl.BlockSpec((QK_ROPE, tm), lambda i: (0, i)),
                  pl.BlockSpec((QK_ROPE, tm), lambda i: (0, i)),
                  full((1, Q_LORA)), full((1, KV_LORA)),
                  full(wqt.shape), full(wqrt.shape), full(wk.shape), full(wvt.shape)],
        out_specs=[pl.BlockSpec((1, HEADS, QK_DIM, tm), lambda i: (bidx(i), 0, 0, sblk(i))),
                   pl.BlockSpec((1, HEADS, tm, QK_DIM), lambda i: (bidx(i), 0, sblk(i), 0)),
                   pl.BlockSpec((1, HEADS, VT_ROWS, tm), lambda i: (bidx(i), 0, 0, sblk(i)))],
        out_shape=[jax.ShapeDtypeStruct((b, HEADS, QK_DIM, ltot), BF16),
                   jax.ShapeDtypeStruct((b, HEADS, ltot, QK_DIM), BF16),
                   jax.ShapeDtypeStruct((b, HEADS, VT_ROWS, ltot), BF16)],
        compiler_params=_cparams(("parallel",), VMEM_LIMIT),
        name="mla_prep",
    )(z, cos_u, sin_u, cost_u, sint_u, gq.reshape(1, -1), gkv.reshape(1, -1), wqt, wqrt, wk, wvt)


def _attn_kernel(qt_ref, k_ref, vt_ref, o_ref, s_ref, *, tk, nk):
    qt = qt_ref[0, 0]
    tq = qt.shape[1]

    def scores(c, slot):
        off = pl.multiple_of(c * tk, tk)
        s = jnp.dot(k_ref[0, 0, pl.ds(off, tk), :], qt, preferred_element_type=F32)
        s_ref[slot] = s
        return jnp.max(s, axis=0, keepdims=True)

    def update(c, slot, m, acc, cmax):
        off = pl.multiple_of(c * tk, tk)
        m_new = jnp.maximum(m, cmax)
        p = jnp.exp2(s_ref[slot] - m_new).astype(BF16)
        alpha = jnp.exp2(m - m_new)
        pv = jnp.dot(vt_ref[0, 0, :, pl.ds(off, tk)], p, preferred_element_type=F32)
        return m_new, alpha * acc + pv

    def body(i, carry):
        m, acc, cmax0 = carry
        c = 2 * i
        cmax1 = scores(c + 1, 1)
        m, acc = update(c, 0, m, acc, cmax0)
        cmax0 = scores(c + 2, 0)
        m, acc = update(c + 1, 1, m, acc, cmax1)
        return m, acc, cmax0

    m0 = jnp.full((1, tq), -jnp.inf, F32)
    acc0 = jnp.zeros((VT_ROWS, tq), F32)
    m, acc, cmax0 = lax.fori_loop(0, (nk - 1) // 2, body, (m0, acc0, scores(0, 0)))
    if nk % 2 == 1:
        _, acc = update(nk - 1, 0, m, acc, cmax0)
    else:
        cmax1 = scores(nk - 1, 1)
        m, acc = update(nk - 2, 0, m, acc, cmax0)
        _, acc = update(nk - 1, 1, m, acc, cmax1)
    o = acc[0:V_HEAD] / acc[V_HEAD:V_HEAD + 1]
    o_ref[...] = o.T.astype(o_ref.dtype)


def _attn_call(qt, k, vt, *, b, nq, tq, q_blk0, kv_len, kv_blk0, out_rows, row_blk, prev):
    tk = _pick(kv_len, (1280, 1024, 512, 256, 128))
    nk = kv_len // tk
    return _pcall(
        functools.partial(_attn_kernel, tk=tk, nk=nk),
        prev=prev,
        grid=(b, HEADS, nq),
        in_specs=[pl.BlockSpec((1, 1, QK_DIM, tq), lambda bb, h, i: (bb, h, 0, q_blk0 + i)),
                  pl.BlockSpec((1, 1, kv_len, QK_DIM), lambda bb, h, i: (bb, h, kv_blk0, 0)),
                  pl.BlockSpec((1, 1, VT_ROWS, kv_len), lambda bb, h, i: (bb, h, 0, kv_blk0))],
        args=[qt, k, vt],
        out_specs=pl.BlockSpec((tq, V_HEAD), lambda bb, h, i: (row_blk(bb, i), h)),
        out_shape=jax.ShapeDtypeStruct((out_rows, WIDTH), BF16),
        scratch_shapes=[pltpu.VMEM((2, tk, tq), F32)],
        compiler_params=_cparams(("parallel", "parallel", "arbitrary"), VMEM_LIMIT),
        name="mla_attn",
    )


def _mlstm_gates_kernel(z_ref, b_ref, gc_ref, gr_ref):
    for c in range(gr_ref.shape[0]):
        rows = slice(c * ML_CHUNK, (c + 1) * ML_CHUNK)
        gc = _mlstm_gates_chunk(z_ref[rows, :] + b_ref[...])
        gc_ref[rows, :] = gc
        gr_ref[c] = gc.T


def _mlstm_gates_chunk(g):
    n = g.shape[0]
    lane = lax.broadcasted_iota(jnp.int32, g.shape, 1)
    row = lax.broadcasted_iota(jnp.int32, g.shape, 0)
    lf = _log_sigmoid(g)
    pre = lf
    suf = lf
    k = 1
    while k < n:
        pre = pre + jnp.where(row >= k, pltpu.roll(pre, k, 0), 0.0)
        suf = suf + jnp.where(row < n - k, pltpu.roll(suf, n - k, 0), 0.0)
        k *= 2
    bwd = lane >= 2 * HEADS
    cum = jnp.where(bwd, suf, pre)
    r = g - pltpu.roll(cum, LANE - HEADS, 1)
    pmax = r
    smax = r
    k = 1
    while k < n:
        pmax = jnp.maximum(pmax, jnp.where(row >= k, pltpu.roll(pmax, k, 0), -jnp.inf))
        smax = jnp.maximum(smax, jnp.where(row < n - k, pltpu.roll(smax, n - k, 0), -jnp.inf))
        k *= 2
    cmax = jnp.where(bwd, smax, pmax)
    is_f = ((lane >= HEADS) & (lane < 2 * HEADS)) | ((lane >= 3 * HEADS) & (lane < 4 * HEADS))
    gc = jnp.where(is_f, cum, g)
    gc = jnp.where((lane >= 4 * HEADS) & (lane < 8 * HEADS), pltpu.roll(cmax, 4 * HEADS, 1), gc)
    return jnp.where((lane >= 8 * HEADS) & (lane < 12 * HEADS), pltpu.roll(r, 8 * HEADS, 1), gc)


def _mlstm_gates(z, bias_pad):
    r = z.shape[0]
    nchunk = r // ML_CHUNK
    per = _pick(nchunk, (4, 2, 1))
    return pl.pallas_call(
        _mlstm_gates_kernel,
        grid=(nchunk // per,),
        in_specs=[pl.BlockSpec((per * ML_CHUNK, LANE), lambda i: (i, Z_G // LANE)),
                  pl.BlockSpec((1, LANE), lambda i: (0, 0))],
        out_specs=[pl.BlockSpec((per * ML_CHUNK, LANE), lambda i: (i, 0)),
                   pl.BlockSpec((per, LANE, ML_CHUNK), lambda i: (i, 0, 0))],
        out_shape=[jax.ShapeDtypeStruct((r, LANE), F32),
                   jax.ShapeDtypeStruct((nchunk, LANE, ML_CHUNK), F32)],
        compiler_params=_cparams(("parallel",)),
        name="mlstm_gates",
    )(z, bias_pad)


def _mlstm_kernel(qf_ref, kf_ref, vf_ref, gcf_ref, grf_ref,
                  qb_ref, kb_ref, vb_ref, gcb_ref, grb_ref,
                  hf_ref, hb_ref, c_ref, m_ref):
    L = ML_CHUNK

    @pl.when(pl.program_id(1) == 0)
    def _():
        c_ref[...] = jnp.zeros(c_ref.shape, F32)
        m_ref[...] = jnp.zeros(m_ref.shape, F32)

    row = lax.broadcasted_iota(jnp.int32, (L, L), 0)
    col = lax.broadcasted_iota(jnp.int32, (L, L), 1)
    nt = (((1,), (1,)), ((), ()))
    ones_rows = jnp.ones((ML_ROWS - ML_DQK, L), BF16)
    dirs = ((qf_ref, kf_ref, vf_ref, gcf_ref, grf_ref, hf_ref, row <= col),
            (qb_ref, kb_ref, vb_ref, gcb_ref, grb_ref, hb_ref, row >= col))
    ln_ks = math.log(ML_DQK ** -0.5)
    st = []
    for d, (q_ref, k_ref, v_ref, gc_ref, gr_ref, o_ref, mask) in enumerate(dirs):
        gc = gc_ref[...]
        gr = gr_ref[0]
        for h in range(HEADS):
            li = 2 * HEADS * d + h
            lb = li + HEADS
            lm = li + 4 * HEADS
            lr = li + 8 * HEADS
            brow = gr[lb:lb + 1, :]
            g_tot = brow[:, L - 1:L] if d == 0 else brow[:, 0:1]
            wend = g_tot - brow + gr[li:li + 1, :]
            a = jnp.max(wend, axis=1, keepdims=True)
            m_old = m_ref[d * HEADS + h][0:1, 0:1]
            urow = jnp.maximum(m_old, gr[lm:lm + 1, :])
            m_new = jnp.maximum(g_tot + m_old, a)
            st.append(dict(
                idx=d * HEADS + h, sl=slice(h * ML_DQK, (h + 1) * ML_DQK),
                q_ref=q_ref, k_ref=k_ref, v_ref=v_ref, o_ref=o_ref, mask=mask,
                e=jnp.exp(wend - a + ln_ks), urow=urow, rcol=gc[:, lr:lr + 1],
                di=jnp.exp(m_old - urow), em=jnp.exp(-(brow + urow)), m_new=m_new,
                dec=jnp.exp(g_tot + m_old - m_new), inp=jnp.exp(a - m_new)))
    for t in st:
        sl = t["sl"]
        q = t["q_ref"][:, sl].astype(BF16)
        k = t["k_ref"][:, sl].astype(BF16)
        v_t = t["v_ref"][:, sl].T
        t["vext"] = jnp.concatenate([v_t.astype(BF16), ones_rows], axis=0)
        vext_e = jnp.concatenate([(v_t * t["e"]).astype(BF16), jnp.broadcast_to(t["e"], ones_rows.shape).astype(BF16)],
                                 axis=0)
        t["sT"] = lax.dot_general(k, q, nt, preferred_element_type=F32)
        t["d_c"] = jnp.dot(vext_e, k, preferred_element_type=F32)
        t["qc"] = lax.dot_general(c_ref[t["idx"]].astype(BF16), q, nt, preferred_element_type=F32)
    for t in st:
        dexp = jnp.exp(jnp.where(t["mask"], t["rcol"] - t["urow"] + ln_ks, -jnp.inf))
        t["p"] = (t["sT"] * dexp).astype(BF16)
    for t in st:
        res = t["di"] * t["qc"] + jnp.dot(t["vext"], t["p"], preferred_element_type=F32)
        den = res[ML_DQK:ML_DQK + 1, :]
        h_t = res[0:ML_DQK, :] / jnp.maximum(jnp.abs(den), t["em"])
        t["o_ref"][:, t["sl"]] = h_t.T
        c_ref[t["idx"]] = t["dec"] * c_ref[t["idx"]] + t["inp"] * t["d_c"]
        m_ref[t["idx"]] = jnp.broadcast_to(t["m_new"], (SUBLANE, LANE))


def _mlstm(z, gc, gr, b, s, lc):
    r = z.shape[0]
    L = ML_CHUNK
    nxc, ncc = s // L, lc // L
    x0 = lambda bb: bb * nxc
    c0 = lambda bb: b * nxc + bb * ncc

    def blk_f(bb, st):
        return jnp.where(st < ncc, c0(bb) + st, x0(bb) + st - ncc)

    def blk_b(bb, st):
        return jnp.where(st < ncc, c0(bb) + ncc - 1 - st, x0(bb) + nxc - 1 - (st - ncc))

    def zspec(blk, cb):
        return pl.BlockSpec((L, WIDTH), lambda bb, st: (blk(bb, st), cb))

    def dir_specs(blk):
        return [zspec(blk, Z_MQ // WIDTH), zspec(blk, Z_MK // WIDTH), zspec(blk, Z_MV // WIDTH),
                pl.BlockSpec((L, LANE), lambda bb, st: (blk(bb, st), 0)),
                pl.BlockSpec((1, LANE, L), lambda bb, st: (blk(bb, st), 0, 0))]

    return pl.pallas_call(
        _mlstm_kernel,
        grid=(b, nxc + ncc),
        in_specs=dir_specs(blk_f) + dir_specs(blk_b),
        out_specs=[pl.BlockSpec((L, WIDTH), lambda bb, st: (blk_f(bb, st), 0)),
                   pl.BlockSpec((L, WIDTH), lambda bb, st: (blk_b(bb, st), 0))],
        out_shape=[jax.ShapeDtypeStruct((r, WIDTH), F32), jax.ShapeDtypeStruct((r, WIDTH), F32)],
        scratch_shapes=[pltpu.VMEM((2 * HEADS, ML_ROWS, ML_DQK), F32),
                        pltpu.VMEM((2 * HEADS, SUBLANE, LANE), F32)],
        compiler_params=_cparams(("arbitrary", "arbitrary"), VMEM_LIMIT),
        name="mlstm",
    )(z, z, z, gc, gr, z, z, z, gc, gr)


def _rg_ab_kernel(cur_ref, prev_ref, next_ref, cw_ref, cb_ref, w_ref, ba_ref, bx_ref, lam_ref,
                  af_ref, bf_ref, ab_ref, bb_ref, xe_ref, *, tm, s_len, c_len, rows_x):
    row0 = pl.program_id(0) * tm
    in_x = row0 < rows_x
    seq = jnp.where(in_x, s_len, c_len)
    off = jnp.where(in_x, row0, row0 - rows_x)
    first = lax.rem(off, seq) == 0
    last = lax.rem(off + tm, seq) == 0
    xe_ref[0:SUBLANE, :] = jnp.where(first, 0.0, prev_ref[...])
    xe_ref[SUBLANE:SUBLANE + tm, :] = cur_ref[...]
    xe_ref[SUBLANE + tm:2 * SUBLANE + tm, :] = jnp.where(last, 0.0, next_ref[...])
    lp = RG_CONV // 2
    xc = cb_ref[...] + cw_ref[0:1, :] * xe_ref[pl.ds(SUBLANE - lp, tm), :]
    for j in range(1, RG_CONV):
        xc = xc + cw_ref[j:j + 1, :] * xe_ref[pl.ds(SUBLANE - lp + j, tm), :]
    outs = ((af_ref, bf_ref), (ab_ref, bb_ref))
    for g in range(RG_BLOCKS):
        sl = slice(g * RG_BW, (g + 1) * RG_BW)
        xg = xc[:, sl]
        o = jnp.dot(xg.astype(BF16), w_ref[g], preferred_element_type=F32)
        for d in range(2):
            r = _sigmoid(o[:, (2 * d) * RG_BW:(2 * d + 1) * RG_BW] + ba_ref[d:d + 1, sl])
            i = _sigmoid(o[:, (2 * d + 1) * RG_BW:(2 * d + 2) * RG_BW] + bx_ref[d:d + 1, sl])
            log_a = (-RG_C) * r * _softplus(-lam_ref[d:d + 1, sl])
            a = jnp.exp(log_a)
            one_m_a2 = -jnp.tanh(log_a) * (a * a + 1.0)
            outs[d][0][:, sl] = a
            outs[d][1][:, sl] = jnp.sqrt(one_m_a2) * (i * xg)


def _rg_coeffs(z, cw, cb, w_rg, ba, bx, lam, b, s, lc, tm):
    r = z.shape[0]
    per = tm // SUBLANE
    nblk8 = r // SUBLANE
    cbk = Z_RX // WIDTH
    full = lambda shape: pl.BlockSpec(shape, lambda i: (0,) * len(shape))
    kernel = functools.partial(_rg_ab_kernel, tm=tm, s_len=s, c_len=lc, rows_x=b * s)
    o_spec = pl.BlockSpec((tm, WIDTH), lambda i: (i, 0))
    o_shape = jax.ShapeDtypeStruct((r, WIDTH), F32)
    return pl.pallas_call(
        kernel,
        grid=(r // tm,),
        in_specs=[pl.BlockSpec((tm, WIDTH), lambda i: (i, cbk)),
                  pl.BlockSpec((SUBLANE, WIDTH), lambda i: (jnp.maximum(i * per - 1, 0), cbk)),
                  pl.BlockSpec((SUBLANE, WIDTH), lambda i: (jnp.minimum((i + 1) * per, nblk8 - 1), cbk)),
                  full((RG_CONV, WIDTH)), full((1, WIDTH)), full(w_rg.shape),
                  full((2, WIDTH)), full((2, WIDTH)), full((2, WIDTH))],
        out_specs=[o_spec] * 4,
        out_shape=[o_shape] * 4,
        scratch_shapes=[pltpu.VMEM((tm + 2 * SUBLANE, WIDTH), F32)],
        compiler_params=_cparams(("parallel",), VMEM_LIMIT),
        name="rglru_coeffs",
    )(z, z, z, cw, cb.reshape(1, WIDTH), w_rg, ba, bx, lam)


def _rg_scan_kernel(af_ref, bf_ref, ab_ref, bb_ref, hf_ref, hb_ref, sf_ref, sb_ref, *, tt):
    @pl.when(pl.program_id(1) == 0)
    def _():
        sf_ref[...] = jnp.zeros(sf_ref.shape, F32)
        sb_ref[...] = jnp.zeros(sb_ref.shape, F32)

    def body(t, carry):
        hf, hb = carry
        hf = af_ref[pl.ds(t, 1), :] * hf + bf_ref[pl.ds(t, 1), :]
        hf_ref[pl.ds(t, 1), :] = hf
        tb = tt - 1 - t
        hb = ab_ref[pl.ds(tb, 1), :] * hb + bb_ref[pl.ds(tb, 1), :]
        hb_ref[pl.ds(tb, 1), :] = hb
        return hf, hb

    hf, hb = lax.fori_loop(0, tt, body, (sf_ref[...], sb_ref[...]), unroll=8)
    sf_ref[...] = hf
    sb_ref[...] = hb


def _rg_scan(af, bf, ab, bb, b, s, lc, tt):
    r = af.shape[0]
    nxt, nct = s // tt, lc // tt
    x0 = lambda bi: bi * nxt
    c0 = lambda bi: b * nxt + bi * nct

    def blk_f(bi, st):
        return jnp.where(st < nct, c0(bi) + st, x0(bi) + st - nct)

    def blk_b(bi, st):
        return jnp.where(st < nct, c0(bi) + nct - 1 - st, x0(bi) + nxt - 1 - (st - nct))

    sf = pl.BlockSpec((tt, WIDTH), lambda bi, st: (blk_f(bi, st), 0))
    sb = pl.BlockSpec((tt, WIDTH), lambda bi, st: (blk_b(bi, st), 0))
    o_shape = jax.ShapeDtypeStruct((r, WIDTH), F32)
    return pl.pallas_call(
        functools.partial(_rg_scan_kernel, tt=tt),
        grid=(b, nxt + nct),
        in_specs=[sf, sf, sb, sb],
        out_specs=[sf, sb],
        out_shape=[o_shape, o_shape],
        scratch_shapes=[pltpu.VMEM((1, WIDTH), F32), pltpu.VMEM((1, WIDTH), F32)],
        compiler_params=_cparams(("arbitrary", "arbitrary")),
        name="rglru_scan",
    )(af, bf, ab, bb)


def _finish_kernel(mhf_ref, mhb_ref, zo_ref, ng_ref, rhf_ref, rhb_ref, zg_ref, yb_ref, yr_ref):
    hsum = mhf_ref[...] + mhb_ref[...]
    for h in range(HEADS):
        sl = slice(h * V_HEAD, (h + 1) * V_HEAD)
        hn = _rms(hsum[:, sl], ng_ref[:, sl])
        yb_ref[:, sl] = (hn * _sigmoid(zo_ref[:, sl])).astype(BF16)
    yr_ref[...] = ((rhf_ref[...] + rhb_ref[...]) * _gelu_tanh(zg_ref[...])).astype(BF16)


def _finish(rows, mhf, mhb, z, ng, rhf, rhb, tm):
    wide = lambda: pl.BlockSpec((tm, WIDTH), lambda i: (i, 0))
    zblk = lambda cb: pl.BlockSpec((tm, WIDTH), lambda i: (i, cb))
    o_shape = jax.ShapeDtypeStruct((rows, WIDTH), BF16)
    return pl.pallas_call(
        _finish_kernel,
        grid=(rows // tm,),
        in_specs=[wide(), wide(), zblk(Z_MO // WIDTH), pl.BlockSpec((1, WIDTH), lambda i: (0, 0)),
                  wide(), wide(), zblk(Z_RG // WIDTH)],
        out_specs=[wide(), wide()],
        out_shape=[o_shape, o_shape],
        compiler_params=_cparams(("parallel",), VMEM_LIMIT),
        name="branch_finish",
    )(mhf, mhb, z, ng.reshape(1, WIDTH), rhf, rhb, z)


def _merge_kernel(ya_ref, yb_ref, yr_ref, gm0_ref, gm1_ref, gm2_ref, w0_ref, w1_ref, w2_ref, m_ref):
    m = (gm0_ref[...].astype(F32) * jnp.dot(ya_ref[...], w0_ref[0], preferred_element_type=F32)
         + gm1_ref[...].astype(F32) * jnp.dot(yb_ref[...], w1_ref[0], preferred_element_type=F32)
         + gm2_ref[...].astype(F32) * jnp.dot(yr_ref[...], w2_ref[0], preferred_element_type=F32))
    m_ref[...] = m.astype(BF16)


def _merge(rows, d, ya, yb, yr, gm, wb, tm):
    tn = _pick(d, (1024, 512, 256, 128))
    wide = lambda: pl.BlockSpec((tm, WIDTH), lambda j, i: (i, 0))
    gate = lambda br: pl.BlockSpec((tm, tn), lambda j, i: (i, (br * d) // tn + j))
    wspec = lambda br: pl.BlockSpec((1, WIDTH, tn), lambda j, i: (br, 0, j))
    return pl.pallas_call(
        _merge_kernel,
        grid=(d // tn, rows // tm),
        in_specs=[wide(), wide(), wide(), gate(0), gate(1), gate(2), wspec(0), wspec(1), wspec(2)],
        out_specs=pl.BlockSpec((tm, tn), lambda j, i: (i, j)),
        out_shape=jax.ShapeDtypeStruct((rows, d), BF16),
        compiler_params=_cparams(("parallel", "parallel"), VMEM_LIMIT),
        name="branch_merge",
    )(ya, yb, yr, gm, gm, gm, wb, wb, wb)


def _outproj_kernel(m_ref, w_ref, x_ref, gate_ref, o_ref):
    o_ref[...] = x_ref[...] + gate_ref[0] * jnp.dot(m_ref[...], w_ref[...], preferred_element_type=F32)


def _out_projection(rows, src, m, w_out, mods3, mod_row, span, prev):
    tm, blk0, nblk = span
    xs, src0 = src
    d = xs.shape[1]
    tn = _pick(d, (1024, 512, 256, 128))
    return _pcall(
        _outproj_kernel,
        prev=prev,
        grid=(nblk, d // tn),
        in_specs=[pl.BlockSpec((tm, d), lambda i, j: (blk0 + i, 0)),
                  pl.BlockSpec((d, tn), lambda i, j: (0, j)),
                  pl.BlockSpec((tm, tn), lambda i, j: (src0 + i, j)),
                  pl.BlockSpec((1, 1, tn), lambda i, j: (mod_row((blk0 + i) * tm), 0, (2 * d) // tn + j))],
        args=[m, w_out, xs, mods3],
        out_specs=pl.BlockSpec((tm, tn), lambda i, j: (blk0 + i, j)),
        out_shape=jax.ShapeDtypeStruct((rows, d), F32),
        compiler_params=_cparams(("parallel", "parallel"), VMEM_LIMIT),
        name="out_proj",
    )


def _mlp_kernel(x_ref, mod_ref, g_ref, w1_ref, w2_ref, fg_ref, o_ref, xn_ref, *, d, final):
    j = pl.program_id(1)

    @pl.when(j == 0)
    def _():
        y = _rms(x_ref[...], g_ref[...])
        shift = mod_ref[0, :, 3 * d:4 * d]
        scale = mod_ref[0, :, 4 * d:5 * d]
        xn_ref[...] = (y * (1.0 + scale) + shift).astype(BF16)
        o_ref[...] = jnp.zeros(o_ref.shape, F32)

    h = jnp.dot(xn_ref[...], w1_ref[...], preferred_element_type=F32)
    h = jnp.square(jnp.maximum(h, 0.0)).astype(BF16)
    o_ref[...] += jnp.dot(h, w2_ref[...], preferred_element_type=F32)

    @pl.when(j == pl.num_programs(1) - 1)
    def _():
        out = x_ref[...] + mod_ref[0, :, 5 * d:6 * d] * o_ref[...]
        if final:
            out = _rms(out, fg_ref[...])
        o_ref[...] = out


def _mlp(rows, x1, mods3, g, w1, w2, fg, mod_row, span, prev, final):
    tm, blk0, nblk = span
    d = x1.shape[1]
    dff = w1.shape[1]
    tf = _pick(dff, (512, 256, 128))
    return _pcall(
        functools.partial(_mlp_kernel, d=d, final=final),
        prev=prev,
        grid=(nblk, dff // tf),
        in_specs=[pl.BlockSpec((tm, d), lambda i, j: (blk0 + i, 0)),
                  pl.BlockSpec((1, 1, N_MOD * d), lambda i, j: (mod_row((blk0 + i) * tm), 0, 0)),
                  pl.BlockSpec((1, d), lambda i, j: (0, 0)),
                  pl.BlockSpec((d, tf), lambda i, j: (0, j)),
                  pl.BlockSpec((tf, d), lambda i, j: (j, 0)),
                  pl.BlockSpec((1, d), lambda i, j: (0, 0))],
        args=[x1, mods3, g.reshape(1, d), w1, w2, fg.reshape(1, d)],
        out_specs=pl.BlockSpec((tm, d), lambda i, j: (blk0 + i, 0)),
        out_shape=jax.ShapeDtypeStruct((rows, d), F32),
        scratch_shapes=[pltpu.VMEM((tm, d), BF16)],
        compiler_params=_cparams(("parallel", "arbitrary"), VMEM_LIMIT),
        name="mlp",
    )


def _rope_perm_cols(w):
    n = QK_ROPE // 4
    return jnp.concatenate([-w[..., n:2 * n], w[..., 0:n], -w[..., 3 * n:4 * n], w[..., 2 * n:3 * n]], axis=-1)


def _layer_weights(l, d, w_in, w_uq, w_ukv, mlstm_gate_b, rg_wa, rg_wx, w_branch, w_out, w_mlp1, w_mlp2):
    wi = w_in[l].astype(BF16)
    o = 0
    parts = {}
    for name, wdt in (("mla_q", Q_LORA), ("mla_kv", KV_LORA), ("mla_kr", QK_ROPE), ("ml_q", WIDTH),
                      ("ml_k", WIDTH), ("ml_v", WIDTH), ("ml_o", WIDTH), ("ml_g", 4 * HEADS),
                      ("rg_x", WIDTH), ("rg_gate", WIDTH), ("merge", 3 * d)):
        parts[name] = wi[:, o:o + wdt]
        o += wdt
    g_pad = jnp.pad(parts["ml_g"], ((0, 0), (0, LANE - 4 * HEADS)))
    w_in_p = jnp.concatenate(
        [parts["mla_q"], parts["mla_kv"], parts["mla_kr"], _rope_perm_cols(parts["mla_kr"]), g_pad,
         parts["ml_q"], parts["ml_k"], parts["ml_v"], parts["ml_o"], parts["rg_x"], parts["rg_gate"],
         parts["merge"]], axis=1)
    wq = w_uq[l].reshape(Q_LORA, HEADS, QK_DIM)
    wqt = jnp.transpose(wq, (1, 2, 0)).reshape(HEADS * QK_DIM, Q_LORA).astype(BF16)
    wqrt = jnp.transpose(_rope_perm_cols(wq[:, :, QK_NOPE:]), (1, 2, 0)).reshape(HEADS * QK_ROPE, Q_LORA).astype(BF16)
    wkv = w_ukv[l].reshape(KV_LORA, HEADS, QK_NOPE + V_HEAD)
    wk = wkv[:, :, :QK_NOPE].reshape(KV_LORA, HEADS * QK_NOPE).astype(BF16)
    wvt = jnp.transpose(wkv[:, :, QK_NOPE:], (1, 2, 0)).reshape(HEADS * V_HEAD, KV_LORA).astype(BF16)
    gate_b = jnp.pad(mlstm_gate_b[l].reshape(1, 4 * HEADS), ((0, 0), (0, LANE - 4 * HEADS)))
    w_rg = jnp.concatenate([rg_wa[l, 0], rg_wx[l, 0], rg_wa[l, 1], rg_wx[l, 1]], axis=-1).astype(BF16)
    return dict(w_in=w_in_p, wqt=wqt, wqrt=wqrt, wk=wk, wvt=wvt, gate_b=gate_b, w_rg=w_rg,
                wb=w_branch[l].astype(BF16), w_out=w_out[l].astype(BF16),
                w1=w_mlp1[l].astype(BF16), w2=w_mlp2[l].astype(BF16))


def _rope_tables(b, s, lc):
    t = jnp.arange(s, dtype=jnp.int32)
    row = (t // GRID_W).astype(F32)
    col = (t % GRID_W).astype(F32)
    n_freq = QK_ROPE // 4
    inv = ROPE_BASE ** (-jnp.arange(n_freq, dtype=F32) / n_freq)
    ang_r = row[:, None] * inv[None, :]
    ang_c = col[:, None] * inv[None, :]
    cos = jnp.concatenate([jnp.cos(ang_r)] * 2 + [jnp.cos(ang_c)] * 2, axis=1)
    sin = jnp.concatenate([jnp.sin(ang_r)] * 2 + [jnp.sin(ang_c)] * 2, axis=1)
    cos_u = jnp.concatenate([jnp.tile(cos, (b, 1)), jnp.ones((b * lc, QK_ROPE), F32)], axis=0)
    sin_u = jnp.concatenate([jnp.tile(sin, (b, 1)), jnp.zeros((b * lc, QK_ROPE), F32)], axis=0)
    return cos_u, sin_u, cos_u.T, sin_u.T


def kernel(x, c, ctx, c_ctx, norm1_g, norm2_g, w_mod, b_mod, w_in, q_norm_g, w_uq, kv_norm_g, w_ukv,
           mlstm_gate_b, mlstm_norm_g, rg_conv_w, rg_conv_b, rg_wa, rg_ba, rg_wx, rg_bx, rg_lam,
           w_branch, w_out, w_mlp1, w_mlp2, final_g):
    b, s, d = x.shape
    lc = ctx.shape[1]
    depth = w_in.shape[0]
    rows_x, rows_c = b * s, b * lc
    r = rows_x + rows_c

    tm_x = _pick(s, (1024, 512, 256, 128))
    tm_c = _pick(math.gcd(rows_x, rows_c), (512, 256, 128))
    tm_mid = _pick(math.gcd(rows_x, rows_c), (512, 256, 128))
    tm_seq = _pick(math.gcd(s, lc), (256, 128))
    tq = _pick(s, (1024, 512, 256, 128))
    span_x = (tm_x, 0, rows_x // tm_x)
    span_c = (tm_c, rows_x // tm_c, rows_c // tm_c)

    def mod_row(row0):
        return jnp.where(row0 < rows_x, 1 + row0 // s, 0)

    def dense(fn, with_ctx):
        out = fn(span_x, src_x, None)
        return fn(span_c, src_c, out) if with_ctx else out

    src_x = (x.reshape(rows_x, d), 0)
    src_c = (ctx.reshape(rows_c, d), 0)
    cc = jnp.concatenate([c_ctx[None, :], c, jnp.zeros((SUBLANE - 1 - b, d), F32)], axis=0)
    tabs = _rope_tables(b, s, lc)

    for l in range(depth):
        last = l == depth - 1
        w = _layer_weights(l, d, w_in, w_uq, w_ukv, mlstm_gate_b, rg_wa, rg_wx, w_branch, w_out,
                           w_mlp1, w_mlp2)
        mods3 = _modulation(cc, w_mod, b_mod, l).reshape(SUBLANE, 1, N_MOD * d)
        z, gm = dense(lambda span, src, prev: _in_projection(r, src, mods3, norm1_g[l], w["w_in"], mod_row,
                                                             span, prev), True)

        qt, kk, vt = _mla_prep(z, tabs, q_norm_g[l], kv_norm_g[l], w["wqt"], w["wqrt"], w["wk"],
                               w["wvt"], b, s, lc, tm_seq)
        ya = _attn_call(qt, kk, vt, b=b, nq=s // tq, tq=tq, q_blk0=0, kv_len=s + lc, kv_blk0=0,
                        out_rows=r, row_blk=lambda bb, i: bb * (s // tq) + i, prev=None)
        if not last:
            ya = _attn_call(qt, kk, vt, b=b, nq=1, tq=lc, q_blk0=s // lc, kv_len=lc, kv_blk0=s // lc,
                            out_rows=r, row_blk=lambda bb, i: rows_x // lc + bb, prev=ya)

        gc, gr = _mlstm_gates(z, w["gate_b"])
        mhf, mhb = _mlstm(z, gc, gr, b, s, lc)

        af, bf, ab, bb_ = _rg_coeffs(z, rg_conv_w[l], rg_conv_b[l], w["w_rg"], rg_ba[l], rg_bx[l],
                                     rg_lam[l], b, s, lc, tm_seq)
        rhf, rhb = _rg_scan(af, bf, ab, bb_, b, s, lc, tm_seq)

        rows = rows_x if last else r
        yb, yr = _finish(rows, mhf, mhb, z, mlstm_norm_g[l], rhf, rhb, tm_mid)
        m = _merge(rows, d, ya, yb, yr, gm, w["wb"], tm_mid)
        x1 = dense(lambda span, src, prev: _out_projection(rows, src, m, w["w_out"], mods3, mod_row, span,
                                                           prev), not last)
        xu = dense(lambda span, src, prev: _mlp(rows, x1, mods3, norm2_g[l], w["w1"], w["w2"], final_g,
                                                mod_row, span, prev, last), not last)
        src_x = (xu, span_x[1])
        src_c = (xu, span_c[1])

    return xu.reshape(b, s, d)
```

```python
import functools
import math

import jax
import jax.numpy as jnp
from jax import lax
from jax.experimental import pallas as pl
from jax.experimental.pallas import tpu as pltpu

F32 = jnp.float32
BF16 = jnp.bfloat16

HEADS = 8
Q_LORA = 512
KV_LORA = 256
QK_NOPE = 128
QK_ROPE = 64
V_HEAD = 128
QK_DIM = QK_NOPE + QK_ROPE
ROPE_BASE = 10000.0
GRID_W = 64
ML_DQK = 128
ML_CHUNK = 128
WIDTH = 1024
RG_BLOCKS = 8
RG_BW = WIDTH // RG_BLOCKS
RG_CONV = 4
RG_C = 8.0
N_MOD = 6
EPS = 1e-6

LANE = 128
SUBLANE = 8
BF16_ROWS = 16
MXU_DIM = 256
ATTN_TK = 5 * MXU_DIM
VT_ROWS = V_HEAD + BF16_ROWS
ML_ROWS = ML_DQK + BF16_ROWS

Z_Q = 0
Z_KV = Z_Q + Q_LORA
Z_KR = Z_KV + KV_LORA
Z_KRP = Z_KR + QK_ROPE
Z_MLA_END = Z_KRP + QK_ROPE
Z_G = Z_MLA_END
Z_MQ = Z_G + LANE
Z_MK = Z_MQ + WIDTH
Z_MV = Z_MK + WIDTH
Z_MO = Z_MV + WIDTH
Z_RX = Z_MO + WIDTH
Z_RG = Z_RX + WIDTH
Z_MERGE = Z_RG + WIDTH

Q_SCALE = (QK_DIM ** -0.5) * math.log2(math.e)
VMEM_LIMIT = 56 * 1024 * 1024


def _cparams(sem, vmem=None):
    return pltpu.CompilerParams(dimension_semantics=sem, vmem_limit_bytes=vmem)


def _pick(n, cands):
    for c in cands:
        if n % c == 0:
            return c
    raise ValueError(f"no tile in {cands} divides {n}")


def _pcall(kernel, *, prev, in_specs, args, **kw):
    if prev is None:
        return pl.pallas_call(kernel, in_specs=in_specs, **kw)(*args)
    prevs = list(prev) if isinstance(prev, (tuple, list)) else [prev]
    n_in, n_prev = len(args), len(prevs)

    def aliased(*refs):
        kernel(*refs[:n_in], *refs[n_in + n_prev:])

    return pl.pallas_call(aliased, in_specs=in_specs + [pl.BlockSpec(memory_space=pl.ANY)] * n_prev,
                          input_output_aliases={n_in + k: k for k in range(n_prev)}, **kw)(*args, *prevs)


def _rms(x, g):
    return x * lax.rsqrt(jnp.mean(x * x, axis=-1, keepdims=True) + EPS) * g


def _sigmoid(x):
    return 0.5 * jnp.tanh(0.5 * x) + 0.5


def _softplus(x):
    return jnp.maximum(x, 0.0) + jnp.log1p(jnp.exp(-jnp.abs(x)))


def _log_sigmoid(x):
    return jnp.minimum(x, 0.0) - jnp.log1p(jnp.exp(-jnp.abs(x)))


def _gelu_tanh(x):
    return 0.5 * x * (1.0 + jnp.tanh(math.sqrt(2.0 / math.pi) * (x + 0.044715 * (x * x * x))))


def _mod_kernel(c_ref, w_ref, b_ref, o_ref):
    c = c_ref[...]
    sc = c * jax.nn.sigmoid(c)
    o_ref[...] = jnp.dot(sc, w_ref[0], preferred_element_type=F32,
                         precision=lax.Precision.HIGHEST) + b_ref[0]


def _modulation(cc, w_mod, b_mod, l):
    rows, d = cc.shape
    depth, _, n = w_mod.shape
    tn = _pick(n, (1024, 512, 256, 128))
    return pl.pallas_call(
        _mod_kernel,
        grid=(n // tn,),
        in_specs=[pl.BlockSpec((rows, d), lambda j: (0, 0)),
                  pl.BlockSpec((1, d, tn), lambda j: (l, 0, j)),
                  pl.BlockSpec((1, 1, tn), lambda j: (l, 0, j))],
        out_specs=pl.BlockSpec((rows, tn), lambda j: (0, j)),
        out_shape=jax.ShapeDtypeStruct((rows, n), F32),
        compiler_params=_cparams(("parallel",), VMEM_LIMIT),
        name="adaln_mod",
    )(cc, w_mod, b_mod.reshape(depth, 1, n))


def _inproj_kernel(x_ref, mod_ref, g_ref, w_ref, z_ref, zm_ref, xn_ref, *, d, nz):
    @pl.when(pl.program_id(1) == 0)
    def _():
        y = _rms(x_ref[...], g_ref[...])
        shift = mod_ref[0, :, 0:d]
        scale = mod_ref[0, :, d:2 * d]
        xn_ref[...] = (y * (1.0 + scale) + shift).astype(BF16)

    @pl.when(pl.program_id(1) < nz)
    def _():
        z_ref[...] = jnp.dot(xn_ref[...], w_ref[...], preferred_element_type=F32)

    @pl.when(pl.program_id(1) >= nz)
    def _():
        zm_ref[...] = _sigmoid(jnp.dot(xn_ref[...], w_ref[...], preferred_element_type=F32)).astype(BF16)


def _in_projection(r, src, mods3, g, w_in_p, mod_row, span, prev):
    tm, blk0, nblk = span
    xs, src0 = src
    d = xs.shape[1]
    n = w_in_p.shape[1]
    tn = _pick(math.gcd(Z_MERGE, n - Z_MERGE), (1024, 512, 256, 128))
    nz = Z_MERGE // tn
    return _pcall(
        functools.partial(_inproj_kernel, d=d, nz=nz),
        prev=prev,
        grid=(nblk, n // tn),
        in_specs=[pl.BlockSpec((tm, d), lambda i, j: (src0 + i, 0)),
                  pl.BlockSpec((1, 1, N_MOD * d), lambda i, j: (mod_row((blk0 + i) * tm), 0, 0)),
                  pl.BlockSpec((1, d), lambda i, j: (0, 0)),
                  pl.BlockSpec((d, tn), lambda i, j: (0, j))],
        args=[xs, mods3, g.reshape(1, d), w_in_p],
        out_specs=[pl.BlockSpec((tm, tn), lambda i, j: (blk0 + i, jnp.minimum(j, nz - 1))),
                   pl.BlockSpec((tm, tn), lambda i, j: (blk0 + i, jnp.maximum(j - nz, 0)))],
        out_shape=[jax.ShapeDtypeStruct((r, Z_MERGE), F32), jax.ShapeDtypeStruct((r, n - Z_MERGE), BF16)],
        scratch_shapes=[pltpu.VMEM((tm, d), BF16)],
        compiler_params=_cparams(("parallel", "arbitrary"), VMEM_LIMIT),
        name="in_proj",
    )


def _mla_prep_kernel(z_ref, cos_ref, sin_ref, cost_ref, sint_ref, gq_ref, gkv_ref,
                     wqt_ref, wqrt_ref, wk_ref, wvt_ref, qt_ref, k_ref, vt_ref):
    tm = z_ref.shape[0]
    qn = _rms(z_ref[:, Z_Q:Z_Q + Q_LORA], gq_ref[...]).astype(BF16)
    kvn = _rms(z_ref[:, Z_KV:Z_KV + KV_LORA], gkv_ref[...]).astype(BF16)
    kr = (z_ref[:, Z_KR:Z_KR + QK_ROPE] * cos_ref[...]
          + z_ref[:, Z_KRP:Z_KRP + QK_ROPE] * sin_ref[...]).astype(BF16)
    kn = jnp.dot(kvn, wk_ref[...], preferred_element_type=F32)
    nt = (((1,), (1,)), ((), ()))
    ones = jnp.ones((VT_ROWS - V_HEAD, tm), BF16)
    q_all = lax.dot_general(wqt_ref[...], qn, nt, preferred_element_type=F32)
    qr_all = lax.dot_general(wqrt_ref[...], qn, nt, preferred_element_type=F32)
    v_all = lax.dot_general(wvt_ref[...], kvn, nt, preferred_element_type=F32)
    for h in range(HEADS):
        q_t = q_all[h * QK_DIM:(h + 1) * QK_DIM]
        q_rot = q_t[QK_NOPE:QK_DIM] * cost_ref[...] + qr_all[h * QK_ROPE:(h + 1) * QK_ROPE] * sint_ref[...]
        qt_ref[0, h, 0:QK_NOPE, :] = (q_t[0:QK_NOPE] * Q_SCALE).astype(BF16)
        qt_ref[0, h, QK_NOPE:QK_DIM, :] = (q_rot * Q_SCALE).astype(BF16)
        k_ref[0, h, :, 0:QK_NOPE] = kn[:, h * QK_NOPE:(h + 1) * QK_NOPE].astype(BF16)
        k_ref[0, h, :, QK_NOPE:QK_DIM] = kr
        vt_ref[0, h, 0:V_HEAD, :] = v_all[h * V_HEAD:(h + 1) * V_HEAD].astype(BF16)
        vt_ref[0, h, V_HEAD:VT_ROWS, :] = ones


def _mla_prep(z, tabs, gq, gkv, wqt, wqrt, wk, wvt, b, s, lc, tm):
    r = z.shape[0]
    cos_u, sin_u, cost_u, sint_u = tabs
    nxt, nct = s // tm, lc // tm
    ltot = s + lc

    def bidx(i):
        j = i - b * nxt
        return jnp.where(i < b * nxt, i // nxt, j // nct)

    def sblk(i):
        j = i - b * nxt
        return jnp.where(i < b * nxt, i % nxt, nxt + j % nct)

    full = lambda shape: pl.BlockSpec(shape, lambda i: (0,) * len(shape))
    return pl.pallas_call(
        _mla_prep_kernel,
        grid=(r // tm,),
        in_specs=[pl.BlockSpec((tm, Z_MLA_END), lambda i: (i, 0)),
                  pl.BlockSpec((tm, QK_ROPE), lambda i: (i, 0)),
                  pl.BlockSpec((tm, QK_ROPE), lambda i: (i, 0)),
                  pl.BlockSpec((QK_ROPE, tm), lambda i: (0, i)),
                  pl.BlockSpec((QK_ROPE, tm), lambda i: (0, i)),
                  full((1, Q_LORA)), full((1, KV_LORA)),
                  full(wqt.shape), full(wqrt.shape), full(wk.shape), full(wvt.shape)],
        out_specs=[pl.BlockSpec((1, HEADS, QK_DIM, tm), lambda i: (bidx(i), 0, 0, sblk(i))),
                   pl.BlockSpec((1, HEADS, tm, QK_DIM), lambda i: (bidx(i), 0, sblk(i), 0)),
                   pl.BlockSpec((1, HEADS, VT_ROWS, tm), lambda i: (bidx(i), 0, 0, sblk(i)))],
        out_shape=[jax.ShapeDtypeStruct((b, HEADS, QK_DIM, ltot), BF16),
                   jax.ShapeDtypeStruct((b, HEADS, ltot, QK_DIM), BF16),
                   jax.ShapeDtypeStruct((b, HEADS, VT_ROWS, ltot), BF16)],
        compiler_params=_cparams(("parallel",), VMEM_LIMIT),
        name="mla_prep",
    )(z, cos_u, sin_u, cost_u, sint_u, gq.reshape(1, -1), gkv.reshape(1, -1), wqt, wqrt, wk, wvt)


def _attn_kernel(qt_ref, k_ref, vt_ref, o_ref, s_ref, *, plan, run):
    qt = qt_ref[0, 0]
    tq = qt.shape[1]

    def scores(off, size, slot):
        s = jnp.dot(k_ref[0, 0, pl.ds(off, size), :], qt, preferred_element_type=F32)
        s_ref[slot, 0:size, :] = s
        return jnp.max(s, axis=0, keepdims=True)

    def update(off, size, slot, m, acc, cmax):
        m_new = jnp.maximum(m, cmax)
        p = jnp.exp2(s_ref[slot, 0:size, :] - m_new).astype(BF16)
        alpha = jnp.exp2(m - m_new)
        pv = jnp.dot(vt_ref[0, 0, :, pl.ds(off, size)], p, preferred_element_type=F32)
        return m_new, alpha * acc + pv

    n = len(plan)
    a, trips = run
    m = jnp.full((1, tq), -jnp.inf, F32)
    acc = jnp.zeros((VT_ROWS, tq), F32)
    cmax = scores(plan[0][0], plan[0][1], 0)
    f = 0
    while f < n:
        if trips and f == a:
            base, tk = plan[a]

            def body(i, carry, base=base, tk=tk):
                m, acc, cmax0 = carry
                off = pl.multiple_of(base + 2 * i * tk, MXU_DIM)
                cmax1 = scores(off + tk, tk, (a + 1) % 2)
                m, acc = update(off, tk, a % 2, m, acc, cmax0)
                cmax0 = scores(off + 2 * tk, tk, a % 2)
                m, acc = update(off + tk, tk, (a + 1) % 2, m, acc, cmax1)
                return m, acc, cmax0

            m, acc, cmax = lax.fori_loop(0, trips, body, (m, acc, cmax))
            f = a + 2 * trips
        else:
            nxt = scores(plan[f + 1][0], plan[f + 1][1], (f + 1) % 2) if f + 1 < n else None
            m, acc = update(plan[f][0], plan[f][1], f % 2, m, acc, cmax)
            cmax = nxt
            f += 1
    o = acc[0:V_HEAD] / acc[V_HEAD:V_HEAD + 1]
    o_ref[...] = o.T.astype(o_ref.dtype)


def _chunk_plan(lat, ctx_len, tk):
    if lat == 0:
        return ((0, ctx_len),), (0, 0)
    assert lat % MXU_DIM == 0 and ctx_len % MXU_DIM == 0 and tk % MXU_DIM == 0
    tail = MXU_DIM if lat > MXU_DIM else 0
    first = (lat - tail) % tk
    n_uni = (lat - tail) // tk
    plan = [(lat, ctx_len)]
    if first:
        plan.append((0, first))
    a = len(plan)
    plan += [(first + k * tk, tk) for k in range(n_uni)]
    if tail:
        plan.append((lat - tail, tail))
    return tuple(plan), (a, max(n_uni - 1, 0) // 2)


def _attn_call(qt, k, vt, *, b, nq, tq, q_blk0, kv_len, ctx_len, kv_blk0, out_rows, row_blk, prev):
    lat = kv_len - ctx_len
    tk = ATTN_TK if lat >= 4 * ATTN_TK else MXU_DIM
    plan, run = _chunk_plan(lat, ctx_len, tk)
    tk = max(size for _, size in plan)
    return _pcall(
        functools.partial(_attn_kernel, plan=plan, run=run),
        prev=prev,
        grid=(b, HEADS, nq),
        in_specs=[pl.BlockSpec((1, 1, QK_DIM, tq), lambda bb, h, i: (bb, h, 0, q_blk0 + i)),
                  pl.BlockSpec((1, 1, kv_len, QK_DIM), lambda bb, h, i: (bb, h, kv_blk0, 0)),
                  pl.BlockSpec((1, 1, VT_ROWS, kv_len), lambda bb, h, i: (bb, h, 0, kv_blk0))],
        args=[qt, k, vt],
        out_specs=pl.BlockSpec((tq, V_HEAD), lambda bb, h, i: (row_blk(bb, i), h)),
        out_shape=jax.ShapeDtypeStruct((out_rows, WIDTH), BF16),
        scratch_shapes=[pltpu.VMEM((2, tk, tq), F32)],
        compiler_params=_cparams(("parallel", "parallel", "arbitrary"), VMEM_LIMIT),
        name="mla_attn",
    )


def _mlstm_gates_kernel(z_ref, b_ref, gc_ref, gr_ref):
    for c in range(gr_ref.shape[0]):
        rows = slice(c * ML_CHUNK, (c + 1) * ML_CHUNK)
        gc = _mlstm_gates_chunk(z_ref[rows, :] + b_ref[...])
        gc_ref[rows, :] = gc
        gr_ref[c] = gc.T


def _mlstm_gates_chunk(g):
    n = g.shape[0]
    lane = lax.broadcasted_iota(jnp.int32, g.shape, 1)
    row = lax.broadcasted_iota(jnp.int32, g.shape, 0)
    lf = _log_sigmoid(g)
    pre = lf
    suf = lf
    k = 1
    while k < n:
        pre = pre + jnp.where(row >= k, pltpu.roll(pre, k, 0), 0.0)
        suf = suf + jnp.where(row < n - k, pltpu.roll(suf, n - k, 0), 0.0)
        k *= 2
    bwd = lane >= 2 * HEADS
    cum = jnp.where(bwd, suf, pre)
    r = g - pltpu.roll(cum, LANE - HEADS, 1)
    pmax = r
    smax = r
    k = 1
    while k < n:
        pmax = jnp.maximum(pmax, jnp.where(row >= k, pltpu.roll(pmax, k, 0), -jnp.inf))
        smax = jnp.maximum(smax, jnp.where(row < n - k, pltpu.roll(smax, n - k, 0), -jnp.inf))
        k *= 2
    cmax = jnp.where(bwd, smax, pmax)
    is_f = ((lane >= HEADS) & (lane < 2 * HEADS)) | ((lane >= 3 * HEADS) & (lane < 4 * HEADS))
    gc = jnp.where(is_f, cum, g)
    gc = jnp.where((lane >= 4 * HEADS) & (lane < 8 * HEADS), pltpu.roll(cmax, 4 * HEADS, 1), gc)
    return jnp.where((lane >= 8 * HEADS) & (lane < 12 * HEADS), pltpu.roll(r, 8 * HEADS, 1), gc)


def _mlstm_gates(z, bias_pad):
    r = z.shape[0]
    nchunk = r // ML_CHUNK
    per = _pick(nchunk, (4, 2, 1))
    return pl.pallas_call(
        _mlstm_gates_kernel,
        grid=(nchunk // per,),
        in_specs=[pl.BlockSpec((per * ML_CHUNK, LANE), lambda i: (i, Z_G // LANE)),
                  pl.BlockSpec((1, LANE), lambda i: (0, 0))],
        out_specs=[pl.BlockSpec((per * ML_CHUNK, LANE), lambda i: (i, 0)),
                   pl.BlockSpec((per, LANE, ML_CHUNK), lambda i: (i, 0, 0))],
        out_shape=[jax.ShapeDtypeStruct((r, LANE), F32),
                   jax.ShapeDtypeStruct((nchunk, LANE, ML_CHUNK), F32)],
        compiler_params=_cparams(("parallel",)),
        name="mlstm_gates",
    )(z, bias_pad)


def _mlstm_kernel(qf_ref, kf_ref, vf_ref, gcf_ref, grf_ref,
                  qb_ref, kb_ref, vb_ref, gcb_ref, grb_ref,
                  hf_ref, hb_ref, c_ref, m_ref):
    L = ML_CHUNK

    @pl.when(pl.program_id(1) == 0)
    def _():
        c_ref[...] = jnp.zeros(c_ref.shape, F32)
        m_ref[...] = jnp.zeros(m_ref.shape, F32)

    row = lax.broadcasted_iota(jnp.int32, (L, L), 0)
    col = lax.broadcasted_iota(jnp.int32, (L, L), 1)
    nt = (((1,), (1,)), ((), ()))
    ones_rows = jnp.ones((ML_ROWS - ML_DQK, L), BF16)
    dirs = ((qf_ref, kf_ref, vf_ref, gcf_ref, grf_ref, hf_ref, row <= col),
            (qb_ref, kb_ref, vb_ref, gcb_ref, grb_ref, hb_ref, row >= col))
    ln_ks = math.log(ML_DQK ** -0.5)
    st = []
    for d, (q_ref, k_ref, v_ref, gc_ref, gr_ref, o_ref, mask) in enumerate(dirs):
        gc = gc_ref[...]
        gr = gr_ref[0]
        for h in range(HEADS):
            li = 2 * HEADS * d + h
            lb = li + HEADS
            lm = li + 4 * HEADS
            lr = li + 8 * HEADS
            brow = gr[lb:lb + 1, :]
            g_tot = brow[:, L - 1:L] if d == 0 else brow[:, 0:1]
            wend = g_tot - brow + gr[li:li + 1, :]
            a = jnp.max(wend, axis=1, keepdims=True)
            m_old = m_ref[d * HEADS + h][0:1, 0:1]
            urow = jnp.maximum(m_old, gr[lm:lm + 1, :])
            m_new = jnp.maximum(g_tot + m_old, a)
            st.append(dict(
                idx=d * HEADS + h, sl=slice(h * ML_DQK, (h + 1) * ML_DQK),
                q_ref=q_ref, k_ref=k_ref, v_ref=v_ref, o_ref=o_ref, mask=mask,
                e=jnp.exp(wend - a + ln_ks), urow=urow, rcol=gc[:, lr:lr + 1],
                di=jnp.exp(m_old - urow), em=jnp.exp(-(brow + urow)), m_new=m_new,
                dec=jnp.exp(g_tot + m_old - m_new), inp=jnp.exp(a - m_new)))
    for t in st:
        sl = t["sl"]
        q = t["q_ref"][:, sl].astype(BF16)
        k = t["k_ref"][:, sl].astype(BF16)
        v_t = t["v_ref"][:, sl].T
        t["vext"] = jnp.concatenate([v_t.astype(BF16), ones_rows], axis=0)
        vext_e = jnp.concatenate([(v_t * t["e"]).astype(BF16), jnp.broadcast_to(t["e"], ones_rows.shape).astype(BF16)],
                                 axis=0)
        t["sT"] = lax.dot_general(k, q, nt, preferred_element_type=F32)
        t["d_c"] = jnp.dot(vext_e, k, preferred_element_type=F32)
        t["qc"] = lax.dot_general(c_ref[t["idx"]].astype(BF16), q, nt, preferred_element_type=F32)
    for t in st:
        dexp = jnp.exp(jnp.where(t["mask"], t["rcol"] - t["urow"] + ln_ks, -jnp.inf))
        t["p"] = (t["sT"] * dexp).astype(BF16)
    for t in st:
        res = t["di"] * t["qc"] + jnp.dot(t["vext"], t["p"], preferred_element_type=F32)
        den = res[ML_DQK:ML_DQK + 1, :]
        h_t = res[0:ML_DQK, :] / jnp.maximum(jnp.abs(den), t["em"])
        t["o_ref"][:, t["sl"]] = h_t.T
        c_ref[t["idx"]] = t["dec"] * c_ref[t["idx"]] + t["inp"] * t["d_c"]
        m_ref[t["idx"]] = jnp.broadcast_to(t["m_new"], (SUBLANE, LANE))


def _mlstm(z, gc, gr, b, s, lc):
    r = z.shape[0]
    L = ML_CHUNK
    nxc, ncc = s // L, lc // L
    x0 = lambda bb: bb * nxc
    c0 = lambda bb: b * nxc + bb * ncc

    def blk_f(bb, st):
        return jnp.where(st < ncc, c0(bb) + st, x0(bb) + st - ncc)

    def blk_b(bb, st):
        return jnp.where(st < ncc, c0(bb) + ncc - 1 - st, x0(bb) + nxc - 1 - (st - ncc))

    def zspec(blk, cb):
        return pl.BlockSpec((L, WIDTH), lambda bb, st: (blk(bb, st), cb))

    def dir_specs(blk):
        return [zspec(blk, Z_MQ // WIDTH), zspec(blk, Z_MK // WIDTH), zspec(blk, Z_MV // WIDTH),
                pl.BlockSpec((L, LANE), lambda bb, st: (blk(bb, st), 0)),
                pl.BlockSpec((1, LANE, L), lambda bb, st: (blk(bb, st), 0, 0))]

    return pl.pallas_call(
        _mlstm_kernel,
        grid=(b, nxc + ncc),
        in_specs=dir_specs(blk_f) + dir_specs(blk_b),
        out_specs=[pl.BlockSpec((L, WIDTH), lambda bb, st: (blk_f(bb, st), 0)),
                   pl.BlockSpec((L, WIDTH), lambda bb, st: (blk_b(bb, st), 0))],
        out_shape=[jax.ShapeDtypeStruct((r, WIDTH), F32), jax.ShapeDtypeStruct((r, WIDTH), F32)],
        scratch_shapes=[pltpu.VMEM((2 * HEADS, ML_ROWS, ML_DQK), F32),
                        pltpu.VMEM((2 * HEADS, SUBLANE, LANE), F32)],
        compiler_params=_cparams(("arbitrary", "arbitrary"), VMEM_LIMIT),
        name="mlstm",
    )(z, z, z, gc, gr, z, z, z, gc, gr)


def _rg_ab_kernel(cur_ref, prev_ref, next_ref, cw_ref, cb_ref, w_ref, ba_ref, bx_ref, lam_ref,
                  af_ref, bf_ref, ab_ref, bb_ref, xe_ref, *, tm, s_len, c_len, rows_x):
    row0 = pl.program_id(0) * tm
    in_x = row0 < rows_x
    seq = jnp.where(in_x, s_len, c_len)
    off = jnp.where(in_x, row0, row0 - rows_x)
    first = lax.rem(off, seq) == 0
    last = lax.rem(off + tm, seq) == 0
    xe_ref[0:SUBLANE, :] = jnp.where(first, 0.0, prev_ref[...])
    xe_ref[SUBLANE:SUBLANE + tm, :] = cur_ref[...]
    xe_ref[SUBLANE + tm:2 * SUBLANE + tm, :] = jnp.where(last, 0.0, next_ref[...])
    lp = RG_CONV // 2
    xc = cb_ref[...] + cw_ref[0:1, :] * xe_ref[pl.ds(SUBLANE - lp, tm), :]
    for j in range(1, RG_CONV):
        xc = xc + cw_ref[j:j + 1, :] * xe_ref[pl.ds(SUBLANE - lp + j, tm), :]
    outs = ((af_ref, bf_ref), (ab_ref, bb_ref))
    for g in range(RG_BLOCKS):
        sl = slice(g * RG_BW, (g + 1) * RG_BW)
        xg = xc[:, sl]
        o = jnp.dot(xg.astype(BF16), w_ref[g], preferred_element_type=F32)
        for d in range(2):
            r = _sigmoid(o[:, (2 * d) * RG_BW:(2 * d + 1) * RG_BW] + ba_ref[d:d + 1, sl])
            i = _sigmoid(o[:, (2 * d + 1) * RG_BW:(2 * d + 2) * RG_BW] + bx_ref[d:d + 1, sl])
            log_a = (-RG_C) * r * _softplus(-lam_ref[d:d + 1, sl])
            a = jnp.exp(log_a)
            one_m_a2 = -jnp.tanh(log_a) * (a * a + 1.0)
            outs[d][0][:, sl] = a
            outs[d][1][:, sl] = jnp.sqrt(one_m_a2) * (i * xg)


def _rg_coeffs(z, cw, cb, w_rg, ba, bx, lam, b, s, lc, tm):
    r = z.shape[0]
    per = tm // SUBLANE
    nblk8 = r // SUBLANE
    cbk = Z_RX // WIDTH
    full = lambda shape: pl.BlockSpec(shape, lambda i: (0,) * len(shape))
    kernel = functools.partial(_rg_ab_kernel, tm=tm, s_len=s, c_len=lc, rows_x=b * s)
    o_spec = pl.BlockSpec((tm, WIDTH), lambda i: (i, 0))
    o_shape = jax.ShapeDtypeStruct((r, WIDTH), F32)
    return pl.pallas_call(
        kernel,
        grid=(r // tm,),
        in_specs=[pl.BlockSpec((tm, WIDTH), lambda i: (i, cbk)),
                  pl.BlockSpec((SUBLANE, WIDTH), lambda i: (jnp.maximum(i * per - 1, 0), cbk)),
                  pl.BlockSpec((SUBLANE, WIDTH), lambda i: (jnp.minimum((i + 1) * per, nblk8 - 1), cbk)),
                  full((RG_CONV, WIDTH)), full((1, WIDTH)), full(w_rg.shape),
                  full((2, WIDTH)), full((2, WIDTH)), full((2, WIDTH))],
        out_specs=[o_spec] * 4,
        out_shape=[o_shape] * 4,
        scratch_shapes=[pltpu.VMEM((tm + 2 * SUBLANE, WIDTH), F32)],
        compiler_params=_cparams(("parallel",), VMEM_LIMIT),
        name="rglru_coeffs",
    )(z, z, z, cw, cb.reshape(1, WIDTH), w_rg, ba, bx, lam)


def _rg_scan_kernel(af_ref, bf_ref, ab_ref, bb_ref, hf_ref, hb_ref, sf_ref, sb_ref, *, tt):
    @pl.when(pl.program_id(1) == 0)
    def _():
        sf_ref[...] = jnp.zeros(sf_ref.shape, F32)
        sb_ref[...] = jnp.zeros(sb_ref.shape, F32)

    def body(t, carry):
        hf, hb = carry
        hf = af_ref[pl.ds(t, 1), :] * hf + bf_ref[pl.ds(t, 1), :]
        hf_ref[pl.ds(t, 1), :] = hf
        tb = tt - 1 - t
        hb = ab_ref[pl.ds(tb, 1), :] * hb + bb_ref[pl.ds(tb, 1), :]
        hb_ref[pl.ds(tb, 1), :] = hb
        return hf, hb

    hf, hb = lax.fori_loop(0, tt, body, (sf_ref[...], sb_ref[...]), unroll=8)
    sf_ref[...] = hf
    sb_ref[...] = hb


def _rg_scan(af, bf, ab, bb, b, s, lc, tt):
    r = af.shape[0]
    nxt, nct = s // tt, lc // tt
    x0 = lambda bi: bi * nxt
    c0 = lambda bi: b * nxt + bi * nct

    def blk_f(bi, st):
        return jnp.where(st < nct, c0(bi) + st, x0(bi) + st - nct)

    def blk_b(bi, st):
        return jnp.where(st < nct, c0(bi) + nct - 1 - st, x0(bi) + nxt - 1 - (st - nct))

    sf = pl.BlockSpec((tt, WIDTH), lambda bi, st: (blk_f(bi, st), 0))
    sb = pl.BlockSpec((tt, WIDTH), lambda bi, st: (blk_b(bi, st), 0))
    o_shape = jax.ShapeDtypeStruct((r, WIDTH), F32)
    return pl.pallas_call(
        functools.partial(_rg_scan_kernel, tt=tt),
        grid=(b, nxt + nct),
        in_specs=[sf, sf, sb, sb],
        out_specs=[sf, sb],
        out_shape=[o_shape, o_shape],
        scratch_shapes=[pltpu.VMEM((1, WIDTH), F32), pltpu.VMEM((1, WIDTH), F32)],
        compiler_params=_cparams(("arbitrary", "arbitrary")),
        name="rglru_scan",
    )(af, bf, ab, bb)


def _finish_kernel(mhf_ref, mhb_ref, zo_ref, ng_ref, rhf_ref, rhb_ref, zg_ref, yb_ref, yr_ref):
    hsum = mhf_ref[...] + mhb_ref[...]
    for h in range(HEADS):
        sl = slice(h * V_HEAD, (h + 1) * V_HEAD)
        hn = _rms(hsum[:, sl], ng_ref[:, sl])
        yb_ref[:, sl] = (hn * _sigmoid(zo_ref[:, sl])).astype(BF16)
    yr_ref[...] = ((rhf_ref[...] + rhb_ref[...]) * _gelu_tanh(zg_ref[...])).astype(BF16)


def _finish(rows, mhf, mhb, z, ng, rhf, rhb, tm):
    wide = lambda: pl.BlockSpec((tm, WIDTH), lambda i: (i, 0))
    zblk = lambda cb: pl.BlockSpec((tm, WIDTH), lambda i: (i, cb))
    o_shape = jax.ShapeDtypeStruct((rows, WIDTH), BF16)
    return pl.pallas_call(
        _finish_kernel,
        grid=(rows // tm,),
        in_specs=[wide(), wide(), zblk(Z_MO // WIDTH), pl.BlockSpec((1, WIDTH), lambda i: (0, 0)),
                  wide(), wide(), zblk(Z_RG // WIDTH)],
        out_specs=[wide(), wide()],
        out_shape=[o_shape, o_shape],
        compiler_params=_cparams(("parallel",), VMEM_LIMIT),
        name="branch_finish",
    )(mhf, mhb, z, ng.reshape(1, WIDTH), rhf, rhb, z)


def _merge_kernel(ya_ref, yb_ref, yr_ref, gm0_ref, gm1_ref, gm2_ref, w0_ref, w1_ref, w2_ref, m_ref):
    m = (gm0_ref[...].astype(F32) * jnp.dot(ya_ref[...], w0_ref[0], preferred_element_type=F32)
         + gm1_ref[...].astype(F32) * jnp.dot(yb_ref[...], w1_ref[0], preferred_element_type=F32)
         + gm2_ref[...].astype(F32) * jnp.dot(yr_ref[...], w2_ref[0], preferred_element_type=F32))
    m_ref[...] = m.astype(BF16)


def _merge(rows, d, ya, yb, yr, gm, wb, tm):
    tn = _pick(d, (1024, 512, 256, 128))
    wide = lambda: pl.BlockSpec((tm, WIDTH), lambda j, i: (i, 0))
    gate = lambda br: pl.BlockSpec((tm, tn), lambda j, i: (i, (br * d) // tn + j))
    wspec = lambda br: pl.BlockSpec((1, WIDTH, tn), lambda j, i: (br, 0, j))
    return pl.pallas_call(
        _merge_kernel,
        grid=(d // tn, rows // tm),
        in_specs=[wide(), wide(), wide(), gate(0), gate(1), gate(2), wspec(0), wspec(1), wspec(2)],
        out_specs=pl.BlockSpec((tm, tn), lambda j, i: (i, j)),
        out_shape=jax.ShapeDtypeStruct((rows, d), BF16),
        compiler_params=_cparams(("parallel", "parallel"), VMEM_LIMIT),
        name="branch_merge",
    )(ya, yb, yr, gm, gm, gm, wb, wb, wb)


def _outproj_kernel(m_ref, w_ref, x_ref, gate_ref, o_ref):
    o_ref[...] = x_ref[...] + gate_ref[0] * jnp.dot(m_ref[...], w_ref[...], preferred_element_type=F32)


def _out_projection(rows, src, m, w_out, mods3, mod_row, span, prev):
    tm, blk0, nblk = span
    xs, src0 = src
    d = xs.shape[1]
    tn = _pick(d, (1024, 512, 256, 128))
    return _pcall(
        _outproj_kernel,
        prev=prev,
        grid=(nblk, d // tn),
        in_specs=[pl.BlockSpec((tm, d), lambda i, j: (blk0 + i, 0)),
                  pl.BlockSpec((d, tn), lambda i, j: (0, j)),
                  pl.BlockSpec((tm, tn), lambda i, j: (src0 + i, j)),
                  pl.BlockSpec((1, 1, tn), lambda i, j: (mod_row((blk0 + i) * tm), 0, (2 * d) // tn + j))],
        args=[m, w_out, xs, mods3],
        out_specs=pl.BlockSpec((tm, tn), lambda i, j: (blk0 + i, j)),
        out_shape=jax.ShapeDtypeStruct((rows, d), F32),
        compiler_params=_cparams(("parallel", "parallel"), VMEM_LIMIT),
        name="out_proj",
    )


def _mlp_kernel(x_ref, mod_ref, g_ref, w1_ref, w2_ref, fg_ref, o_ref, xn_ref, *, d, final):
    j = pl.program_id(1)

    @pl.when(j == 0)
    def _():
        y = _rms(x_ref[...], g_ref[...])
        shift = mod_ref[0, :, 3 * d:4 * d]
        scale = mod_ref[0, :, 4 * d:5 * d]
        xn_ref[...] = (y * (1.0 + scale) + shift).astype(BF16)
        o_ref[...] = jnp.zeros(o_ref.shape, F32)

    h = jnp.dot(xn_ref[...], w1_ref[...], preferred_element_type=F32)
    h = jnp.square(jnp.maximum(h, 0.0)).astype(BF16)
    o_ref[...] += jnp.dot(h, w2_ref[...], preferred_element_type=F32)

    @pl.when(j == pl.num_programs(1) - 1)
    def _():
        out = x_ref[...] + mod_ref[0, :, 5 * d:6 * d] * o_ref[...]
        if final:
            out = _rms(out, fg_ref[...])
        o_ref[...] = out


def _mlp(rows, x1, mods3, g, w1, w2, fg, mod_row, span, prev, final):
    tm, blk0, nblk = span
    d = x1.shape[1]
    dff = w1.shape[1]
    tf = _pick(dff, (512, 256, 128))
    return _pcall(
        functools.partial(_mlp_kernel, d=d, final=final),
        prev=prev,
        grid=(nblk, dff // tf),
        in_specs=[pl.BlockSpec((tm, d), lambda i, j: (blk0 + i, 0)),
                  pl.BlockSpec((1, 1, N_MOD * d), lambda i, j: (mod_row((blk0 + i) * tm), 0, 0)),
                  pl.BlockSpec((1, d), lambda i, j: (0, 0)),
                  pl.BlockSpec((d, tf), lambda i, j: (0, j)),
                  pl.BlockSpec((tf, d), lambda i, j: (j, 0)),
                  pl.BlockSpec((1, d), lambda i, j: (0, 0))],
        args=[x1, mods3, g.reshape(1, d), w1, w2, fg.reshape(1, d)],
        out_specs=pl.BlockSpec((tm, d), lambda i, j: (blk0 + i, 0)),
        out_shape=jax.ShapeDtypeStruct((rows, d), F32),
        scratch_shapes=[pltpu.VMEM((tm, d), BF16)],
        compiler_params=_cparams(("parallel", "arbitrary"), VMEM_LIMIT),
        name="mlp",
    )


def _rope_perm_cols(w):
    n = QK_ROPE // 4
    return jnp.concatenate([-w[..., n:2 * n], w[..., 0:n], -w[..., 3 * n:4 * n], w[..., 2 * n:3 * n]], axis=-1)


def _layer_weights(l, d, w_in, w_uq, w_ukv, mlstm_gate_b, rg_wa, rg_wx, w_branch, w_out, w_mlp1, w_mlp2):
    wi = w_in[l].astype(BF16)
    o = 0
    parts = {}
    for name, wdt in (("mla_q", Q_LORA), ("mla_kv", KV_LORA), ("mla_kr", QK_ROPE), ("ml_q", WIDTH),
                      ("ml_k", WIDTH), ("ml_v", WIDTH), ("ml_o", WIDTH), ("ml_g", 4 * HEADS),
                      ("rg_x", WIDTH), ("rg_gate", WIDTH), ("merge", 3 * d)):
        parts[name] = wi[:, o:o + wdt]
        o += wdt
    g_pad = jnp.pad(parts["ml_g"], ((0, 0), (0, LANE - 4 * HEADS)))
    w_in_p = jnp.concatenate(
        [parts["mla_q"], parts["mla_kv"], parts["mla_kr"], _rope_perm_cols(parts["mla_kr"]), g_pad,
         parts["ml_q"], parts["ml_k"], parts["ml_v"], parts["ml_o"], parts["rg_x"], parts["rg_gate"],
         parts["merge"]], axis=1)
    wq = w_uq[l].reshape(Q_LORA, HEADS, QK_DIM)
    wqt = jnp.transpose(wq, (1, 2, 0)).reshape(HEADS * QK_DIM, Q_LORA).astype(BF16)
    wqrt = jnp.transpose(_rope_perm_cols(wq[:, :, QK_NOPE:]), (1, 2, 0)).reshape(HEADS * QK_ROPE, Q_LORA).astype(BF16)
    wkv = w_ukv[l].reshape(KV_LORA, HEADS, QK_NOPE + V_HEAD)
    wk = wkv[:, :, :QK_NOPE].reshape(KV_LORA, HEADS * QK_NOPE).astype(BF16)
    wvt = jnp.transpose(wkv[:, :, QK_NOPE:], (1, 2, 0)).reshape(HEADS * V_HEAD, KV_LORA).astype(BF16)
    gate_b = jnp.pad(mlstm_gate_b[l].reshape(1, 4 * HEADS), ((0, 0), (0, LANE - 4 * HEADS)))
    w_rg = jnp.concatenate([rg_wa[l, 0], rg_wx[l, 0], rg_wa[l, 1], rg_wx[l, 1]], axis=-1).astype(BF16)
    return dict(w_in=w_in_p, wqt=wqt, wqrt=wqrt, wk=wk, wvt=wvt, gate_b=gate_b, w_rg=w_rg,
                wb=w_branch[l].astype(BF16), w_out=w_out[l].astype(BF16),
                w1=w_mlp1[l].astype(BF16), w2=w_mlp2[l].astype(BF16))


def _rope_tables(b, s, lc):
    t = jnp.arange(s, dtype=jnp.int32)
    row = (t // GRID_W).astype(F32)
    col = (t % GRID_W).astype(F32)
    n_freq = QK_ROPE // 4
    inv = ROPE_BASE ** (-jnp.arange(n_freq, dtype=F32) / n_freq)
    ang_r = row[:, None] * inv[None, :]
    ang_c = col[:, None] * inv[None, :]
    cos = jnp.concatenate([jnp.cos(ang_r)] * 2 + [jnp.cos(ang_c)] * 2, axis=1)
    sin = jnp.concatenate([jnp.sin(ang_r)] * 2 + [jnp.sin(ang_c)] * 2, axis=1)
    cos_u = jnp.concatenate([jnp.tile(cos, (b, 1)), jnp.ones((b * lc, QK_ROPE), F32)], axis=0)
    sin_u = jnp.concatenate([jnp.tile(sin, (b, 1)), jnp.zeros((b * lc, QK_ROPE), F32)], axis=0)
    return cos_u, sin_u, cos_u.T, sin_u.T


def kernel(x, c, ctx, c_ctx, norm1_g, norm2_g, w_mod, b_mod, w_in, q_norm_g, w_uq, kv_norm_g, w_ukv,
           mlstm_gate_b, mlstm_norm_g, rg_conv_w, rg_conv_b, rg_wa, rg_ba, rg_wx, rg_bx, rg_lam,
           w_branch, w_out, w_mlp1, w_mlp2, final_g):
    b, s, d = x.shape
    lc = ctx.shape[1]
    depth = w_in.shape[0]
    rows_x, rows_c = b * s, b * lc
    r = rows_x + rows_c

    tm_x = _pick(s, (1024, 512, 256, 128))
    tm_c = _pick(math.gcd(rows_x, rows_c), (512, 256, 128))
    tm_mid = _pick(math.gcd(rows_x, rows_c), (512, 256, 128))
    tm_seq = _pick(math.gcd(s, lc), (256, 128))
    tq = _pick(s, (1024, 512, 256, 128))
    span_x = (tm_x, 0, rows_x // tm_x)
    span_c = (tm_c, rows_x // tm_c, rows_c // tm_c)

    def mod_row(row0):
        return jnp.where(row0 < rows_x, 1 + row0 // s, 0)

    def dense(fn, with_ctx):
        out = fn(span_x, src_x, None)
        return fn(span_c, src_c, out) if with_ctx else out

    src_x = (x.reshape(rows_x, d), 0)
    src_c = (ctx.reshape(rows_c, d), 0)
    cc = jnp.concatenate([c_ctx[None, :], c, jnp.zeros((SUBLANE - 1 - b, d), F32)], axis=0)
    tabs = _rope_tables(b, s, lc)

    for l in range(depth):
        last = l == depth - 1
        w = _layer_weights(l, d, w_in, w_uq, w_ukv, mlstm_gate_b, rg_wa, rg_wx, w_branch, w_out,
                           w_mlp1, w_mlp2)
        mods3 = _modulation(cc, w_mod, b_mod, l).reshape(SUBLANE, 1, N_MOD * d)
        z, gm = dense(lambda span, src, prev: _in_projection(r, src, mods3, norm1_g[l], w["w_in"], mod_row,
                                                             span, prev), True)

        qt, kk, vt = _mla_prep(z, tabs, q_norm_g[l], kv_norm_g[l], w["wqt"], w["wqrt"], w["wk"],
                               w["wvt"], b, s, lc, tm_seq)
        ya = _attn_call(qt, kk, vt, b=b, nq=s // tq, tq=tq, q_blk0=0, kv_len=s + lc, ctx_len=lc, kv_blk0=0,
                        out_rows=r, row_blk=lambda bb, i: bb * (s // tq) + i, prev=None)
        if not last:
            ya = _attn_call(qt, kk, vt, b=b, nq=1, tq=lc, q_blk0=s // lc, kv_len=lc, ctx_len=lc, kv_blk0=s // lc,
                            out_rows=r, row_blk=lambda bb, i: rows_x // lc + bb, prev=ya)

        gc, gr = _mlstm_gates(z, w["gate_b"])
        mhf, mhb = _mlstm(z, gc, gr, b, s, lc)

        af, bf, ab, bb_ = _rg_coeffs(z, rg_conv_w[l], rg_conv_b[l], w["w_rg"], rg_ba[l], rg_bx[l],
                                     rg_lam[l], b, s, lc, tm_seq)
        rhf, rhb = _rg_scan(af, bf, ab, bb_, b, s, lc, tm_seq)

        rows = rows_x if last else r
        yb, yr = _finish(rows, mhf, mhb, z, mlstm_norm_g[l], rhf, rhb, tm_mid)
        m = _merge(rows, d, ya, yb, yr, gm, w["wb"], tm_mid)
        x1 = dense(lambda span, src, prev: _out_projection(rows, src, m, w["w_out"], mods3, mod_row, span,
                                                           prev), not last)
        xu = dense(lambda span, src, prev: _mlp(rows, x1, mods3, norm2_g[l], w["w1"], w["w2"], final_g,
                                                mod_row, span, prev, last), not last)
        src_x = (xu, span_x[1])
        src_c = (xu, span_c[1])

    return xu.reshape(b, s, d)
```

```python
import functools
import math

import jax
import jax.numpy as jnp
from jax import lax
from jax.experimental import pallas as pl
from jax.experimental.pallas import tpu as pltpu

F32 = jnp.float32
BF16 = jnp.bfloat16

HEADS = 8
Q_LORA = 512
KV_LORA = 256
QK_NOPE = 128
QK_ROPE = 64
V_HEAD = 128
QK_DIM = QK_NOPE + QK_ROPE
ROPE_BASE = 10000.0
GRID_W = 64
ML_DQK = 128
ML_CHUNK = 128
WIDTH = 1024
RG_BLOCKS = 8
RG_BW = WIDTH // RG_BLOCKS
RG_CONV = 4
RG_C = 8.0
N_MOD = 6
EPS = 1e-6

LANE = 128
SUBLANE = 8
BF16_ROWS = 16
MXU_DIM = 256
ATTN_TK = 5 * MXU_DIM
VT_ROWS = V_HEAD
ML_ROWS = ML_DQK + BF16_ROWS

Z_Q = 0
Z_KV = Z_Q + Q_LORA
Z_KR = Z_KV + KV_LORA
Z_KRP = Z_KR + QK_ROPE
Z_MLA_END = Z_KRP + QK_ROPE
Z_G = Z_MLA_END
Z_MQ = Z_G + LANE
Z_MK = Z_MQ + WIDTH
Z_MV = Z_MK + WIDTH
Z_MO = Z_MV + WIDTH
Z_RX = Z_MO + WIDTH
Z_RG = Z_RX + WIDTH
Z_MERGE = Z_RG + WIDTH

Q_SCALE = (QK_DIM ** -0.5) * math.log2(math.e)
VMEM_LIMIT = 56 * 1024 * 1024


def _cparams(sem, vmem=None):
    return pltpu.CompilerParams(dimension_semantics=sem, vmem_limit_bytes=vmem)


def _pick(n, cands):
    for c in cands:
        if n % c == 0:
            return c
    raise ValueError(f"no tile in {cands} divides {n}")


def _pcall(kernel, *, prev, in_specs, args, **kw):
    if prev is None:
        return pl.pallas_call(kernel, in_specs=in_specs, **kw)(*args)
    prevs = list(prev) if isinstance(prev, (tuple, list)) else [prev]
    n_in, n_prev = len(args), len(prevs)

    def aliased(*refs):
        kernel(*refs[:n_in], *refs[n_in + n_prev:])

    return pl.pallas_call(aliased, in_specs=in_specs + [pl.BlockSpec(memory_space=pl.ANY)] * n_prev,
                          input_output_aliases={n_in + k: k for k in range(n_prev)}, **kw)(*args, *prevs)


def _rms(x, g):
    return x * lax.rsqrt(jnp.mean(x * x, axis=-1, keepdims=True) + EPS) * g


def _sigmoid(x):
    return 0.5 * jnp.tanh(0.5 * x) + 0.5


def _softplus(x):
    return jnp.maximum(x, 0.0) + jnp.log1p(jnp.exp(-jnp.abs(x)))


def _log_sigmoid(x):
    return jnp.minimum(x, 0.0) - jnp.log1p(jnp.exp(-jnp.abs(x)))


def _gelu_tanh(x):
    return 0.5 * x * (1.0 + jnp.tanh(math.sqrt(2.0 / math.pi) * (x + 0.044715 * (x * x * x))))


def _mod_kernel(c_ref, w_ref, b_ref, o_ref):
    c = c_ref[...]
    sc = c * jax.nn.sigmoid(c)
    o_ref[...] = jnp.dot(sc, w_ref[0], preferred_element_type=F32,
                         precision=lax.Precision.HIGHEST) + b_ref[0]


def _modulation(cc, w_mod, b_mod, l):
    rows, d = cc.shape
    depth, _, n = w_mod.shape
    tn = _pick(n, (1024, 512, 256, 128))
    return pl.pallas_call(
        _mod_kernel,
        grid=(n // tn,),
        in_specs=[pl.BlockSpec((rows, d), lambda j: (0, 0)),
                  pl.BlockSpec((1, d, tn), lambda j: (l, 0, j)),
                  pl.BlockSpec((1, 1, tn), lambda j: (l, 0, j))],
        out_specs=pl.BlockSpec((rows, tn), lambda j: (0, j)),
        out_shape=jax.ShapeDtypeStruct((rows, n), F32),
        compiler_params=_cparams(("parallel",), VMEM_LIMIT),
        name="adaln_mod",
    )(cc, w_mod, b_mod.reshape(depth, 1, n))


def _inproj_kernel(x_ref, mod_ref, g_ref, w_ref, z_ref, zm_ref, xn_ref, *, d, nz):
    @pl.when(pl.program_id(1) == 0)
    def _():
        y = _rms(x_ref[...], g_ref[...])
        shift = mod_ref[0, :, 0:d]
        scale = mod_ref[0, :, d:2 * d]
        xn_ref[...] = (y * (1.0 + scale) + shift).astype(BF16)

    @pl.when(pl.program_id(1) < nz)
    def _():
        z_ref[...] = jnp.dot(xn_ref[...], w_ref[...], preferred_element_type=F32)

    @pl.when(pl.program_id(1) >= nz)
    def _():
        zm_ref[...] = _sigmoid(jnp.dot(xn_ref[...], w_ref[...], preferred_element_type=F32)).astype(BF16)


def _in_projection(r, src, mods3, g, w_in_p, mod_row, span, prev):
    tm, blk0, nblk = span
    xs, src0 = src
    d = xs.shape[1]
    n = w_in_p.shape[1]
    tn = _pick(math.gcd(Z_MERGE, n - Z_MERGE), (1024, 512, 256, 128))
    nz = Z_MERGE // tn
    return _pcall(
        functools.partial(_inproj_kernel, d=d, nz=nz),
        prev=prev,
        grid=(nblk, n // tn),
        in_specs=[pl.BlockSpec((tm, d), lambda i, j: (src0 + i, 0)),
                  pl.BlockSpec((1, 1, N_MOD * d), lambda i, j: (mod_row((blk0 + i) * tm), 0, 0)),
                  pl.BlockSpec((1, d), lambda i, j: (0, 0)),
                  pl.BlockSpec((d, tn), lambda i, j: (0, j))],
        args=[xs, mods3, g.reshape(1, d), w_in_p],
        out_specs=[pl.BlockSpec((tm, tn), lambda i, j: (blk0 + i, jnp.minimum(j, nz - 1))),
                   pl.BlockSpec((tm, tn), lambda i, j: (blk0 + i, jnp.maximum(j - nz, 0)))],
        out_shape=[jax.ShapeDtypeStruct((r, Z_MERGE), F32), jax.ShapeDtypeStruct((r, n - Z_MERGE), BF16)],
        scratch_shapes=[pltpu.VMEM((tm, d), BF16)],
        compiler_params=_cparams(("parallel", "arbitrary"), VMEM_LIMIT),
        name="in_proj",
    )


def _mla_prep_kernel(z_ref, cos_ref, sin_ref, cost_ref, sint_ref, gq_ref, gkv_ref,
                     wqt_ref, wqrt_ref, wk_ref, wvt_ref, qt_ref, k_ref, vt_ref):
    qn = _rms(z_ref[:, Z_Q:Z_Q + Q_LORA], gq_ref[...]).astype(BF16)
    kvn = _rms(z_ref[:, Z_KV:Z_KV + KV_LORA], gkv_ref[...]).astype(BF16)
    kr = (z_ref[:, Z_KR:Z_KR + QK_ROPE] * cos_ref[...]
          + z_ref[:, Z_KRP:Z_KRP + QK_ROPE] * sin_ref[...]).astype(BF16)
    kn = jnp.dot(kvn, wk_ref[...], preferred_element_type=F32)
    nt = (((1,), (1,)), ((), ()))
    q_all = lax.dot_general(wqt_ref[...], qn, nt, preferred_element_type=F32)
    qr_all = lax.dot_general(wqrt_ref[...], qn, nt, preferred_element_type=F32)
    v_all = lax.dot_general(wvt_ref[...], kvn, nt, preferred_element_type=F32)
    for h in range(HEADS):
        q_t = q_all[h * QK_DIM:(h + 1) * QK_DIM]
        q_rot = q_t[QK_NOPE:QK_DIM] * cost_ref[...] + qr_all[h * QK_ROPE:(h + 1) * QK_ROPE] * sint_ref[...]
        qt_ref[0, h, 0:QK_NOPE, :] = (q_t[0:QK_NOPE] * Q_SCALE).astype(BF16)
        qt_ref[0, h, QK_NOPE:QK_DIM, :] = (q_rot * Q_SCALE).astype(BF16)
        k_ref[0, h, :, 0:QK_NOPE] = kn[:, h * QK_NOPE:(h + 1) * QK_NOPE].astype(BF16)
        k_ref[0, h, :, QK_NOPE:QK_DIM] = kr
        vt_ref[0, h] = v_all[h * V_HEAD:(h + 1) * V_HEAD].astype(BF16)


def _mla_prep(z, tabs, gq, gkv, wqt, wqrt, wk, wvt, b, s, lc, tm):
    r = z.shape[0]
    cos_u, sin_u, cost_u, sint_u = tabs
    nxt, nct = s // tm, lc // tm
    ltot = s + lc

    def bidx(i):
        j = i - b * nxt
        return jnp.where(i < b * nxt, i // nxt, j // nct)

    def sblk(i):
        j = i - b * nxt
        return jnp.where(i < b * nxt, i % nxt, nxt + j % nct)

    full = lambda shape: pl.BlockSpec(shape, lambda i: (0,) * len(shape))
    return pl.pallas_call(
        _mla_prep_kernel,
        grid=(r // tm,),
        in_specs=[pl.BlockSpec((tm, Z_MLA_END), lambda i: (i, 0)),
                  pl.BlockSpec((tm, QK_ROPE), lambda i: (i, 0)),
                  pl.BlockSpec((tm, QK_ROPE), lambda i: (i, 0)),
                  pl.BlockSpec((QK_ROPE, tm), lambda i: (0, i)),
                  pl.BlockSpec((QK_ROPE, tm), lambda i: (0, i)),
                  full((1, Q_LORA)), full((1, KV_LORA)),
                  full(wqt.shape), full(wqrt.shape), full(wk.shape), full(wvt.shape)],
        out_specs=[pl.BlockSpec((1, HEADS, QK_DIM, tm), lambda i: (bidx(i), 0, 0, sblk(i))),
                   pl.BlockSpec((1, HEADS, tm, QK_DIM), lambda i: (bidx(i), 0, sblk(i), 0)),
                   pl.BlockSpec((1, HEADS, VT_ROWS, tm), lambda i: (bidx(i), 0, 0, sblk(i)))],
        out_shape=[jax.ShapeDtypeStruct((b, HEADS, QK_DIM, ltot), BF16),
                   jax.ShapeDtypeStruct((b, HEADS, ltot, QK_DIM), BF16),
                   jax.ShapeDtypeStruct((b, HEADS, VT_ROWS, ltot), BF16)],
        compiler_params=_cparams(("parallel",), VMEM_LIMIT),
        name="mla_prep",
    )(z, cos_u, sin_u, cost_u, sint_u, gq.reshape(1, -1), gkv.reshape(1, -1), wqt, wqrt, wk, wvt)


def _attn_kernel(qt_ref, k_ref, vt_ref, o_ref, s_ref, *, plan, run):
    qt = qt_ref[0, 0]
    tq = qt.shape[1]

    def scores(off, size, slot):
        s = jnp.dot(k_ref[0, 0, pl.ds(off, size), :], qt, preferred_element_type=F32)
        s_ref[slot, 0:size, :] = s
        return jnp.max(s, axis=0, keepdims=True)

    def update(off, size, slot, m, acc, cmax):
        m_new = jnp.maximum(m, cmax)
        p = jnp.exp2(s_ref[slot, 0:size, :] - m_new)
        alpha = jnp.exp2(m - m_new)
        pv = jnp.dot(vt_ref[0, 0, :, pl.ds(off, size)], p.astype(BF16), preferred_element_type=F32)
        return m_new, (alpha * acc[0] + pv, alpha * acc[1] + jnp.sum(p, axis=0, keepdims=True))

    n = len(plan)
    a, trips = run
    m = jnp.full((1, tq), -jnp.inf, F32)
    acc = (jnp.zeros((V_HEAD, tq), F32), jnp.zeros((1, tq), F32))
    cmax = scores(plan[0][0], plan[0][1], 0)
    f = 0
    while f < n:
        if trips and f == a:
            base, tk = plan[a]

            def body(i, carry, base=base, tk=tk):
                m, acc, cmax0 = carry
                off = pl.multiple_of(base + 2 * i * tk, MXU_DIM)
                cmax1 = scores(off + tk, tk, (a + 1) % 2)
                m, acc = update(off, tk, a % 2, m, acc, cmax0)
                cmax0 = scores(off + 2 * tk, tk, a % 2)
                m, acc = update(off + tk, tk, (a + 1) % 2, m, acc, cmax1)
                return m, acc, cmax0

            m, acc, cmax = lax.fori_loop(0, trips, body, (m, acc, cmax))
            f = a + 2 * trips
        else:
            nxt = scores(plan[f + 1][0], plan[f + 1][1], (f + 1) % 2) if f + 1 < n else None
            m, acc = update(plan[f][0], plan[f][1], f % 2, m, acc, cmax)
            cmax = nxt
            f += 1
    o = acc[0] / acc[1]
    o_ref[...] = o.T.astype(o_ref.dtype)


def _chunk_plan(lat, ctx_len, tk):
    if lat == 0:
        return ((0, ctx_len),), (0, 0)
    assert lat % MXU_DIM == 0 and ctx_len % MXU_DIM == 0 and tk % MXU_DIM == 0
    tail = MXU_DIM if lat > MXU_DIM else 0
    first = (lat - tail) % tk
    n_uni = (lat - tail) // tk
    plan = [(lat, ctx_len)]
    if first:
        plan.append((0, first))
    a = len(plan)
    plan += [(first + k * tk, tk) for k in range(n_uni)]
    if tail:
        plan.append((lat - tail, tail))
    return tuple(plan), (a, max(n_uni - 1, 0) // 2)


def _attn_call(qt, k, vt, *, b, nq, tq, q_blk0, kv_len, ctx_len, kv_blk0, out_rows, row_blk, prev):
    lat = kv_len - ctx_len
    tk = ATTN_TK if lat >= 4 * ATTN_TK else MXU_DIM
    plan, run = _chunk_plan(lat, ctx_len, tk)
    tk = max(size for _, size in plan)
    return _pcall(
        functools.partial(_attn_kernel, plan=plan, run=run),
        prev=prev,
        grid=(b, HEADS, nq),
        in_specs=[pl.BlockSpec((1, 1, QK_DIM, tq), lambda bb, h, i: (bb, h, 0, q_blk0 + i)),
                  pl.BlockSpec((1, 1, kv_len, QK_DIM), lambda bb, h, i: (bb, h, kv_blk0, 0)),
                  pl.BlockSpec((1, 1, VT_ROWS, kv_len), lambda bb, h, i: (bb, h, 0, kv_blk0))],
        args=[qt, k, vt],
        out_specs=pl.BlockSpec((tq, V_HEAD), lambda bb, h, i: (row_blk(bb, i), h)),
        out_shape=jax.ShapeDtypeStruct((out_rows, WIDTH), BF16),
        scratch_shapes=[pltpu.VMEM((2, tk, tq), F32)],
        compiler_params=_cparams(("parallel", "parallel", "arbitrary"), VMEM_LIMIT),
        name="mla_attn",
    )


def _mlstm_gates_kernel(z_ref, b_ref, gc_ref, gr_ref):
    for c in range(gr_ref.shape[0]):
        rows = slice(c * ML_CHUNK, (c + 1) * ML_CHUNK)
        gc = _mlstm_gates_chunk(z_ref[rows, :] + b_ref[...])
        gc_ref[rows, :] = gc
        gr_ref[c] = gc.T


def _mlstm_gates_chunk(g):
    n = g.shape[0]
    lane = lax.broadcasted_iota(jnp.int32, g.shape, 1)
    row = lax.broadcasted_iota(jnp.int32, g.shape, 0)
    lf = _log_sigmoid(g)
    pre = lf
    suf = lf
    k = 1
    while k < n:
        pre = pre + jnp.where(row >= k, pltpu.roll(pre, k, 0), 0.0)
        suf = suf + jnp.where(row < n - k, pltpu.roll(suf, n - k, 0), 0.0)
        k *= 2
    bwd = lane >= 2 * HEADS
    cum = jnp.where(bwd, suf, pre)
    r = g - pltpu.roll(cum, LANE - HEADS, 1)
    pmax = r
    smax = r
    k = 1
    while k < n:
        pmax = jnp.maximum(pmax, jnp.where(row >= k, pltpu.roll(pmax, k, 0), -jnp.inf))
        smax = jnp.maximum(smax, jnp.where(row < n - k, pltpu.roll(smax, n - k, 0), -jnp.inf))
        k *= 2
    cmax = jnp.where(bwd, smax, pmax)
    is_f = ((lane >= HEADS) & (lane < 2 * HEADS)) | ((lane >= 3 * HEADS) & (lane < 4 * HEADS))
    gc = jnp.where(is_f, cum, g)
    gc = jnp.where((lane >= 4 * HEADS) & (lane < 8 * HEADS), pltpu.roll(cmax, 4 * HEADS, 1), gc)
    return jnp.where((lane >= 8 * HEADS) & (lane < 12 * HEADS), pltpu.roll(r, 8 * HEADS, 1), gc)


def _mlstm_gates(z, bias_pad):
    r = z.shape[0]
    nchunk = r // ML_CHUNK
    per = _pick(nchunk, (4, 2, 1))
    return pl.pallas_call(
        _mlstm_gates_kernel,
        grid=(nchunk // per,),
        in_specs=[pl.BlockSpec((per * ML_CHUNK, LANE), lambda i: (i, Z_G // LANE)),
                  pl.BlockSpec((1, LANE), lambda i: (0, 0))],
        out_specs=[pl.BlockSpec((per * ML_CHUNK, LANE), lambda i: (i, 0)),
                   pl.BlockSpec((per, LANE, ML_CHUNK), lambda i: (i, 0, 0))],
        out_shape=[jax.ShapeDtypeStruct((r, LANE), F32),
                   jax.ShapeDtypeStruct((nchunk, LANE, ML_CHUNK), F32)],
        compiler_params=_cparams(("parallel",)),
        name="mlstm_gates",
    )(z, bias_pad)


def _mlstm_kernel(qf_ref, kf_ref, vf_ref, gcf_ref, grf_ref,
                  qb_ref, kb_ref, vb_ref, gcb_ref, grb_ref,
                  hf_ref, hb_ref, c_ref, m_ref):
    L = ML_CHUNK

    @pl.when(pl.program_id(1) == 0)
    def _():
        c_ref[...] = jnp.zeros(c_ref.shape, F32)
        m_ref[...] = jnp.zeros(m_ref.shape, F32)

    row = lax.broadcasted_iota(jnp.int32, (L, L), 0)
    col = lax.broadcasted_iota(jnp.int32, (L, L), 1)
    nt = (((1,), (1,)), ((), ()))
    ones_rows = jnp.ones((ML_ROWS - ML_DQK, L), BF16)
    dirs = ((qf_ref, kf_ref, vf_ref, gcf_ref, grf_ref, hf_ref, row <= col),
            (qb_ref, kb_ref, vb_ref, gcb_ref, grb_ref, hb_ref, row >= col))
    ln_ks = math.log(ML_DQK ** -0.5)
    st = []
    for d, (q_ref, k_ref, v_ref, gc_ref, gr_ref, o_ref, mask) in enumerate(dirs):
        gc = gc_ref[...]
        gr = gr_ref[0]
        for h in range(HEADS):
            li = 2 * HEADS * d + h
            lb = li + HEADS
            lm = li + 4 * HEADS
            lr = li + 8 * HEADS
            brow = gr[lb:lb + 1, :]
            g_tot = brow[:, L - 1:L] if d == 0 else brow[:, 0:1]
            wend = g_tot - brow + gr[li:li + 1, :]
            a = jnp.max(wend, axis=1, keepdims=True)
            m_old = m_ref[d * HEADS + h][0:1, 0:1]
            urow = jnp.maximum(m_old, gr[lm:lm + 1, :])
            m_new = jnp.maximum(g_tot + m_old, a)
            st.append(dict(
                idx=d * HEADS + h, sl=slice(h * ML_DQK, (h + 1) * ML_DQK),
                q_ref=q_ref, k_ref=k_ref, v_ref=v_ref, o_ref=o_ref, mask=mask,
                e=jnp.exp(wend - a + ln_ks), urow=urow, rcol=gc[:, lr:lr + 1],
                di=jnp.exp(m_old - urow), em=jnp.exp(-(brow + urow)), m_new=m_new,
                dec=jnp.exp(g_tot + m_old - m_new), inp=jnp.exp(a - m_new)))
    for t in st:
        sl = t["sl"]
        q = t["q_ref"][:, sl].astype(BF16)
        k = t["k_ref"][:, sl].astype(BF16)
        v_t = t["v_ref"][:, sl].T
        t["vext"] = jnp.concatenate([v_t.astype(BF16), ones_rows], axis=0)
        vext_e = jnp.concatenate([(v_t * t["e"]).astype(BF16), jnp.broadcast_to(t["e"], ones_rows.shape).astype(BF16)],
                                 axis=0)
        t["sT"] = lax.dot_general(k, q, nt, preferred_element_type=F32)
        t["d_c"] = jnp.dot(vext_e, k, preferred_element_type=F32)
        t["qc"] = lax.dot_general(c_ref[t["idx"]].astype(BF16), q, nt, preferred_element_type=F32)
    for t in st:
        dexp = jnp.exp(jnp.where(t["mask"], t["rcol"] - t["urow"] + ln_ks, -jnp.inf))
        t["p"] = (t["sT"] * dexp).astype(BF16)
    for t in st:
        res = t["di"] * t["qc"] + jnp.dot(t["vext"], t["p"], preferred_element_type=F32)
        den = res[ML_DQK:ML_DQK + 1, :]
        h_t = res[0:ML_DQK, :] / jnp.maximum(jnp.abs(den), t["em"])
        t["o_ref"][:, t["sl"]] = h_t.T
        c_ref[t["idx"]] = t["dec"] * c_ref[t["idx"]] + t["inp"] * t["d_c"]
        m_ref[t["idx"]] = jnp.broadcast_to(t["m_new"], (SUBLANE, LANE))


def _mlstm(z, gc, gr, b, s, lc):
    r = z.shape[0]
    L = ML_CHUNK
    nxc, ncc = s // L, lc // L
    x0 = lambda bb: bb * nxc
    c0 = lambda bb: b * nxc + bb * ncc

    def blk_f(bb, st):
        return jnp.where(st < ncc, c0(bb) + st, x0(bb) + st - ncc)

    def blk_b(bb, st):
        return jnp.where(st < ncc, c0(bb) + ncc - 1 - st, x0(bb) + nxc - 1 - (st - ncc))

    def zspec(blk, cb):
        return pl.BlockSpec((L, WIDTH), lambda bb, st: (blk(bb, st), cb))

    def dir_specs(blk):
        return [zspec(blk, Z_MQ // WIDTH), zspec(blk, Z_MK // WIDTH), zspec(blk, Z_MV // WIDTH),
                pl.BlockSpec((L, LANE), lambda bb, st: (blk(bb, st), 0)),
                pl.BlockSpec((1, LANE, L), lambda bb, st: (blk(bb, st), 0, 0))]

    return pl.pallas_call(
        _mlstm_kernel,
        grid=(b, nxc + ncc),
        in_specs=dir_specs(blk_f) + dir_specs(blk_b),
        out_specs=[pl.BlockSpec((L, WIDTH), lambda bb, st: (blk_f(bb, st), 0)),
                   pl.BlockSpec((L, WIDTH), lambda bb, st: (blk_b(bb, st), 0))],
        out_shape=[jax.ShapeDtypeStruct((r, WIDTH), F32), jax.ShapeDtypeStruct((r, WIDTH), F32)],
        scratch_shapes=[pltpu.VMEM((2 * HEADS, ML_ROWS, ML_DQK), F32),
                        pltpu.VMEM((2 * HEADS, SUBLANE, LANE), F32)],
        compiler_params=_cparams(("arbitrary", "arbitrary"), VMEM_LIMIT),
        name="mlstm",
    )(z, z, z, gc, gr, z, z, z, gc, gr)


def _rg_ab_kernel(cur_ref, prev_ref, next_ref, cw_ref, cb_ref, w_ref, ba_ref, bx_ref, lam_ref,
                  af_ref, bf_ref, ab_ref, bb_ref, xe_ref, *, tm, s_len, c_len, rows_x):
    row0 = pl.program_id(0) * tm
    in_x = row0 < rows_x
    seq = jnp.where(in_x, s_len, c_len)
    off = jnp.where(in_x, row0, row0 - rows_x)
    first = lax.rem(off, seq) == 0
    last = lax.rem(off + tm, seq) == 0
    xe_ref[0:SUBLANE, :] = jnp.where(first, 0.0, prev_ref[...])
    xe_ref[SUBLANE:SUBLANE + tm, :] = cur_ref[...]
    xe_ref[SUBLANE + tm:2 * SUBLANE + tm, :] = jnp.where(last, 0.0, next_ref[...])
    lp = RG_CONV // 2
    xc = cb_ref[...] + cw_ref[0:1, :] * xe_ref[pl.ds(SUBLANE - lp, tm), :]
    for j in range(1, RG_CONV):
        xc = xc + cw_ref[j:j + 1, :] * xe_ref[pl.ds(SUBLANE - lp + j, tm), :]
    outs = ((af_ref, bf_ref), (ab_ref, bb_ref))
    for g in range(RG_BLOCKS):
        sl = slice(g * RG_BW, (g + 1) * RG_BW)
        xg = xc[:, sl]
        o = jnp.dot(xg.astype(BF16), w_ref[g], preferred_element_type=F32)
        for d in range(2):
            r = _sigmoid(o[:, (2 * d) * RG_BW:(2 * d + 1) * RG_BW] + ba_ref[d:d + 1, sl])
            i = _sigmoid(o[:, (2 * d + 1) * RG_BW:(2 * d + 2) * RG_BW] + bx_ref[d:d + 1, sl])
            log_a = (-RG_C) * r * _softplus(-lam_ref[d:d + 1, sl])
            a = jnp.exp(log_a)
            one_m_a2 = -jnp.tanh(log_a) * (a * a + 1.0)
            outs[d][0][:, sl] = a
            outs[d][1][:, sl] = jnp.sqrt(one_m_a2) * (i * xg)


def _rg_coeffs(z, cw, cb, w_rg, ba, bx, lam, b, s, lc, tm):
    r = z.shape[0]
    per = tm // SUBLANE
    nblk8 = r // SUBLANE
    cbk = Z_RX // WIDTH
    full = lambda shape: pl.BlockSpec(shape, lambda i: (0,) * len(shape))
    kernel = functools.partial(_rg_ab_kernel, tm=tm, s_len=s, c_len=lc, rows_x=b * s)
    o_spec = pl.BlockSpec((tm, WIDTH), lambda i: (i, 0))
    o_shape = jax.ShapeDtypeStruct((r, WIDTH), F32)
    return pl.pallas_call(
        kernel,
        grid=(r // tm,),
        in_specs=[pl.BlockSpec((tm, WIDTH), lambda i: (i, cbk)),
                  pl.BlockSpec((SUBLANE, WIDTH), lambda i: (jnp.maximum(i * per - 1, 0), cbk)),
                  pl.BlockSpec((SUBLANE, WIDTH), lambda i: (jnp.minimum((i + 1) * per, nblk8 - 1), cbk)),
                  full((RG_CONV, WIDTH)), full((1, WIDTH)), full(w_rg.shape),
                  full((2, WIDTH)), full((2, WIDTH)), full((2, WIDTH))],
        out_specs=[o_spec] * 4,
        out_shape=[o_shape] * 4,
        scratch_shapes=[pltpu.VMEM((tm + 2 * SUBLANE, WIDTH), F32)],
        compiler_params=_cparams(("parallel",), VMEM_LIMIT),
        name="rglru_coeffs",
    )(z, z, z, cw, cb.reshape(1, WIDTH), w_rg, ba, bx, lam)


def _rg_scan_kernel(af_ref, bf_ref, ab_ref, bb_ref, hf_ref, hb_ref, sf_ref, sb_ref, *, tt):
    @pl.when(pl.program_id(1) == 0)
    def _():
        sf_ref[...] = jnp.zeros(sf_ref.shape, F32)
        sb_ref[...] = jnp.zeros(sb_ref.shape, F32)

    def body(t, carry):
        hf, hb = carry
        hf = af_ref[pl.ds(t, 1), :] * hf + bf_ref[pl.ds(t, 1), :]
        hf_ref[pl.ds(t, 1), :] = hf
        tb = tt - 1 - t
        hb = ab_ref[pl.ds(tb, 1), :] * hb + bb_ref[pl.ds(tb, 1), :]
        hb_ref[pl.ds(tb, 1), :] = hb
        return hf, hb

    hf, hb = lax.fori_loop(0, tt, body, (sf_ref[...], sb_ref[...]), unroll=8)
    sf_ref[...] = hf
    sb_ref[...] = hb


def _rg_scan(af, bf, ab, bb, b, s, lc, tt):
    r = af.shape[0]
    nxt, nct = s // tt, lc // tt
    x0 = lambda bi: bi * nxt
    c0 = lambda bi: b * nxt + bi * nct

    def blk_f(bi, st):
        return jnp.where(st < nct, c0(bi) + st, x0(bi) + st - nct)

    def blk_b(bi, st):
        return jnp.where(st < nct, c0(bi) + nct - 1 - st, x0(bi) + nxt - 1 - (st - nct))

    sf = pl.BlockSpec((tt, WIDTH), lambda bi, st: (blk_f(bi, st), 0))
    sb = pl.BlockSpec((tt, WIDTH), lambda bi, st: (blk_b(bi, st), 0))
    o_shape = jax.ShapeDtypeStruct((r, WIDTH), F32)
    return pl.pallas_call(
        functools.partial(_rg_scan_kernel, tt=tt),
        grid=(b, nxt + nct),
        in_specs=[sf, sf, sb, sb],
        out_specs=[sf, sb],
        out_shape=[o_shape, o_shape],
        scratch_shapes=[pltpu.VMEM((1, WIDTH), F32), pltpu.VMEM((1, WIDTH), F32)],
        compiler_params=_cparams(("arbitrary", "arbitrary")),
        name="rglru_scan",
    )(af, bf, ab, bb)


def _finish_kernel(mhf_ref, mhb_ref, zo_ref, ng_ref, rhf_ref, rhb_ref, zg_ref, yb_ref, yr_ref):
    hsum = mhf_ref[...] + mhb_ref[...]
    for h in range(HEADS):
        sl = slice(h * V_HEAD, (h + 1) * V_HEAD)
        hn = _rms(hsum[:, sl], ng_ref[:, sl])
        yb_ref[:, sl] = (hn * _sigmoid(zo_ref[:, sl])).astype(BF16)
    yr_ref[...] = ((rhf_ref[...] + rhb_ref[...]) * _gelu_tanh(zg_ref[...])).astype(BF16)


def _finish(rows, mhf, mhb, z, ng, rhf, rhb, tm):
    wide = lambda: pl.BlockSpec((tm, WIDTH), lambda i: (i, 0))
    zblk = lambda cb: pl.BlockSpec((tm, WIDTH), lambda i: (i, cb))
    o_shape = jax.ShapeDtypeStruct((rows, WIDTH), BF16)
    return pl.pallas_call(
        _finish_kernel,
        grid=(rows // tm,),
        in_specs=[wide(), wide(), zblk(Z_MO // WIDTH), pl.BlockSpec((1, WIDTH), lambda i: (0, 0)),
                  wide(), wide(), zblk(Z_RG // WIDTH)],
        out_specs=[wide(), wide()],
        out_shape=[o_shape, o_shape],
        compiler_params=_cparams(("parallel",), VMEM_LIMIT),
        name="branch_finish",
    )(mhf, mhb, z, ng.reshape(1, WIDTH), rhf, rhb, z)


def _merge_kernel(ya_ref, yb_ref, yr_ref, gm0_ref, gm1_ref, gm2_ref, w0_ref, w1_ref, w2_ref, m_ref):
    m = (gm0_ref[...].astype(F32) * jnp.dot(ya_ref[...], w0_ref[0], preferred_element_type=F32)
         + gm1_ref[...].astype(F32) * jnp.dot(yb_ref[...], w1_ref[0], preferred_element_type=F32)
         + gm2_ref[...].astype(F32) * jnp.dot(yr_ref[...], w2_ref[0], preferred_element_type=F32))
    m_ref[...] = m.astype(BF16)


def _merge(rows, d, ya, yb, yr, gm, wb, tm):
    tn = _pick(d, (1024, 512, 256, 128))
    wide = lambda: pl.BlockSpec((tm, WIDTH), lambda j, i: (i, 0))
    gate = lambda br: pl.BlockSpec((tm, tn), lambda j, i: (i, (br * d) // tn + j))
    wspec = lambda br: pl.BlockSpec((1, WIDTH, tn), lambda j, i: (br, 0, j))
    return pl.pallas_call(
        _merge_kernel,
        grid=(d // tn, rows // tm),
        in_specs=[wide(), wide(), wide(), gate(0), gate(1), gate(2), wspec(0), wspec(1), wspec(2)],
        out_specs=pl.BlockSpec((tm, tn), lambda j, i: (i, j)),
        out_shape=jax.ShapeDtypeStruct((rows, d), BF16),
        compiler_params=_cparams(("parallel", "parallel"), VMEM_LIMIT),
        name="branch_merge",
    )(ya, yb, yr, gm, gm, gm, wb, wb, wb)


def _outproj_kernel(m_ref, w_ref, x_ref, gate_ref, o_ref):
    o_ref[...] = x_ref[...] + gate_ref[0] * jnp.dot(m_ref[...], w_ref[...], preferred_element_type=F32)


def _out_projection(rows, src, m, w_out, mods3, mod_row, span, prev):
    tm, blk0, nblk = span
    xs, src0 = src
    d = xs.shape[1]
    tn = _pick(d, (1024, 512, 256, 128))
    return _pcall(
        _outproj_kernel,
        prev=prev,
        grid=(nblk, d // tn),
        in_specs=[pl.BlockSpec((tm, d), lambda i, j: (blk0 + i, 0)),
                  pl.BlockSpec((d, tn), lambda i, j: (0, j)),
                  pl.BlockSpec((tm, tn), lambda i, j: (src0 + i, j)),
                  pl.BlockSpec((1, 1, tn), lambda i, j: (mod_row((blk0 + i) * tm), 0, (2 * d) // tn + j))],
        args=[m, w_out, xs, mods3],
        out_specs=pl.BlockSpec((tm, tn), lambda i, j: (blk0 + i, j)),
        out_shape=jax.ShapeDtypeStruct((rows, d), F32),
        compiler_params=_cparams(("parallel", "parallel"), VMEM_LIMIT),
        name="out_proj",
    )


def _mlp_kernel(x_ref, mod_ref, g_ref, w1_ref, w2_ref, fg_ref, o_ref, xn_ref, *, d, final):
    j = pl.program_id(1)

    @pl.when(j == 0)
    def _():
        y = _rms(x_ref[...], g_ref[...])
        shift = mod_ref[0, :, 3 * d:4 * d]
        scale = mod_ref[0, :, 4 * d:5 * d]
        xn_ref[...] = (y * (1.0 + scale) + shift).astype(BF16)
        o_ref[...] = jnp.zeros(o_ref.shape, F32)

    h = jnp.dot(xn_ref[...], w1_ref[...], preferred_element_type=F32)
    h = jnp.square(jnp.maximum(h, 0.0)).astype(BF16)
    o_ref[...] += jnp.dot(h, w2_ref[...], preferred_element_type=F32)

    @pl.when(j == pl.num_programs(1) - 1)
    def _():
        out = x_ref[...] + mod_ref[0, :, 5 * d:6 * d] * o_ref[...]
        if final:
            out = _rms(out, fg_ref[...])
        o_ref[...] = out


def _mlp(rows, x1, mods3, g, w1, w2, fg, mod_row, span, prev, final):
    tm, blk0, nblk = span
    d = x1.shape[1]
    dff = w1.shape[1]
    tf = _pick(dff, (512, 256, 128))
    return _pcall(
        functools.partial(_mlp_kernel, d=d, final=final),
        prev=prev,
        grid=(nblk, dff // tf),
        in_specs=[pl.BlockSpec((tm, d), lambda i, j: (blk0 + i, 0)),
                  pl.BlockSpec((1, 1, N_MOD * d), lambda i, j: (mod_row((blk0 + i) * tm), 0, 0)),
                  pl.BlockSpec((1, d), lambda i, j: (0, 0)),
                  pl.BlockSpec((d, tf), lambda i, j: (0, j)),
                  pl.BlockSpec((tf, d), lambda i, j: (j, 0)),
                  pl.BlockSpec((1, d), lambda i, j: (0, 0))],
        args=[x1, mods3, g.reshape(1, d), w1, w2, fg.reshape(1, d)],
        out_specs=pl.BlockSpec((tm, d), lambda i, j: (blk0 + i, 0)),
        out_shape=jax.ShapeDtypeStruct((rows, d), F32),
        scratch_shapes=[pltpu.VMEM((tm, d), BF16)],
        compiler_params=_cparams(("parallel", "arbitrary"), VMEM_LIMIT),
        name="mlp",
    )


def _rope_perm_cols(w):
    n = QK_ROPE // 4
    return jnp.concatenate([-w[..., n:2 * n], w[..., 0:n], -w[..., 3 * n:4 * n], w[..., 2 * n:3 * n]], axis=-1)


def _layer_weights(l, d, w_in, w_uq, w_ukv, mlstm_gate_b, rg_wa, rg_wx, w_branch, w_out, w_mlp1, w_mlp2):
    wi = w_in[l].astype(BF16)
    o = 0
    parts = {}
    for name, wdt in (("mla_q", Q_LORA), ("mla_kv", KV_LORA), ("mla_kr", QK_ROPE), ("ml_q", WIDTH),
                      ("ml_k", WIDTH), ("ml_v", WIDTH), ("ml_o", WIDTH), ("ml_g", 4 * HEADS),
                      ("rg_x", WIDTH), ("rg_gate", WIDTH), ("merge", 3 * d)):
        parts[name] = wi[:, o:o + wdt]
        o += wdt
    g_pad = jnp.pad(parts["ml_g"], ((0, 0), (0, LANE - 4 * HEADS)))
    w_in_p = jnp.concatenate(
        [parts["mla_q"], parts["mla_kv"], parts["mla_kr"], _rope_perm_cols(parts["mla_kr"]), g_pad,
         parts["ml_q"], parts["ml_k"], parts["ml_v"], parts["ml_o"], parts["rg_x"], parts["rg_gate"],
         parts["merge"]], axis=1)
    wq = w_uq[l].reshape(Q_LORA, HEADS, QK_DIM)
    wqt = jnp.transpose(wq, (1, 2, 0)).reshape(HEADS * QK_DIM, Q_LORA).astype(BF16)
    wqrt = jnp.transpose(_rope_perm_cols(wq[:, :, QK_NOPE:]), (1, 2, 0)).reshape(HEADS * QK_ROPE, Q_LORA).astype(BF16)
    wkv = w_ukv[l].reshape(KV_LORA, HEADS, QK_NOPE + V_HEAD)
    wk = wkv[:, :, :QK_NOPE].reshape(KV_LORA, HEADS * QK_NOPE).astype(BF16)
    wvt = jnp.transpose(wkv[:, :, QK_NOPE:], (1, 2, 0)).reshape(HEADS * V_HEAD, KV_LORA).astype(BF16)
    gate_b = jnp.pad(mlstm_gate_b[l].reshape(1, 4 * HEADS), ((0, 0), (0, LANE - 4 * HEADS)))
    w_rg = jnp.concatenate([rg_wa[l, 0], rg_wx[l, 0], rg_wa[l, 1], rg_wx[l, 1]], axis=-1).astype(BF16)
    return dict(w_in=w_in_p, wqt=wqt, wqrt=wqrt, wk=wk, wvt=wvt, gate_b=gate_b, w_rg=w_rg,
                wb=w_branch[l].astype(BF16), w_out=w_out[l].astype(BF16),
                w1=w_mlp1[l].astype(BF16), w2=w_mlp2[l].astype(BF16))


def _rope_tables(b, s, lc):
    t = jnp.arange(s, dtype=jnp.int32)
    row = (t // GRID_W).astype(F32)
    col = (t % GRID_W).astype(F32)
    n_freq = QK_ROPE // 4
    inv = ROPE_BASE ** (-jnp.arange(n_freq, dtype=F32) / n_freq)
    ang_r = row[:, None] * inv[None, :]
    ang_c = col[:, None] * inv[None, :]
    cos = jnp.concatenate([jnp.cos(ang_r)] * 2 + [jnp.cos(ang_c)] * 2, axis=1)
    sin = jnp.concatenate([jnp.sin(ang_r)] * 2 + [jnp.sin(ang_c)] * 2, axis=1)
    cos_u = jnp.concatenate([jnp.tile(cos, (b, 1)), jnp.ones((b * lc, QK_ROPE), F32)], axis=0)
    sin_u = jnp.concatenate([jnp.tile(sin, (b, 1)), jnp.zeros((b * lc, QK_ROPE), F32)], axis=0)
    return cos_u, sin_u, cos_u.T, sin_u.T


def kernel(x, c, ctx, c_ctx, norm1_g, norm2_g, w_mod, b_mod, w_in, q_norm_g, w_uq, kv_norm_g, w_ukv,
           mlstm_gate_b, mlstm_norm_g, rg_conv_w, rg_conv_b, rg_wa, rg_ba, rg_wx, rg_bx, rg_lam,
           w_branch, w_out, w_mlp1, w_mlp2, final_g):
    b, s, d = x.shape
    lc = ctx.shape[1]
    depth = w_in.shape[0]
    rows_x, rows_c = b * s, b * lc
    r = rows_x + rows_c

    tm_x = _pick(s, (1024, 512, 256, 128))
    tm_c = _pick(math.gcd(rows_x, rows_c), (512, 256, 128))
    tm_mid = _pick(math.gcd(rows_x, rows_c), (512, 256, 128))
    tm_seq = _pick(math.gcd(s, lc), (256, 128))
    tq = _pick(s, (1024, 512, 256, 128))
    span_x = (tm_x, 0, rows_x // tm_x)
    span_c = (tm_c, rows_x // tm_c, rows_c // tm_c)

    def mod_row(row0):
        return jnp.where(row0 < rows_x, 1 + row0 // s, 0)

    def dense(fn, with_ctx):
        out = fn(span_x, src_x, None)
        return fn(span_c, src_c, out) if with_ctx else out

    src_x = (x.reshape(rows_x, d), 0)
    src_c = (ctx.reshape(rows_c, d), 0)
    cc = jnp.concatenate([c_ctx[None, :], c, jnp.zeros((SUBLANE - 1 - b, d), F32)], axis=0)
    tabs = _rope_tables(b, s, lc)

    for l in range(depth):
        last = l == depth - 1
        w = _layer_weights(l, d, w_in, w_uq, w_ukv, mlstm_gate_b, rg_wa, rg_wx, w_branch, w_out,
                           w_mlp1, w_mlp2)
        mods3 = _modulation(cc, w_mod, b_mod, l).reshape(SUBLANE, 1, N_MOD * d)
        z, gm = dense(lambda span, src, prev: _in_projection(r, src, mods3, norm1_g[l], w["w_in"], mod_row,
                                                             span, prev), True)

        qt, kk, vt = _mla_prep(z, tabs, q_norm_g[l], kv_norm_g[l], w["wqt"], w["wqrt"], w["wk"],
                               w["wvt"], b, s, lc, tm_seq)
        ya = _attn_call(qt, kk, vt, b=b, nq=s // tq, tq=tq, q_blk0=0, kv_len=s + lc, ctx_len=lc, kv_blk0=0,
                        out_rows=r, row_blk=lambda bb, i: bb * (s // tq) + i, prev=None)
        if not last:
            ya = _attn_call(qt, kk, vt, b=b, nq=1, tq=lc, q_blk0=s // lc, kv_len=lc, ctx_len=lc, kv_blk0=s // lc,
                            out_rows=r, row_blk=lambda bb, i: rows_x // lc + bb, prev=ya)

        gc, gr = _mlstm_gates(z, w["gate_b"])
        mhf, mhb = _mlstm(z, gc, gr, b, s, lc)

        af, bf, ab, bb_ = _rg_coeffs(z, rg_conv_w[l], rg_conv_b[l], w["w_rg"], rg_ba[l], rg_bx[l],
                                     rg_lam[l], b, s, lc, tm_seq)
        rhf, rhb = _rg_scan(af, bf, ab, bb_, b, s, lc, tm_seq)

        rows = rows_x if last else r
        yb, yr = _finish(rows, mhf, mhb, z, mlstm_norm_g[l], rhf, rhb, tm_mid)
        m = _merge(rows, d, ya, yb, yr, gm, w["wb"], tm_mid)
        x1 = dense(lambda span, src, prev: _out_projection(rows, src, m, w["w_out"], mods3, mod_row, span,
                                                           prev), not last)
        xu = dense(lambda span, src, prev: _mlp(rows, x1, mods3, norm2_g[l], w["w1"], w["w2"], final_g,
                                                mod_row, span, prev, last), not last)
        src_x = (xu, span_x[1])
        src_c = (xu, span_c[1])

    return xu.reshape(b, s, d)
```

```python
import functools
import math

import jax
import jax.numpy as jnp
from jax import lax
from jax.experimental import pallas as pl
from jax.experimental.pallas import tpu as pltpu

F32 = jnp.float32
BF16 = jnp.bfloat16

HEADS = 8
Q_LORA = 512
KV_LORA = 256
QK_NOPE = 128
QK_ROPE = 64
V_HEAD = 128
QK_DIM = QK_NOPE + QK_ROPE
ROPE_BASE = 10000.0
GRID_W = 64
ML_DQK = 128
ML_CHUNK = 128
WIDTH = 1024
RG_BLOCKS = 8
RG_BW = WIDTH // RG_BLOCKS
RG_CONV = 4
RG_C = 8.0
N_MOD = 6
EPS = 1e-6

LANE = 128
SUBLANE = 8
BF16_ROWS = 16
MXU_DIM = 256
ATTN_TK = 10 * MXU_DIM
VT_ROWS = V_HEAD
ML_ROWS = ML_DQK + BF16_ROWS

Z_Q = 0
Z_KV = Z_Q + Q_LORA
Z_KR = Z_KV + KV_LORA
Z_KRP = Z_KR + QK_ROPE
Z_MLA_END = Z_KRP + QK_ROPE
Z_G = Z_MLA_END
Z_MQ = Z_G + LANE
Z_MK = Z_MQ + WIDTH
Z_MV = Z_MK + WIDTH
Z_MO = Z_MV + WIDTH
Z_RX = Z_MO + WIDTH
Z_RG = Z_RX + WIDTH
Z_MERGE = Z_RG + WIDTH

Q_SCALE = (QK_DIM ** -0.5) * math.log2(math.e)
VMEM_LIMIT = 56 * 1024 * 1024


def _cparams(sem, vmem=None):
    return pltpu.CompilerParams(dimension_semantics=sem, vmem_limit_bytes=vmem)


def _pick(n, cands):
    for c in cands:
        if n % c == 0:
            return c
    raise ValueError(f"no tile in {cands} divides {n}")


def _pcall(kernel, *, prev, in_specs, args, **kw):
    if prev is None:
        return pl.pallas_call(kernel, in_specs=in_specs, **kw)(*args)
    prevs = list(prev) if isinstance(prev, (tuple, list)) else [prev]
    n_in, n_prev = len(args), len(prevs)

    def aliased(*refs):
        kernel(*refs[:n_in], *refs[n_in + n_prev:])

    return pl.pallas_call(aliased, in_specs=in_specs + [pl.BlockSpec(memory_space=pl.ANY)] * n_prev,
                          input_output_aliases={n_in + k: k for k in range(n_prev)}, **kw)(*args, *prevs)


def _rms(x, g):
    return x * lax.rsqrt(jnp.mean(x * x, axis=-1, keepdims=True) + EPS) * g


def _sigmoid(x):
    return 0.5 * jnp.tanh(0.5 * x) + 0.5


def _softplus(x):
    return jnp.maximum(x, 0.0) + jnp.log1p(jnp.exp(-jnp.abs(x)))


def _log_sigmoid(x):
    return jnp.minimum(x, 0.0) - jnp.log1p(jnp.exp(-jnp.abs(x)))


def _gelu_tanh(x):
    return 0.5 * x * (1.0 + jnp.tanh(math.sqrt(2.0 / math.pi) * (x + 0.044715 * (x * x * x))))


def _mod_kernel(c_ref, w_ref, b_ref, o_ref):
    c = c_ref[...]
    sc = c * jax.nn.sigmoid(c)
    o_ref[...] = jnp.dot(sc, w_ref[0], preferred_element_type=F32,
                         precision=lax.Precision.HIGHEST) + b_ref[0]


def _modulation(cc, w_mod, b_mod, l):
    rows, d = cc.shape
    depth, _, n = w_mod.shape
    tn = _pick(n, (1024, 512, 256, 128))
    return pl.pallas_call(
        _mod_kernel,
        grid=(n // tn,),
        in_specs=[pl.BlockSpec((rows, d), lambda j: (0, 0)),
                  pl.BlockSpec((1, d, tn), lambda j: (l, 0, j)),
                  pl.BlockSpec((1, 1, tn), lambda j: (l, 0, j))],
        out_specs=pl.BlockSpec((rows, tn), lambda j: (0, j)),
        out_shape=jax.ShapeDtypeStruct((rows, n), F32),
        compiler_params=_cparams(("parallel",), VMEM_LIMIT),
        name="adaln_mod",
    )(cc, w_mod, b_mod.reshape(depth, 1, n))


def _inproj_kernel(x_ref, mod_ref, g_ref, w_ref, z_ref, zm_ref, xn_ref, *, d, nz):
    @pl.when(pl.program_id(1) == 0)
    def _():
        y = _rms(x_ref[...], g_ref[...])
        shift = mod_ref[0, :, 0:d]
        scale = mod_ref[0, :, d:2 * d]
        xn_ref[...] = (y * (1.0 + scale) + shift).astype(BF16)

    @pl.when(pl.program_id(1) < nz)
    def _():
        z_ref[...] = jnp.dot(xn_ref[...], w_ref[...], preferred_element_type=F32)

    @pl.when(pl.program_id(1) >= nz)
    def _():
        zm_ref[...] = _sigmoid(jnp.dot(xn_ref[...], w_ref[...], preferred_element_type=F32)).astype(BF16)


def _in_projection(r, src, mods3, g, w_in_p, mod_row, span, prev):
    tm, blk0, nblk = span
    xs, src0 = src
    d = xs.shape[1]
    n = w_in_p.shape[1]
    tn = _pick(math.gcd(Z_MERGE, n - Z_MERGE), (1024, 512, 256, 128))
    nz = Z_MERGE // tn
    return _pcall(
        functools.partial(_inproj_kernel, d=d, nz=nz),
        prev=prev,
        grid=(nblk, n // tn),
        in_specs=[pl.BlockSpec((tm, d), lambda i, j: (src0 + i, 0)),
                  pl.BlockSpec((1, 1, N_MOD * d), lambda i, j: (mod_row((blk0 + i) * tm), 0, 0)),
                  pl.BlockSpec((1, d), lambda i, j: (0, 0)),
                  pl.BlockSpec((d, tn), lambda i, j: (0, j))],
        args=[xs, mods3, g.reshape(1, d), w_in_p],
        out_specs=[pl.BlockSpec((tm, tn), lambda i, j: (blk0 + i, jnp.minimum(j, nz - 1))),
                   pl.BlockSpec((tm, tn), lambda i, j: (blk0 + i, jnp.maximum(j - nz, 0)))],
        out_shape=[jax.ShapeDtypeStruct((r, Z_MERGE), F32), jax.ShapeDtypeStruct((r, n - Z_MERGE), BF16)],
        scratch_shapes=[pltpu.VMEM((tm, d), BF16)],
        compiler_params=_cparams(("parallel", "arbitrary"), VMEM_LIMIT),
        name="in_proj",
    )


def _mla_prep_kernel(z_ref, cos_ref, sin_ref, cost_ref, sint_ref, gq_ref, gkv_ref,
                     wqt_ref, wqrt_ref, wk_ref, wvt_ref, qt_ref, k_ref, vt_ref):
    qn = _rms(z_ref[:, Z_Q:Z_Q + Q_LORA], gq_ref[...]).astype(BF16)
    kvn = _rms(z_ref[:, Z_KV:Z_KV + KV_LORA], gkv_ref[...]).astype(BF16)
    kr = (z_ref[:, Z_KR:Z_KR + QK_ROPE] * cos_ref[...]
          + z_ref[:, Z_KRP:Z_KRP + QK_ROPE] * sin_ref[...]).astype(BF16)
    kn = jnp.dot(kvn, wk_ref[...], preferred_element_type=F32)
    nt = (((1,), (1,)), ((), ()))
    q_all = lax.dot_general(wqt_ref[...], qn, nt, preferred_element_type=F32)
    qr_all = lax.dot_general(wqrt_ref[...], qn, nt, preferred_element_type=F32)
    v_all = lax.dot_general(wvt_ref[...], kvn, nt, preferred_element_type=F32)
    for h in range(HEADS):
        q_t = q_all[h * QK_DIM:(h + 1) * QK_DIM]
        q_rot = q_t[QK_NOPE:QK_DIM] * cost_ref[...] + qr_all[h * QK_ROPE:(h + 1) * QK_ROPE] * sint_ref[...]
        qt_ref[0, h, 0:QK_NOPE, :] = (q_t[0:QK_NOPE] * Q_SCALE).astype(BF16)
        qt_ref[0, h, QK_NOPE:QK_DIM, :] = (q_rot * Q_SCALE).astype(BF16)
        k_ref[0, h, :, 0:QK_NOPE] = kn[:, h * QK_NOPE:(h + 1) * QK_NOPE].astype(BF16)
        k_ref[0, h, :, QK_NOPE:QK_DIM] = kr
        vt_ref[0, h] = v_all[h * V_HEAD:(h + 1) * V_HEAD].astype(BF16)


def _mla_prep(z, tabs, gq, gkv, wqt, wqrt, wk, wvt, b, s, lc, tm):
    r = z.shape[0]
    cos_u, sin_u, cost_u, sint_u = tabs
    nxt, nct = s // tm, lc // tm
    ltot = s + lc

    def bidx(i):
        j = i - b * nxt
        return jnp.where(i < b * nxt, i // nxt, j // nct)

    def sblk(i):
        j = i - b * nxt
        return jnp.where(i < b * nxt, i % nxt, nxt + j % nct)

    full = lambda shape: pl.BlockSpec(shape, lambda i: (0,) * len(shape))
    return pl.pallas_call(
        _mla_prep_kernel,
        grid=(r // tm,),
        in_specs=[pl.BlockSpec((tm, Z_MLA_END), lambda i: (i, 0)),
                  pl.BlockSpec((tm, QK_ROPE), lambda i: (i, 0)),
                  pl.BlockSpec((tm, QK_ROPE), lambda i: (i, 0)),
                  pl.BlockSpec((QK_ROPE, tm), lambda i: (0, i)),
                  pl.BlockSpec((QK_ROPE, tm), lambda i: (0, i)),
                  full((1, Q_LORA)), full((1, KV_LORA)),
                  full(wqt.shape), full(wqrt.shape), full(wk.shape), full(wvt.shape)],
        out_specs=[pl.BlockSpec((1, HEADS, QK_DIM, tm), lambda i: (bidx(i), 0, 0, sblk(i))),
                   pl.BlockSpec((1, HEADS, tm, QK_DIM), lambda i: (bidx(i), 0, sblk(i), 0)),
                   pl.BlockSpec((1, HEADS, VT_ROWS, tm), lambda i: (bidx(i), 0, 0, sblk(i)))],
        out_shape=[jax.ShapeDtypeStruct((b, HEADS, QK_DIM, ltot), BF16),
                   jax.ShapeDtypeStruct((b, HEADS, ltot, QK_DIM), BF16),
                   jax.ShapeDtypeStruct((b, HEADS, VT_ROWS, ltot), BF16)],
        compiler_params=_cparams(("parallel",), VMEM_LIMIT),
        name="mla_prep",
    )(z, cos_u, sin_u, cost_u, sint_u, gq.reshape(1, -1), gkv.reshape(1, -1), wqt, wqrt, wk, wvt)


def _attn_kernel(qt_ref, k_ref, vt_ref, o_ref, s_ref, *, plan, run):
    qt = qt_ref[0, 0]
    tq = qt.shape[1]

    def scores(off, size, slot):
        s = jnp.dot(k_ref[0, 0, pl.ds(off, size), :], qt, preferred_element_type=F32)
        s_ref[slot, 0:size, :] = s
        return jnp.max(s, axis=0, keepdims=True)

    def update(off, size, slot, m, acc, cmax):
        m_new = jnp.maximum(m, cmax)
        p = jnp.exp2(s_ref[slot, 0:size, :] - m_new)
        alpha = jnp.exp2(m - m_new)
        pv = jnp.dot(vt_ref[0, 0, :, pl.ds(off, size)], p.astype(BF16), preferred_element_type=F32)
        return m_new, (alpha * acc[0] + pv, alpha * acc[1] + jnp.sum(p, axis=0, keepdims=True))

    n = len(plan)
    a, trips = run
    m = jnp.full((1, tq), -jnp.inf, F32)
    acc = (jnp.zeros((V_HEAD, tq), F32), jnp.zeros((1, tq), F32))
    cmax = scores(plan[0][0], plan[0][1], 0)
    f = 0
    while f < n:
        if trips and f == a:
            base, tk = plan[a]

            def body(i, carry, base=base, tk=tk):
                m, acc, cmax0 = carry
                off = pl.multiple_of(base + 2 * i * tk, MXU_DIM)
                cmax1 = scores(off + tk, tk, (a + 1) % 2)
                m, acc = update(off, tk, a % 2, m, acc, cmax0)
                cmax0 = scores(off + 2 * tk, tk, a % 2)
                m, acc = update(off + tk, tk, (a + 1) % 2, m, acc, cmax1)
                return m, acc, cmax0

            m, acc, cmax = lax.fori_loop(0, trips, body, (m, acc, cmax))
            f = a + 2 * trips
        else:
            nxt = scores(plan[f + 1][0], plan[f + 1][1], (f + 1) % 2) if f + 1 < n else None
            m, acc = update(plan[f][0], plan[f][1], f % 2, m, acc, cmax)
            cmax = nxt
            f += 1
    o = acc[0] / acc[1]
    o_ref[...] = o.T.astype(o_ref.dtype)


def _chunk_plan(lat, ctx_len, tk):
    if lat == 0:
        return ((0, ctx_len),), (0, 0)
    assert lat % MXU_DIM == 0 and ctx_len % MXU_DIM == 0 and tk % MXU_DIM == 0
    tail = MXU_DIM if lat > MXU_DIM else 0
    first = (lat - tail) % tk
    n_uni = (lat - tail) // tk
    plan = [(lat, ctx_len)]
    if first:
        plan.append((0, first))
    a = len(plan)
    plan += [(first + k * tk, tk) for k in range(n_uni)]
    if tail:
        plan.append((lat - tail, tail))
    return tuple(plan), (a, max(n_uni - 1, 0) // 2)


def _attn_call(qt, k, vt, *, b, nq, tq, q_blk0, kv_len, ctx_len, kv_blk0, out_rows, row_blk, prev):
    lat = kv_len - ctx_len
    tk = ATTN_TK if lat >= 4 * ATTN_TK else MXU_DIM
    plan, run = _chunk_plan(lat, ctx_len, tk)
    tk = max(size for _, size in plan)
    return _pcall(
        functools.partial(_attn_kernel, plan=plan, run=run),
        prev=prev,
        grid=(b, HEADS, nq),
        in_specs=[pl.BlockSpec((1, 1, QK_DIM, tq), lambda bb, h, i: (bb, h, 0, q_blk0 + i)),
                  pl.BlockSpec((1, 1, kv_len, QK_DIM), lambda bb, h, i: (bb, h, kv_blk0, 0),
                               pipeline_mode=pl.Buffered(1)),
                  pl.BlockSpec((1, 1, VT_ROWS, kv_len), lambda bb, h, i: (bb, h, 0, kv_blk0),
                               pipeline_mode=pl.Buffered(1))],
        args=[qt, k, vt],
        out_specs=pl.BlockSpec((tq, V_HEAD), lambda bb, h, i: (row_blk(bb, i), h)),
        out_shape=jax.ShapeDtypeStruct((out_rows, WIDTH), BF16),
        scratch_shapes=[pltpu.VMEM((2, tk, tq), F32)],
        compiler_params=_cparams(("parallel", "parallel", "arbitrary"), VMEM_LIMIT),
        name="mla_attn",
    )


def _mlstm_gates_kernel(z_ref, b_ref, gc_ref, gr_ref):
    for c in range(gr_ref.shape[0]):
        rows = slice(c * ML_CHUNK, (c + 1) * ML_CHUNK)
        gc = _mlstm_gates_chunk(z_ref[rows, :] + b_ref[...])
        gc_ref[rows, :] = gc
        gr_ref[c] = gc.T


def _mlstm_gates_chunk(g):
    n = g.shape[0]
    lane = lax.broadcasted_iota(jnp.int32, g.shape, 1)
    row = lax.broadcasted_iota(jnp.int32, g.shape, 0)
    lf = _log_sigmoid(g)
    pre = lf
    suf = lf
    k = 1
    while k < n:
        pre = pre + jnp.where(row >= k, pltpu.roll(pre, k, 0), 0.0)
        suf = suf + jnp.where(row < n - k, pltpu.roll(suf, n - k, 0), 0.0)
        k *= 2
    bwd = lane >= 2 * HEADS
    cum = jnp.where(bwd, suf, pre)
    r = g - pltpu.roll(cum, LANE - HEADS, 1)
    pmax = r
    smax = r
    k = 1
    while k < n:
        pmax = jnp.maximum(pmax, jnp.where(row >= k, pltpu.roll(pmax, k, 0), -jnp.inf))
        smax = jnp.maximum(smax, jnp.where(row < n - k, pltpu.roll(smax, n - k, 0), -jnp.inf))
        k *= 2
    cmax = jnp.where(bwd, smax, pmax)
    is_f = ((lane >= HEADS) & (lane < 2 * HEADS)) | ((lane >= 3 * HEADS) & (lane < 4 * HEADS))
    gc = jnp.where(is_f, cum, g)
    gc = jnp.where((lane >= 4 * HEADS) & (lane < 8 * HEADS), pltpu.roll(cmax, 4 * HEADS, 1), gc)
    return jnp.where((lane >= 8 * HEADS) & (lane < 12 * HEADS), pltpu.roll(r, 8 * HEADS, 1), gc)


def _mlstm_gates(z, bias_pad):
    r = z.shape[0]
    nchunk = r // ML_CHUNK
    per = _pick(nchunk, (4, 2, 1))
    return pl.pallas_call(
        _mlstm_gates_kernel,
        grid=(nchunk // per,),
        in_specs=[pl.BlockSpec((per * ML_CHUNK, LANE), lambda i: (i, Z_G // LANE)),
                  pl.BlockSpec((1, LANE), lambda i: (0, 0))],
        out_specs=[pl.BlockSpec((per * ML_CHUNK, LANE), lambda i: (i, 0)),
                   pl.BlockSpec((per, LANE, ML_CHUNK), lambda i: (i, 0, 0))],
        out_shape=[jax.ShapeDtypeStruct((r, LANE), F32),
                   jax.ShapeDtypeStruct((nchunk, LANE, ML_CHUNK), F32)],
        compiler_params=_cparams(("parallel",)),
        name="mlstm_gates",
    )(z, bias_pad)


def _mlstm_kernel(qf_ref, kf_ref, vf_ref, gcf_ref, grf_ref,
                  qb_ref, kb_ref, vb_ref, gcb_ref, grb_ref,
                  hf_ref, hb_ref, c_ref, m_ref):
    L = ML_CHUNK

    @pl.when(pl.program_id(1) == 0)
    def _():
        c_ref[...] = jnp.zeros(c_ref.shape, F32)
        m_ref[...] = jnp.zeros(m_ref.shape, F32)

    row = lax.broadcasted_iota(jnp.int32, (L, L), 0)
    col = lax.broadcasted_iota(jnp.int32, (L, L), 1)
    nt = (((1,), (1,)), ((), ()))
    ones_rows = jnp.ones((ML_ROWS - ML_DQK, L), BF16)
    dirs = ((qf_ref, kf_ref, vf_ref, gcf_ref, grf_ref, hf_ref, row <= col),
            (qb_ref, kb_ref, vb_ref, gcb_ref, grb_ref, hb_ref, row >= col))
    ln_ks = math.log(ML_DQK ** -0.5)
    st = []
    for d, (q_ref, k_ref, v_ref, gc_ref, gr_ref, o_ref, mask) in enumerate(dirs):
        gc = gc_ref[...]
        gr = gr_ref[0]
        for h in range(HEADS):
            li = 2 * HEADS * d + h
            lb = li + HEADS
            lm = li + 4 * HEADS
            lr = li + 8 * HEADS
            brow = gr[lb:lb + 1, :]
            g_tot = brow[:, L - 1:L] if d == 0 else brow[:, 0:1]
            wend = g_tot - brow + gr[li:li + 1, :]
            a = jnp.max(wend, axis=1, keepdims=True)
            m_old = m_ref[d * HEADS + h][0:1, 0:1]
            urow = jnp.maximum(m_old, gr[lm:lm + 1, :])
            m_new = jnp.maximum(g_tot + m_old, a)
            st.append(dict(
                idx=d * HEADS + h, sl=slice(h * ML_DQK, (h + 1) * ML_DQK),
                q_ref=q_ref, k_ref=k_ref, v_ref=v_ref, o_ref=o_ref, mask=mask,
                e=jnp.exp(wend - a + ln_ks), urow=urow, rcol=gc[:, lr:lr + 1],
                di=jnp.exp(m_old - urow), em=jnp.exp(-(brow + urow)), m_new=m_new,
                dec=jnp.exp(g_tot + m_old - m_new), inp=jnp.exp(a - m_new)))
    for t in st:
        sl = t["sl"]
        q = t["q_ref"][:, sl].astype(BF16)
        k = t["k_ref"][:, sl].astype(BF16)
        v_t = t["v_ref"][:, sl].T
        t["vext"] = jnp.concatenate([v_t.astype(BF16), ones_rows], axis=0)
        vext_e = jnp.concatenate([(v_t * t["e"]).astype(BF16), jnp.broadcast_to(t["e"], ones_rows.shape).astype(BF16)],
                                 axis=0)
        t["sT"] = lax.dot_general(k, q, nt, preferred_element_type=F32)
        t["d_c"] = jnp.dot(vext_e, k, preferred_element_type=F32)
        t["qc"] = lax.dot_general(c_ref[t["idx"]].astype(BF16), q, nt, preferred_element_type=F32)
    for t in st:
        dexp = jnp.exp(jnp.where(t["mask"], t["rcol"] - t["urow"] + ln_ks, -jnp.inf))
        t["p"] = (t["sT"] * dexp).astype(BF16)
    for t in st:
        res = t["di"] * t["qc"] + jnp.dot(t["vext"], t["p"], preferred_element_type=F32)
        den = res[ML_DQK:ML_DQK + 1, :]
        h_t = res[0:ML_DQK, :] / jnp.maximum(jnp.abs(den), t["em"])
        t["o_ref"][:, t["sl"]] = h_t.T
        c_ref[t["idx"]] = t["dec"] * c_ref[t["idx"]] + t["inp"] * t["d_c"]
        m_ref[t["idx"]] = jnp.broadcast_to(t["m_new"], (SUBLANE, LANE))


def _mlstm(z, gc, gr, b, s, lc):
    r = z.shape[0]
    L = ML_CHUNK
    nxc, ncc = s // L, lc // L
    x0 = lambda bb: bb * nxc
    c0 = lambda bb: b * nxc + bb * ncc

    def blk_f(bb, st):
        return jnp.where(st < ncc, c0(bb) + st, x0(bb) + st - ncc)

    def blk_b(bb, st):
        return jnp.where(st < ncc, c0(bb) + ncc - 1 - st, x0(bb) + nxc - 1 - (st - ncc))

    def zspec(blk, cb):
        return pl.BlockSpec((L, WIDTH), lambda bb, st: (blk(bb, st), cb))

    def dir_specs(blk):
        return [zspec(blk, Z_MQ // WIDTH), zspec(blk, Z_MK // WIDTH), zspec(blk, Z_MV // WIDTH),
                pl.BlockSpec((L, LANE), lambda bb, st: (blk(bb, st), 0)),
                pl.BlockSpec((1, LANE, L), lambda bb, st: (blk(bb, st), 0, 0))]

    return pl.pallas_call(
        _mlstm_kernel,
        grid=(b, nxc + ncc),
        in_specs=dir_specs(blk_f) + dir_specs(blk_b),
        out_specs=[pl.BlockSpec((L, WIDTH), lambda bb, st: (blk_f(bb, st), 0)),
                   pl.BlockSpec((L, WIDTH), lambda bb, st: (blk_b(bb, st), 0))],
        out_shape=[jax.ShapeDtypeStruct((r, WIDTH), F32), jax.ShapeDtypeStruct((r, WIDTH), F32)],
        scratch_shapes=[pltpu.VMEM((2 * HEADS, ML_ROWS, ML_DQK), F32),
                        pltpu.VMEM((2 * HEADS, SUBLANE, LANE), F32)],
        compiler_params=_cparams(("arbitrary", "arbitrary"), VMEM_LIMIT),
        name="mlstm",
    )(z, z, z, gc, gr, z, z, z, gc, gr)


def _rg_ab_kernel(cur_ref, prev_ref, next_ref, cw_ref, cb_ref, w_ref, ba_ref, bx_ref, lam_ref,
                  af_ref, bf_ref, ab_ref, bb_ref, xe_ref, *, tm, s_len, c_len, rows_x):
    row0 = pl.program_id(0) * tm
    in_x = row0 < rows_x
    seq = jnp.where(in_x, s_len, c_len)
    off = jnp.where(in_x, row0, row0 - rows_x)
    first = lax.rem(off, seq) == 0
    last = lax.rem(off + tm, seq) == 0
    xe_ref[0:SUBLANE, :] = jnp.where(first, 0.0, prev_ref[...])
    xe_ref[SUBLANE:SUBLANE + tm, :] = cur_ref[...]
    xe_ref[SUBLANE + tm:2 * SUBLANE + tm, :] = jnp.where(last, 0.0, next_ref[...])
    lp = RG_CONV // 2
    xc = cb_ref[...] + cw_ref[0:1, :] * xe_ref[pl.ds(SUBLANE - lp, tm), :]
    for j in range(1, RG_CONV):
        xc = xc + cw_ref[j:j + 1, :] * xe_ref[pl.ds(SUBLANE - lp + j, tm), :]
    outs = ((af_ref, bf_ref), (ab_ref, bb_ref))
    for g in range(RG_BLOCKS):
        sl = slice(g * RG_BW, (g + 1) * RG_BW)
        xg = xc[:, sl]
        o = jnp.dot(xg.astype(BF16), w_ref[g], preferred_element_type=F32)
        for d in range(2):
            r = _sigmoid(o[:, (2 * d) * RG_BW:(2 * d + 1) * RG_BW] + ba_ref[d:d + 1, sl])
            i = _sigmoid(o[:, (2 * d + 1) * RG_BW:(2 * d + 2) * RG_BW] + bx_ref[d:d + 1, sl])
            log_a = (-RG_C) * r * _softplus(-lam_ref[d:d + 1, sl])
            a = jnp.exp(log_a)
            one_m_a2 = -jnp.tanh(log_a) * (a * a + 1.0)
            outs[d][0][:, sl] = a
            outs[d][1][:, sl] = jnp.sqrt(one_m_a2) * (i * xg)


def _rg_coeffs(z, cw, cb, w_rg, ba, bx, lam, b, s, lc, tm):
    r = z.shape[0]
    per = tm // SUBLANE
    nblk8 = r // SUBLANE
    cbk = Z_RX // WIDTH
    full = lambda shape: pl.BlockSpec(shape, lambda i: (0,) * len(shape))
    kernel = functools.partial(_rg_ab_kernel, tm=tm, s_len=s, c_len=lc, rows_x=b * s)
    o_spec = pl.BlockSpec((tm, WIDTH), lambda i: (i, 0))
    o_shape = jax.ShapeDtypeStruct((r, WIDTH), F32)
    return pl.pallas_call(
        kernel,
        grid=(r // tm,),
        in_specs=[pl.BlockSpec((tm, WIDTH), lambda i: (i, cbk)),
                  pl.BlockSpec((SUBLANE, WIDTH), lambda i: (jnp.maximum(i * per - 1, 0), cbk)),
                  pl.BlockSpec((SUBLANE, WIDTH), lambda i: (jnp.minimum((i + 1) * per, nblk8 - 1), cbk)),
                  full((RG_CONV, WIDTH)), full((1, WIDTH)), full(w_rg.shape),
                  full((2, WIDTH)), full((2, WIDTH)), full((2, WIDTH))],
        out_specs=[o_spec] * 4,
        out_shape=[o_shape] * 4,
        scratch_shapes=[pltpu.VMEM((tm + 2 * SUBLANE, WIDTH), F32)],
        compiler_params=_cparams(("parallel",), VMEM_LIMIT),
        name="rglru_coeffs",
    )(z, z, z, cw, cb.reshape(1, WIDTH), w_rg, ba, bx, lam)


def _rg_scan_kernel(af_ref, bf_ref, ab_ref, bb_ref, hf_ref, hb_ref, sf_ref, sb_ref, *, tt):
    @pl.when(pl.program_id(1) == 0)
    def _():
        sf_ref[...] = jnp.zeros(sf_ref.shape, F32)
        sb_ref[...] = jnp.zeros(sb_ref.shape, F32)

    def body(t, carry):
        hf, hb = carry
        hf = af_ref[pl.ds(t, 1), :] * hf + bf_ref[pl.ds(t, 1), :]
        hf_ref[pl.ds(t, 1), :] = hf
        tb = tt - 1 - t
        hb = ab_ref[pl.ds(tb, 1), :] * hb + bb_ref[pl.ds(tb, 1), :]
        hb_ref[pl.ds(tb, 1), :] = hb
        return hf, hb

    hf, hb = lax.fori_loop(0, tt, body, (sf_ref[...], sb_ref[...]), unroll=8)
    sf_ref[...] = hf
    sb_ref[...] = hb


def _rg_scan(af, bf, ab, bb, b, s, lc, tt):
    r = af.shape[0]
    nxt, nct = s // tt, lc // tt
    x0 = lambda bi: bi * nxt
    c0 = lambda bi: b * nxt + bi * nct

    def blk_f(bi, st):
        return jnp.where(st < nct, c0(bi) + st, x0(bi) + st - nct)

    def blk_b(bi, st):
        return jnp.where(st < nct, c0(bi) + nct - 1 - st, x0(bi) + nxt - 1 - (st - nct))

    sf = pl.BlockSpec((tt, WIDTH), lambda bi, st: (blk_f(bi, st), 0))
    sb = pl.BlockSpec((tt, WIDTH), lambda bi, st: (blk_b(bi, st), 0))
    o_shape = jax.ShapeDtypeStruct((r, WIDTH), F32)
    return pl.pallas_call(
        functools.partial(_rg_scan_kernel, tt=tt),
        grid=(b, nxt + nct),
        in_specs=[sf, sf, sb, sb],
        out_specs=[sf, sb],
        out_shape=[o_shape, o_shape],
        scratch_shapes=[pltpu.VMEM((1, WIDTH), F32), pltpu.VMEM((1, WIDTH), F32)],
        compiler_params=_cparams(("arbitrary", "arbitrary")),
        name="rglru_scan",
    )(af, bf, ab, bb)


def _finish_kernel(mhf_ref, mhb_ref, zo_ref, ng_ref, rhf_ref, rhb_ref, zg_ref, yb_ref, yr_ref):
    hsum = mhf_ref[...] + mhb_ref[...]
    for h in range(HEADS):
        sl = slice(h * V_HEAD, (h + 1) * V_HEAD)
        hn = _rms(hsum[:, sl], ng_ref[:, sl])
        yb_ref[:, sl] = (hn * _sigmoid(zo_ref[:, sl])).astype(BF16)
    yr_ref[...] = ((rhf_ref[...] + rhb_ref[...]) * _gelu_tanh(zg_ref[...])).astype(BF16)


def _finish(rows, mhf, mhb, z, ng, rhf, rhb, tm):
    wide = lambda: pl.BlockSpec((tm, WIDTH), lambda i: (i, 0))
    zblk = lambda cb: pl.BlockSpec((tm, WIDTH), lambda i: (i, cb))
    o_shape = jax.ShapeDtypeStruct((rows, WIDTH), BF16)
    return pl.pallas_call(
        _finish_kernel,
        grid=(rows // tm,),
        in_specs=[wide(), wide(), zblk(Z_MO // WIDTH), pl.BlockSpec((1, WIDTH), lambda i: (0, 0)),
                  wide(), wide(), zblk(Z_RG // WIDTH)],
        out_specs=[wide(), wide()],
        out_shape=[o_shape, o_shape],
        compiler_params=_cparams(("parallel",), VMEM_LIMIT),
        name="branch_finish",
    )(mhf, mhb, z, ng.reshape(1, WIDTH), rhf, rhb, z)


def _merge_kernel(ya_ref, yb_ref, yr_ref, gm0_ref, gm1_ref, gm2_ref, w0_ref, w1_ref, w2_ref, m_ref):
    m = (gm0_ref[...].astype(F32) * jnp.dot(ya_ref[...], w0_ref[0], preferred_element_type=F32)
         + gm1_ref[...].astype(F32) * jnp.dot(yb_ref[...], w1_ref[0], preferred_element_type=F32)
         + gm2_ref[...].astype(F32) * jnp.dot(yr_ref[...], w2_ref[0], preferred_element_type=F32))
    m_ref[...] = m.astype(BF16)


def _merge(rows, d, ya, yb, yr, gm, wb, tm):
    tn = _pick(d, (1024, 512, 256, 128))
    wide = lambda: pl.BlockSpec((tm, WIDTH), lambda j, i: (i, 0))
    gate = lambda br: pl.BlockSpec((tm, tn), lambda j, i: (i, (br * d) // tn + j))
    wspec = lambda br: pl.BlockSpec((1, WIDTH, tn), lambda j, i: (br, 0, j))
    return pl.pallas_call(
        _merge_kernel,
        grid=(d // tn, rows // tm),
        in_specs=[wide(), wide(), wide(), gate(0), gate(1), gate(2), wspec(0), wspec(1), wspec(2)],
        out_specs=pl.BlockSpec((tm, tn), lambda j, i: (i, j)),
        out_shape=jax.ShapeDtypeStruct((rows, d), BF16),
        compiler_params=_cparams(("parallel", "parallel"), VMEM_LIMIT),
        name="branch_merge",
    )(ya, yb, yr, gm, gm, gm, wb, wb, wb)


def _outproj_kernel(m_ref, w_ref, x_ref, gate_ref, o_ref):
    o_ref[...] = x_ref[...] + gate_ref[0] * jnp.dot(m_ref[...], w_ref[...], preferred_element_type=F32)


def _out_projection(rows, src, m, w_out, mods3, mod_row, span, prev):
    tm, blk0, nblk = span
    xs, src0 = src
    d = xs.shape[1]
    tn = _pick(d, (1024, 512, 256, 128))
    return _pcall(
        _outproj_kernel,
        prev=prev,
        grid=(nblk, d // tn),
        in_specs=[pl.BlockSpec((tm, d), lambda i, j: (blk0 + i, 0)),
                  pl.BlockSpec((d, tn), lambda i, j: (0, j)),
                  pl.BlockSpec((tm, tn), lambda i, j: (src0 + i, j)),
                  pl.BlockSpec((1, 1, tn), lambda i, j: (mod_row((blk0 + i) * tm), 0, (2 * d) // tn + j))],
        args=[m, w_out, xs, mods3],
        out_specs=pl.BlockSpec((tm, tn), lambda i, j: (blk0 + i, j)),
        out_shape=jax.ShapeDtypeStruct((rows, d), F32),
        compiler_params=_cparams(("parallel", "parallel"), VMEM_LIMIT),
        name="out_proj",
    )


def _mlp_kernel(x_ref, mod_ref, g_ref, w1_ref, w2_ref, fg_ref, o_ref, xn_ref, *, d, final):
    j = pl.program_id(1)

    @pl.when(j == 0)
    def _():
        y = _rms(x_ref[...], g_ref[...])
        shift = mod_ref[0, :, 3 * d:4 * d]
        scale = mod_ref[0, :, 4 * d:5 * d]
        xn_ref[...] = (y * (1.0 + scale) + shift).astype(BF16)
        o_ref[...] = jnp.zeros(o_ref.shape, F32)

    h = jnp.dot(xn_ref[...], w1_ref[...], preferred_element_type=F32)
    h = jnp.square(jnp.maximum(h, 0.0)).astype(BF16)
    o_ref[...] += jnp.dot(h, w2_ref[...], preferred_element_type=F32)

    @pl.when(j == pl.num_programs(1) - 1)
    def _():
        out = x_ref[...] + mod_ref[0, :, 5 * d:6 * d] * o_ref[...]
        if final:
            out = _rms(out, fg_ref[...])
        o_ref[...] = out


def _mlp(rows, x1, mods3, g, w1, w2, fg, mod_row, span, prev, final):
    tm, blk0, nblk = span
    d = x1.shape[1]
    dff = w1.shape[1]
    tf = _pick(dff, (512, 256, 128))
    return _pcall(
        functools.partial(_mlp_kernel, d=d, final=final),
        prev=prev,
        grid=(nblk, dff // tf),
        in_specs=[pl.BlockSpec((tm, d), lambda i, j: (blk0 + i, 0)),
                  pl.BlockSpec((1, 1, N_MOD * d), lambda i, j: (mod_row((blk0 + i) * tm), 0, 0)),
                  pl.BlockSpec((1, d), lambda i, j: (0, 0)),
                  pl.BlockSpec((d, tf), lambda i, j: (0, j)),
                  pl.BlockSpec((tf, d), lambda i, j: (j, 0)),
                  pl.BlockSpec((1, d), lambda i, j: (0, 0))],
        args=[x1, mods3, g.reshape(1, d), w1, w2, fg.reshape(1, d)],
        out_specs=pl.BlockSpec((tm, d), lambda i, j: (blk0 + i, 0)),
        out_shape=jax.ShapeDtypeStruct((rows, d), F32),
        scratch_shapes=[pltpu.VMEM((tm, d), BF16)],
        compiler_params=_cparams(("parallel", "arbitrary"), VMEM_LIMIT),
        name="mlp",
    )


def _rope_perm_cols(w):
    n = QK_ROPE // 4
    return jnp.concatenate([-w[..., n:2 * n], w[..., 0:n], -w[..., 3 * n:4 * n], w[..., 2 * n:3 * n]], axis=-1)


def _layer_weights(l, d, w_in, w_uq, w_ukv, mlstm_gate_b, rg_wa, rg_wx, w_branch, w_out, w_mlp1, w_mlp2):
    wi = w_in[l].astype(BF16)
    o = 0
    parts = {}
    for name, wdt in (("mla_q", Q_LORA), ("mla_kv", KV_LORA), ("mla_kr", QK_ROPE), ("ml_q", WIDTH),
                      ("ml_k", WIDTH), ("ml_v", WIDTH), ("ml_o", WIDTH), ("ml_g", 4 * HEADS),
                      ("rg_x", WIDTH), ("rg_gate", WIDTH), ("merge", 3 * d)):
        parts[name] = wi[:, o:o + wdt]
        o += wdt
    g_pad = jnp.pad(parts["ml_g"], ((0, 0), (0, LANE - 4 * HEADS)))
    w_in_p = jnp.concatenate(
        [parts["mla_q"], parts["mla_kv"], parts["mla_kr"], _rope_perm_cols(parts["mla_kr"]), g_pad,
         parts["ml_q"], parts["ml_k"], parts["ml_v"], parts["ml_o"], parts["rg_x"], parts["rg_gate"],
         parts["merge"]], axis=1)
    wq = w_uq[l].reshape(Q_LORA, HEADS, QK_DIM)
    wqt = jnp.transpose(wq, (1, 2, 0)).reshape(HEADS * QK_DIM, Q_LORA).astype(BF16)
    wqrt = jnp.transpose(_rope_perm_cols(wq[:, :, QK_NOPE:]), (1, 2, 0)).reshape(HEADS * QK_ROPE, Q_LORA).astype(BF16)
    wkv = w_ukv[l].reshape(KV_LORA, HEADS, QK_NOPE + V_HEAD)
    wk = wkv[:, :, :QK_NOPE].reshape(KV_LORA, HEADS * QK_NOPE).astype(BF16)
    wvt = jnp.transpose(wkv[:, :, QK_NOPE:], (1, 2, 0)).reshape(HEADS * V_HEAD, KV_LORA).astype(BF16)
    gate_b = jnp.pad(mlstm_gate_b[l].reshape(1, 4 * HEADS), ((0, 0), (0, LANE - 4 * HEADS)))
    w_rg = jnp.concatenate([rg_wa[l, 0], rg_wx[l, 0], rg_wa[l, 1], rg_wx[l, 1]], axis=-1).astype(BF16)
    return dict(w_in=w_in_p, wqt=wqt, wqrt=wqrt, wk=wk, wvt=wvt, gate_b=gate_b, w_rg=w_rg,
                wb=w_branch[l].astype(BF16), w_out=w_out[l].astype(BF16),
                w1=w_mlp1[l].astype(BF16), w2=w_mlp2[l].astype(BF16))


def _rope_tables(b, s, lc):
    t = jnp.arange(s, dtype=jnp.int32)
    row = (t // GRID_W).astype(F32)
    col = (t % GRID_W).astype(F32)
    n_freq = QK_ROPE // 4
    inv = ROPE_BASE ** (-jnp.arange(n_freq, dtype=F32) / n_freq)
    ang_r = row[:, None] * inv[None, :]
    ang_c = col[:, None] * inv[None, :]
    cos = jnp.concatenate([jnp.cos(ang_r)] * 2 + [jnp.cos(ang_c)] * 2, axis=1)
    sin = jnp.concatenate([jnp.sin(ang_r)] * 2 + [jnp.sin(ang_c)] * 2, axis=1)
    cos_u = jnp.concatenate([jnp.tile(cos, (b, 1)), jnp.ones((b * lc, QK_ROPE), F32)], axis=0)
    sin_u = jnp.concatenate([jnp.tile(sin, (b, 1)), jnp.zeros((b * lc, QK_ROPE), F32)], axis=0)
    return cos_u, sin_u, cos_u.T, sin_u.T


def kernel(x, c, ctx, c_ctx, norm1_g, norm2_g, w_mod, b_mod, w_in, q_norm_g, w_uq, kv_norm_g, w_ukv,
           mlstm_gate_b, mlstm_norm_g, rg_conv_w, rg_conv_b, rg_wa, rg_ba, rg_wx, rg_bx, rg_lam,
           w_branch, w_out, w_mlp1, w_mlp2, final_g):
    b, s, d = x.shape
    lc = ctx.shape[1]
    depth = w_in.shape[0]
    rows_x, rows_c = b * s, b * lc
    r = rows_x + rows_c

    tm_x = _pick(s, (1024, 512, 256, 128))
    tm_c = _pick(math.gcd(rows_x, rows_c), (512, 256, 128))
    tm_mid = _pick(math.gcd(rows_x, rows_c), (512, 256, 128))
    tm_seq = _pick(math.gcd(s, lc), (256, 128))
    tq = _pick(s, (1024, 512, 256, 128))
    span_x = (tm_x, 0, rows_x // tm_x)
    span_c = (tm_c, rows_x // tm_c, rows_c // tm_c)

    def mod_row(row0):
        return jnp.where(row0 < rows_x, 1 + row0 // s, 0)

    def dense(fn, with_ctx):
        out = fn(span_x, src_x, None)
        return fn(span_c, src_c, out) if with_ctx else out

    src_x = (x.reshape(rows_x, d), 0)
    src_c = (ctx.reshape(rows_c, d), 0)
    cc = jnp.concatenate([c_ctx[None, :], c, jnp.zeros((SUBLANE - 1 - b, d), F32)], axis=0)
    tabs = _rope_tables(b, s, lc)

    for l in range(depth):
        last = l == depth - 1
        w = _layer_weights(l, d, w_in, w_uq, w_ukv, mlstm_gate_b, rg_wa, rg_wx, w_branch, w_out,
                           w_mlp1, w_mlp2)
        mods3 = _modulation(cc, w_mod, b_mod, l).reshape(SUBLANE, 1, N_MOD * d)
        z, gm = dense(lambda span, src, prev: _in_projection(r, src, mods3, norm1_g[l], w["w_in"], mod_row,
                                                             span, prev), True)

        qt, kk, vt = _mla_prep(z, tabs, q_norm_g[l], kv_norm_g[l], w["wqt"], w["wqrt"], w["wk"],
                               w["wvt"], b, s, lc, tm_seq)
        ya = _attn_call(qt, kk, vt, b=b, nq=s // tq, tq=tq, q_blk0=0, kv_len=s + lc, ctx_len=lc, kv_blk0=0,
                        out_rows=r, row_blk=lambda bb, i: bb * (s // tq) + i, prev=None)
        if not last:
            ya = _attn_call(qt, kk, vt, b=b, nq=1, tq=lc, q_blk0=s // lc, kv_len=lc, ctx_len=lc, kv_blk0=s // lc,
                            out_rows=r, row_blk=lambda bb, i: rows_x // lc + bb, prev=ya)

        gc, gr = _mlstm_gates(z, w["gate_b"])
        mhf, mhb = _mlstm(z, gc, gr, b, s, lc)

        af, bf, ab, bb_ = _rg_coeffs(z, rg_conv_w[l], rg_conv_b[l], w["w_rg"], rg_ba[l], rg_bx[l],
                                     rg_lam[l], b, s, lc, tm_seq)
        rhf, rhb = _rg_scan(af, bf, ab, bb_, b, s, lc, tm_seq)

        rows = rows_x if last else r
        yb, yr = _finish(rows, mhf, mhb, z, mlstm_norm_g[l], rhf, rhb, tm_mid)
        m = _merge(rows, d, ya, yb, yr, gm, w["wb"], tm_mid)
        x1 = dense(lambda span, src, prev: _out_projection(rows, src, m, w["w_out"], mods3, mod_row, span,
                                                           prev), not last)
        xu = dense(lambda span, src, prev: _mlp(rows, x1, mods3, norm2_g[l], w["w1"], w["w2"], final_g,
                                                mod_row, span, prev, last), not last)
        src_x = (xu, span_x[1])
        src_c = (xu, span_c[1])

    return xu.reshape(b, s, d)
```
